```python
import math
import jax
import jax.numpy as jnp
from jax import lax
import numpy as np

D_MODEL = 1024
BATCH = 8
SEQ = 2048
DEPTH = 2

GRID_W = 64
CTX_LEN = 256
HEAD_DIM = 64
A_HEADS = 6
A_KV_HEADS = 2
WINDOW = 128
WIN_BLK = 128
B_HEADS = 4
B_QK_DIM = 32
C_HEADS = 6
NA_KH = 8
NA_KW = 16
Q_BLK = 128
N_MOD = 6
D_FF_RAW = -(-8 * D_MODEL // 3)
D_FF = -(-D_FF_RAW // 256) * 256
MIX_WIDTH = (A_HEADS + B_HEADS + C_HEADS) * HEAD_DIM
A_Q_W = A_HEADS * HEAD_DIM
A_KV_W = A_KV_HEADS * HEAD_DIM
B_QK_W = B_HEADS * 2 * B_QK_DIM
B_V_W = B_HEADS * HEAD_DIM
C_W = C_HEADS * HEAD_DIM
PROJ_WIDTH = A_Q_W + 2 * A_KV_W + 2 * B_QK_W + B_V_W + 3 * C_W
ROPE_BASE = 10000.0
LN_EPS = 1e-5
NEG_INF = -1e30

kernel_name = "hybrid_dit_parallel_head_groups"


def layer_norm(x, g, b):
    xf = x.astype(jnp.float32)
    mu = jnp.mean(xf, axis=-1, keepdims=True)
    var = jnp.mean(jnp.square(xf - mu), axis=-1, keepdims=True)
    return ((xf - mu) * lax.rsqrt(var + LN_EPS) * g + b).astype(x.dtype)


def rms_norm(x, g):
    xf = x.astype(jnp.float32)
    return (xf * lax.rsqrt(jnp.mean(jnp.square(xf), axis=-1, keepdims=True) + LN_EPS) * g).astype(x.dtype)


def modulate(t, shift, scale):
    return t * (1.0 + scale) + shift


def post_norm(res, y, gate, g, b, alpha):
    return layer_norm(alpha * res + gate * y, g, b)


def swiglu(h, w_in_l, w_out_l):
    gate, up = jnp.split(h @ w_in_l, 2, axis=-1)
    return (jax.nn.silu(gate) * up) @ w_out_l


def rope_1d(x, pos):
    half = x.shape[-1] // 2
    inv = ROPE_BASE ** (-jnp.arange(half, dtype=jnp.float32) / half)
    ang = pos.astype(jnp.float32)[:, None] * inv[None, :]
    cos = jnp.cos(ang)[:, None, :]
    sin = jnp.sin(ang)[:, None, :]
    x1 = x[..., :half].astype(jnp.float32)
    x2 = x[..., half:].astype(jnp.float32)
    return jnp.concatenate([x1 * cos - x2 * sin, x2 * cos + x1 * sin], axis=-1).astype(x.dtype)


def rope_2d(x, rows, cols):
    d = x.shape[-1]
    return jnp.concatenate([rope_1d(x[..., : d // 2], rows), rope_1d(x[..., d // 2:], cols)], axis=-1)


def split_proj(p):
    B, L, _ = p.shape
    widths = (A_Q_W, A_KV_W, A_KV_W, B_QK_W, B_QK_W, B_V_W, C_W, C_W, C_W)
    offs = [sum(widths[:i]) for i in range(1, len(widths))]
    aq, ak, av, bq, bk, bv, cq, ck, cv = jnp.split(p, offs, axis=-1)
    return (aq.reshape(B, L, A_HEADS, HEAD_DIM),
            ak.reshape(B, L, A_KV_HEADS, HEAD_DIM),
            av.reshape(B, L, A_KV_HEADS, HEAD_DIM),
            bq.reshape(B, L, B_HEADS, 2, B_QK_DIM),
            bk.reshape(B, L, B_HEADS, 2, B_QK_DIM),
            bv.reshape(B, L, B_HEADS, HEAD_DIM),
            cq.reshape(B, L, C_HEADS, HEAD_DIM),
            ck.reshape(B, L, C_HEADS, HEAD_DIM),
            cv.reshape(B, L, C_HEADS, HEAD_DIM))


def context_attention(q, k, v, sink=None):
    B, C, HQ, d = q.shape
    G = k.shape[2]
    R = HQ // G
    qg = q.reshape(B, C, G, R, d)
    s = jnp.einsum('bqgrd,bkgd->bgrqk', qg, k).astype(jnp.float32) * (d ** -0.5)
    if sink is not None:
        sk = jnp.broadcast_to(sink.reshape(G, R)[None, :, :, None, None].astype(jnp.float32), (B, G, R, C, 1))
        s = jnp.concatenate([s, sk], axis=-1)
    p = jax.nn.softmax(s, axis=-1)[..., :C]
    return jnp.einsum('bgrqk,bkgd->bqgrd', p, v).reshape(B, C, HQ * d).astype(q.dtype)


def window_gqa(q, k, v, kc, vc, sink):
    B, L, HQ, d = q.shape
    G = A_KV_HEADS
    R = HQ // G
    nb = L // WIN_BLK
    C = kc.shape[1]
    qb = q.reshape(B, nb, WIN_BLK, G, R, d)
    pad = ((0, 0), (WIN_BLK, WIN_BLK), (0, 0), (0, 0))
    kp = jnp.pad(k, pad)
    vp = jnp.pad(v, pad)

    def bands(t):
        return jnp.concatenate([t[:, j * WIN_BLK: j * WIN_BLK + L].reshape(B, nb, WIN_BLK, G, d) for j in range(3)], axis=2)

    kb, vb = bands(kp), bands(vp)
    scale = d ** -0.5
    s_win = jnp.einsum('bnqgrd,bnkgd->bgrnqk', qb, kb).astype(jnp.float32) * scale
    s_ctx = jnp.einsum('bnqgrd,bcgd->bgrnqc', qb, kc).astype(jnp.float32) * scale
    blk = jnp.arange(nb, dtype=jnp.int32)[:, None] * WIN_BLK
    qpos = blk + jnp.arange(WIN_BLK, dtype=jnp.int32)[None, :]
    kpos = blk - WIN_BLK + jnp.arange(3 * WIN_BLK, dtype=jnp.int32)[None, :]
    valid = ((jnp.abs(qpos[:, :, None] - kpos[:, None, :]) <= WINDOW)
             & (kpos >= 0)[:, None, :] & (kpos < L)[:, None, :])
    s_win = jnp.where(valid, s_win, NEG_INF)
    sk = jnp.broadcast_to(sink.reshape(G, R)[None, :, :, None, None, None].astype(jnp.float32), (B, G, R, nb, WIN_BLK, 1))
    p = jax.nn.softmax(jnp.concatenate([s_win, s_ctx, sk], axis=-1), axis=-1)
    pw = p[..., : 3 * WIN_BLK]
    pc = p[..., 3 * WIN_BLK: 3 * WIN_BLK + C]
    out = jnp.einsum('bgrnqk,bnkgd->bnqgrd', pw, vb) + jnp.einsum('bgrnqc,bcgd->bnqgrd', pc, vc)
    return out.reshape(B, L, HQ * d).astype(q.dtype)


def diff_core(q, k, v, lam):
    s = jnp.einsum('bqhmd,bkhmd->bhmqk', q, k).astype(jnp.float32) * (q.shape[-1] ** -0.5)
    p = jax.nn.softmax(s, axis=-1)
    a = p[:, :, 0] - lam * p[:, :, 1]
    return jnp.einsum('bhqk,bkhd->bqhd', a, v)


def diff_head_norm(o, g, lam_init):
    B, L = o.shape[:2]
    return (rms_norm(o, g) * (1.0 - lam_init)).reshape(B, L, B_HEADS * HEAD_DIM)


def diff_attention_latent(q, k, v, kc, vc, lam, g, lam_init):
    B, L = q.shape[:2]
    nb = L // Q_BLK
    k_all = jnp.concatenate([kc, k], axis=1)
    v_all = jnp.concatenate([vc, v], axis=1)
    qb = jnp.swapaxes(q.reshape(B, nb, Q_BLK, B_HEADS, 2, B_QK_DIM), 0, 1)
    ob = lax.map(lambda qblk: diff_core(qblk, k_all, v_all, lam), qb)
    o = jnp.swapaxes(ob, 0, 1).reshape(B, L, B_HEADS, HEAD_DIM)
    return diff_head_norm(o, g, lam_init)


def neighbourhood_attention(q, k, v, kc, vc, rel_bias):
    B, L, H, d = q.shape
    W = GRID_W
    R = L // W
    KH = min(NA_KH, R)
    qg, kg, vg = (t.reshape(B, R, W, H, d) for t in (q, k, v))
    r = jnp.arange(R, dtype=jnp.int32)
    rs = jnp.clip(r - KH // 2, 0, R - KH)
    row_idx = rs[:, None] + jnp.arange(KH, dtype=jnp.int32)[None, :]
    kr = kg[:, row_idx]
    vr = vg[:, row_idx]
    cq = jnp.arange(W, dtype=jnp.int32)
    cs = jnp.clip(cq - NA_KW // 2, 0, W - NA_KW)
    col_valid = (cq[None, :] >= cs[:, None]) & (cq[None, :] < cs[:, None] + NA_KW)
    dr = row_idx - r[:, None] + (NA_KH - 1)
    dc = jnp.clip(cq[None, :] - cq[:, None], -(NA_KW - 1), NA_KW - 1) + (NA_KW - 1)
    bias = rel_bias[:, dr[:, None, :, None], dc[None, :, None, :]].astype(jnp.float32)
    scale = d ** -0.5
    s_win = jnp.einsum('brwhd,brkvhd->bhrwkv', qg, kr).astype(jnp.float32) * scale + bias
    s_win = jnp.where(col_valid[:, None, :], s_win, NEG_INF).reshape(B, H, R, W, KH * W)
    s_ctx = jnp.einsum('brwhd,bchd->bhrwc', qg, kc).astype(jnp.float32) * scale
    p = jax.nn.softmax(jnp.concatenate([s_win, s_ctx], axis=-1), axis=-1)
    pw = p[..., : KH * W].reshape(B, H, R, W, KH, W)
    pc = p[..., KH * W:]
    out = jnp.einsum('bhrwkv,brkvhd->brwhd', pw, vr) + jnp.einsum('bhrwc,bchd->brwhd', pc, vc)
    return out.reshape(B, L, H * d).astype(q.dtype)


def setup_inputs(seed: int = 0) -> dict:
    key = jax.random.key(seed)
    ks = jax.random.split(key, 24)
    f32 = jnp.float32
    beta = (8.0 * DEPTH) ** -0.25

    def nrm(k, shape, scale):
        return jax.random.normal(k, shape, f32) * scale

    return {
        "x": nrm(ks[0], (BATCH, SEQ, D_MODEL), 1.0),
        "c": nrm(ks[1], (BATCH, D_MODEL), 1.0),
        "ctx": nrm(ks[2], (BATCH, CTX_LEN, D_MODEL), 1.0),
        "c_ctx": nrm(ks[3], (D_MODEL,), 1.0),
        "w_ada": nrm(ks[4], (DEPTH, D_MODEL, N_MOD * D_MODEL), 0.5 * D_MODEL ** -0.5),
        "b_ada": nrm(ks[5], (DEPTH, N_MOD * D_MODEL), 0.02),
        "w_in": nrm(ks[6], (DEPTH, D_MODEL, PROJ_WIDTH), D_MODEL ** -0.5),
        "w_o": nrm(ks[7], (DEPTH, MIX_WIDTH, D_MODEL), beta * MIX_WIDTH ** -0.5),
        "sink": nrm(ks[8], (DEPTH, A_HEADS), 0.5),
        "lam_q1": nrm(ks[9], (DEPTH, B_QK_DIM), 0.1),
        "lam_k1": nrm(ks[10], (DEPTH, B_QK_DIM), 0.1),
        "lam_q2": nrm(ks[11], (DEPTH, B_QK_DIM), 0.1),
        "lam_k2": nrm(ks[12], (DEPTH, B_QK_DIM), 0.1),
        "subln_g": 1.0 + nrm(ks[13], (DEPTH, HEAD_DIM), 0.02),
        "na_bias": nrm(ks[14], (DEPTH, C_HEADS, 2 * NA_KH - 1, 2 * NA_KW - 1), 0.1),
        "ln1_g": 1.0 + nrm(ks[15], (DEPTH, D_MODEL), 0.02),
        "ln1_b": nrm(ks[16], (DEPTH, D_MODEL), 0.02),
        "w_ffn_in": nrm(ks[17], (DEPTH, D_MODEL, 2 * D_FF), D_MODEL ** -0.5),
        "w_ffn_out": nrm(ks[18], (DEPTH, D_FF, D_MODEL), beta * D_FF ** -0.5),
        "ln2_g": 1.0 + nrm(ks[19], (DEPTH, D_MODEL), 0.02),
        "ln2_b": nrm(ks[20], (DEPTH, D_MODEL), 0.02),
    }


def reference(x, c, ctx, c_ctx, w_ada, b_ada, w_in, w_o, sink, lam_q1, lam_k1, lam_q2, lam_k2,
              subln_g, na_bias, ln1_g, ln1_b, w_ffn_in, w_ffn_out, ln2_g, ln2_b):
    B, L, _ = x.shape
    tpos = jnp.arange(L, dtype=jnp.int32)
    rows = tpos // GRID_W
    cols = tpos % GRID_W
    alpha = (2.0 * DEPTH) ** 0.25
    sc = jax.nn.silu(c)
    scc = jax.nn.silu(c_ctx)
    xs, cs = x, ctx
    for l in range(DEPTH):
        last = l == DEPTH - 1
        mx = (sc @ w_ada[l] + b_ada[l])[:, None, :]
        mc = scc @ w_ada[l] + b_ada[l]
        x_sh1, x_sc1, x_g1, x_sh2, x_sc2, x_g2 = jnp.split(mx, N_MOD, axis=-1)
        c_sh1, c_sc1, c_g1, c_sh2, c_sc2, c_g2 = jnp.split(mc, N_MOD, axis=-1)
        hx = modulate(xs, x_sh1, x_sc1)
        hc = modulate(cs, c_sh1, c_sc1)
        aq, ak, av, bq, bk, bv, cq, ck, cv = split_proj(hx @ w_in[l])
        aqc, akc, avc, bqc, bkc, bvc, cqc, ckc, cvc = split_proj(hc @ w_in[l])
        aq = rope_2d(aq, rows, cols)
        ak = rope_2d(ak, rows, cols)
        bq = rope_2d(bq.reshape(B, L, 2 * B_HEADS, B_QK_DIM), rows, cols).reshape(B, L, B_HEADS, 2, B_QK_DIM)
        bk = rope_2d(bk.reshape(B, L, 2 * B_HEADS, B_QK_DIM), rows, cols).reshape(B, L, B_HEADS, 2, B_QK_DIM)
        lam_init = 0.8 - 0.6 * math.exp(-0.3 * l)
        lam = (jnp.exp(jnp.sum(lam_q1[l] * lam_k1[l]).astype(jnp.float32))
               - jnp.exp(jnp.sum(lam_q2[l] * lam_k2[l]).astype(jnp.float32)) + lam_init)
        ya = window_gqa(aq, ak, av, akc, avc, sink[l])
        yb = diff_attention_latent(bq, bk, bv, bkc, bvc, lam, subln_g[l], lam_init)
        yc = neighbourhood_attention(cq, ck, cv, ckc, cvc, na_bias[l])
        y = jnp.concatenate([ya, yb, yc], axis=-1).astype(xs.dtype) @ w_o[l]
        xn = post_norm(xs, y, x_g1, ln1_g[l], ln1_b[l], alpha)
        xn = post_norm(xn, swiglu(modulate(xn, x_sh2, x_sc2), w_ffn_in[l], w_ffn_out[l]), x_g2, ln2_g[l], ln2_b[l], alpha)
        if not last:
            yac = context_attention(aqc, akc, avc, sink[l])
            ybc = diff_head_norm(diff_core(bqc, bkc, bvc, lam), subln_g[l], lam_init)
            ycc = context_attention(cqc, ckc, cvc)
            yctx = jnp.concatenate([yac, ybc, ycc], axis=-1).astype(cs.dtype) @ w_o[l]
            cs = post_norm(cs, yctx, c_g1, ln1_g[l], ln1_b[l], alpha)
            cs = post_norm(cs, swiglu(modulate(cs, c_sh2, c_sc2), w_ffn_in[l], w_ffn_out[l]), c_g2, ln2_g[l], ln2_b[l], alpha)
        xs = xn
    return xs
```

```python
import functools
import math

import jax
import jax.numpy as jnp
from jax import lax
from jax.experimental import pallas as pl
from jax.experimental.pallas import tpu as pltpu

F32 = jnp.float32
BF16 = jnp.bfloat16

D_MODEL = 1024
DEPTH = 2
GRID_W = 64
HEAD_DIM = 64
A_HEADS = 6
A_KV_HEADS = 2
A_REP = A_HEADS // A_KV_HEADS
WINDOW = 128
WIN_BLK = 128
B_HEADS = 4
B_QK_DIM = 32
C_HEADS = 6
NA_KH = 8
NA_KW = 16
N_MOD = 6
D_FF = 2816
MIX_WIDTH = (A_HEADS + B_HEADS + C_HEADS) * HEAD_DIM
ROPE_BASE = 10000.0
LN_EPS = 1e-5
NEG_INF = -1e30
ALPHA = (2.0 * DEPTH) ** 0.25

AQ = 0
AK = AQ + A_HEADS * HEAD_DIM
AV = AK + A_KV_HEADS * HEAD_DIM
BQ = AV + A_KV_HEADS * HEAD_DIM
BK = BQ + B_HEADS * 2 * B_QK_DIM
BV = BK + B_HEADS * 2 * B_QK_DIM
CQ = BV + B_HEADS * HEAD_DIM
CK = CQ + C_HEADS * HEAD_DIM
CV = CK + C_HEADS * HEAD_DIM
PROJ_WIDTH = CV + C_HEADS * HEAD_DIM

YA = 0
YB = YA + A_HEADS * HEAD_DIM
YC = YB + B_HEADS * HEAD_DIM

LANES = 128
MOD_ROWS = 16
CTX_ROW = 8
VMEM_LIMIT = 56 * 1024 * 1024

Q_BLK = 128
IN_TILE = 512
FFN_TILE = 512
FFN_CHUNK = 1408


def _dot(a, b):
    return jnp.dot(a, b, preferred_element_type=F32)


def _dot_nt(a, b):
    return lax.dot_general(a, b, (((1,), (1,)), ((), ())), preferred_element_type=F32)


def _silu(v):
    return v / (1.0 + jnp.exp(-v))


def _layer_norm(v, g, b):
    mu = jnp.mean(v, axis=-1, keepdims=True)
    d = v - mu
    var = jnp.mean(d * d, axis=-1, keepdims=True)
    return d * lax.rsqrt(var + LN_EPS) * g + b


def _ada_kernel(c_ref, w_ref, b_ref, o_ref):
    s = _silu(c_ref[...])
    o_ref[0] = _dot(s.astype(BF16), w_ref[0].astype(BF16)) + b_ref[0]


def _ada(cc, w_ada, b_ada):
    tn = D_MODEL
    return pl.pallas_call(
        _ada_kernel,
        grid=(DEPTH, N_MOD * D_MODEL // tn),
        in_specs=[
            pl.BlockSpec((MOD_ROWS, D_MODEL), lambda l, j: (0, 0)),
            pl.BlockSpec((1, D_MODEL, tn), lambda l, j: (l, 0, j)),
            pl.BlockSpec((1, 1, tn), lambda l, j: (l, 0, j)),
        ],
        out_specs=pl.BlockSpec((1, MOD_ROWS, tn), lambda l, j: (l, 0, j)),
        out_shape=jax.ShapeDtypeStruct((DEPTH, MOD_ROWS, N_MOD * D_MODEL), F32),
        compiler_params=pltpu.CompilerParams(vmem_limit_bytes=VMEM_LIMIT),
        name="ada",
    )(cc, w_ada, b_ada.reshape(DEPTH, 1, N_MOD * D_MODEL))


def _rope_group(v, cos, sin, off):
    lane = lax.broadcasted_iota(jnp.int32, v.shape, 1)
    low = (lane % (2 * off)) < off
    partner = jnp.where(low, pltpu.roll(v, LANES - off, 1), pltpu.roll(v, off, 1))
    return v * cos + partner * sin


def _inproj_kernel(x_ref, sh_ref, sc_ref, w_ref, *rest, rope):
    if rope:
        cos_a, sin_a, cos_b, sin_b, o_ref = rest
    else:
        (o_ref,) = rest
    h = x_ref[0] * (1.0 + sc_ref[0]) + sh_ref[0]
    p = _dot(h.astype(BF16), w_ref[...])
    if not rope:
        o_ref[0] = p.astype(BF16)
        return
    a_groups = tuple(range(AQ // LANES, AV // LANES))
    b_groups = tuple(range(BQ // LANES, BV // LANES))
    for g in range(PROJ_WIDTH // LANES):
        v = p[:, g * LANES:(g + 1) * LANES]
        if g in a_groups:
            v = _rope_group(v, cos_a[...], sin_a[...], HEAD_DIM // 4)
        elif g in b_groups:
            v = _rope_group(v, cos_b[...], sin_b[...], B_QK_DIM // 4)
        o_ref[0, :, g * LANES:(g + 1) * LANES] = v.astype(BF16)


def _inproj(xs, mod, w_in_b, tables, *, ctx):
    bsz, seq, _ = xs.shape
    t = min(IN_TILE, seq)
    row = (lambda b: CTX_ROW) if ctx else (lambda b: b)
    in_specs = [
        pl.BlockSpec((1, t, D_MODEL), lambda b, i: (b, i, 0)),
        pl.BlockSpec((1, 1, D_MODEL), lambda b, i: (row(b), 0, 0)),
        pl.BlockSpec((1, 1, D_MODEL), lambda b, i: (row(b), 0, 1)),
        pl.BlockSpec((D_MODEL, PROJ_WIDTH), lambda b, i: (0, 0)),
    ]
    args = [xs, mod, mod, w_in_b]
    if tables is not None:
        in_specs += [pl.BlockSpec((t, LANES), lambda b, i: (i, 0))] * 4
        args += list(tables)
    return pl.pallas_call(
        functools.partial(_inproj_kernel, rope=tables is not None),
        grid=(bsz, seq // t),
        in_specs=in_specs,
        out_specs=pl.BlockSpec((1, t, PROJ_WIDTH), lambda b, i: (b, i, 0)),
        out_shape=jax.ShapeDtypeStruct((bsz, seq, PROJ_WIDTH), BF16),
        compiler_params=pltpu.CompilerParams(vmem_limit_bytes=VMEM_LIMIT),
        name="inproj_ctx" if ctx else "inproj",
    )(*args)


def _rope_tables(seq):
    tpos = jnp.arange(seq, dtype=jnp.int32)
    rows = (tpos // GRID_W).astype(F32)[:, None]
    cols = (tpos % GRID_W).astype(F32)[:, None]
    lane = jnp.arange(LANES, dtype=jnp.int32)

    def table(head_dim):
        quarter = head_dim // 4
        inv = ROPE_BASE ** (-jnp.arange(quarter, dtype=F32) / quarter)
        freq = inv[lane % quarter][None, :]
        use_cols = ((lane % head_dim) >= head_dim // 2)[None, :]
        ang = jnp.where(use_cols, cols * freq, rows * freq)
        sign = jnp.where((lane % (2 * quarter)) < quarter, -1.0, 1.0).astype(F32)[None, :]
        return jnp.cos(ang), jnp.sin(ang) * sign

    cos_a, sin_a = table(HEAD_DIM)
    cos_b, sin_b = table(B_QK_DIM)
    return cos_a, sin_a, cos_b, sin_b


def _lam_value(lam_ref, lam_init):
    v = lam_ref[...]
    s1 = jnp.sum(v[0:1, :] * v[1:2, :], axis=-1, keepdims=True)
    s2 = jnp.sum(v[2:3, :] * v[3:4, :], axis=-1, keepdims=True)
    return jnp.exp(s1) - jnp.exp(s2) + lam_init


def _softmax_av(scores, values, extra_logit=None):
    m = None
    for s in scores:
        ms = jnp.max(s, axis=-1, keepdims=True)
        m = ms if m is None else jnp.maximum(m, ms)
    if extra_logit is not None:
        m = jnp.maximum(m, extra_logit)
    den = None if extra_logit is None else jnp.exp(extra_logit - m)
    out = None
    for s, v in zip(scores, values):
        e = jnp.exp(s - m)
        ls = jnp.sum(e, axis=-1, keepdims=True)
        den = ls if den is None else den + ls
        o = _dot(e.astype(BF16), v)
        out = o if out is None else out + o
    return out / den


def _diff_head(q_pairs, k_pieces, v_pieces, lam, g, lam_init):
    scale = B_QK_DIM ** -0.5
    weights = [None] * len(k_pieces)
    for m in range(2):
        ss = [_dot_nt(q_pairs[m], kp[m]) * scale for kp in k_pieces]
        mx = None
        for s in ss:
            ms = jnp.max(s, axis=-1, keepdims=True)
            mx = ms if mx is None else jnp.maximum(mx, ms)
        es = [jnp.exp(s - mx) for s in ss]
        den = None
        for e in es:
            ls = jnp.sum(e, axis=-1, keepdims=True)
            den = ls if den is None else den + ls
        w = 1.0 / den
        if m == 1:
            w = -lam * w
        for j, e in enumerate(es):
            weights[j] = e * w if weights[j] is None else weights[j] + e * w
    o = None
    for a, v in zip(weights, v_pieces):
        t = _dot(a.astype(BF16), v)
        o = t if o is None else o + t
    ms = jnp.mean(o * o, axis=-1, keepdims=True)
    return o * lax.rsqrt(ms + LN_EPS) * g * (1.0 - lam_init)


def _attn_kernel(sink_ref, p_ref, pc_ref, lam_ref, g_ref, tb_ref, o_ref, *, lam_init, seq):
    n = pl.program_id(1)
    nb = seq // Q_BLK
    q0 = pl.multiple_of(n * Q_BLK, Q_BLK)
    scale = HEAD_DIM ** -0.5

    ws = pl.multiple_of(jnp.clip(n - 1, 0, nb - 3) * WIN_BLK, WIN_BLK)
    qpos = q0 + lax.broadcasted_iota(jnp.int32, (Q_BLK, 3 * WIN_BLK), 0)
    kpos = ws + lax.broadcasted_iota(jnp.int32, (Q_BLK, 3 * WIN_BLK), 1)
    valid = jnp.abs(qpos - kpos) <= WINDOW
    for g in range(A_KV_HEADS):
        kw = p_ref[0, pl.ds(ws, 3 * WIN_BLK), AK + g * HEAD_DIM:AK + (g + 1) * HEAD_DIM]
        vw = p_ref[0, pl.ds(ws, 3 * WIN_BLK), AV + g * HEAD_DIM:AV + (g + 1) * HEAD_DIM]
        kc = pc_ref[0, :, AK + g * HEAD_DIM:AK + (g + 1) * HEAD_DIM]
        vc = pc_ref[0, :, AV + g * HEAD_DIM:AV + (g + 1) * HEAD_DIM]
        for r in range(A_REP):
            h = g * A_REP + r
            q = p_ref[0, pl.ds(q0, Q_BLK), AQ + h * HEAD_DIM:AQ + (h + 1) * HEAD_DIM]
            s_w = jnp.where(valid, _dot_nt(q, kw) * scale, NEG_INF)
            s_c = _dot_nt(q, kc) * scale
            o = _softmax_av([s_w, s_c], [vw, vc], extra_logit=sink_ref[h])
            o_ref[0, :, YA + h * HEAD_DIM:YA + (h + 1) * HEAD_DIM] = o.astype(BF16)

    lam = _lam_value(lam_ref, lam_init)
    for h in range(B_HEADS):
        def cols(base, m):
            c0 = base + h * 2 * B_QK_DIM + m * B_QK_DIM
            return slice(c0, c0 + B_QK_DIM)
        q_pairs = [p_ref[0, pl.ds(q0, Q_BLK), cols(BQ, m)] for m in range(2)]
        k_lat = [p_ref[0, :, cols(BK, m)] for m in range(2)]
        k_ctx = [pc_ref[0, :, cols(BK, m)] for m in range(2)]
        v_lat = p_ref[0, :, BV + h * HEAD_DIM:BV + (h + 1) * HEAD_DIM]
        v_ctx = pc_ref[0, :, BV + h * HEAD_DIM:BV + (h + 1) * HEAD_DIM]
        o = _diff_head(q_pairs, [k_lat, k_ctx], [v_lat, v_ctx], lam, g_ref[...], lam_init)
        o_ref[0, :, YB + h * HEAD_DIM:YB + (h + 1) * HEAD_DIM] = o.astype(BF16)

    n_rows = seq // GRID_W
    for i in range(Q_BLK // GRID_W):
        r = n * (Q_BLK // GRID_W) + i
        rs = jnp.clip(r - NA_KH // 2, 0, n_rows - NA_KH)
        didx = r - rs
        qr = pl.multiple_of(r * GRID_W, GRID_W)
        kr = pl.multiple_of(rs * GRID_W, GRID_W)
        for h in range(C_HEADS):
            hs = slice(h * HEAD_DIM, (h + 1) * HEAD_DIM)
            q = p_ref[0, pl.ds(qr, GRID_W), CQ + h * HEAD_DIM:CQ + (h + 1) * HEAD_DIM]
            kw = p_ref[0, pl.ds(kr, NA_KH * GRID_W), CK + h * HEAD_DIM:CK + (h + 1) * HEAD_DIM]
            vw = p_ref[0, pl.ds(kr, NA_KH * GRID_W), CV + h * HEAD_DIM:CV + (h + 1) * HEAD_DIM]
            kc = pc_ref[0, :, CK + h * HEAD_DIM:CK + (h + 1) * HEAD_DIM]
            vc = pc_ref[0, :, CV + h * HEAD_DIM:CV + (h + 1) * HEAD_DIM]
            s_w = _dot_nt(q, kw) * scale + tb_ref[h, didx]
            s_c = _dot_nt(q, kc) * scale
            o = _softmax_av([s_w, s_c], [vw, vc])
            o_ref[0, i * GRID_W:(i + 1) * GRID_W, YC + h * HEAD_DIM:YC + (h + 1) * HEAD_DIM] = o.astype(BF16)


def _attn(p, pc, sink_l, lamv, g, tb, lam_init):
    bsz, seq, _ = p.shape
    n_ctx = pc.shape[1]
    return pl.pallas_call(
        functools.partial(_attn_kernel, lam_init=lam_init, seq=seq),
        grid=(bsz, seq // Q_BLK),
        in_specs=[
            pl.BlockSpec(memory_space=pltpu.SMEM),
            pl.BlockSpec((1, seq, PROJ_WIDTH), lambda b, n: (b, 0, 0)),
            pl.BlockSpec((1, n_ctx, PROJ_WIDTH), lambda b, n: (b, 0, 0)),
            pl.BlockSpec((4, B_QK_DIM), lambda b, n: (0, 0)),
            pl.BlockSpec((1, HEAD_DIM), lambda b, n: (0, 0)),
            pl.BlockSpec(tb.shape, lambda b, n: (0, 0, 0, 0)),
        ],
        out_specs=pl.BlockSpec((1, Q_BLK, MIX_WIDTH), lambda b, n: (b, n, 0)),
        out_shape=jax.ShapeDtypeStruct((bsz, seq, MIX_WIDTH), BF16),
        compiler_params=pltpu.CompilerParams(vmem_limit_bytes=VMEM_LIMIT),
        name="attn",
    )(sink_l, p, pc, lamv, g, tb)


def _ctx_attn_kernel(sink_ref, pc_ref, lam_ref, g_ref, o_ref, *, lam_init):
    scale = HEAD_DIM ** -0.5
    for h in range(A_HEADS):
        g = h // A_REP
        q = pc_ref[0, :, AQ + h * HEAD_DIM:AQ + (h + 1) * HEAD_DIM]
        k = pc_ref[0, :, AK + g * HEAD_DIM:AK + (g + 1) * HEAD_DIM]
        v = pc_ref[0, :, AV + g * HEAD_DIM:AV + (g + 1) * HEAD_DIM]
        o = _softmax_av([_dot_nt(q, k) * scale], [v], extra_logit=sink_ref[h])
        o_ref[0, :, YA + h * HEAD_DIM:YA + (h + 1) * HEAD_DIM] = o.astype(BF16)
    lam = _lam_value(lam_ref, lam_init)
    for h in range(B_HEADS):
        def cols(base, m):
            c0 = base + h * 2 * B_QK_DIM + m * B_QK_DIM
            return slice(c0, c0 + B_QK_DIM)
        q_pairs = [pc_ref[0, :, cols(BQ, m)] for m in range(2)]
        k_ctx = [pc_ref[0, :, cols(BK, m)] for m in range(2)]
        v_ctx = pc_ref[0, :, BV + h * HEAD_DIM:BV + (h + 1) * HEAD_DIM]
        o = _diff_head(q_pairs, [k_ctx], [v_ctx], lam, g_ref[...], lam_init)
        o_ref[0, :, YB + h * HEAD_DIM:YB + (h + 1) * HEAD_DIM] = o.astype(BF16)
    for h in range(C_HEADS):
        q = pc_ref[0, :, CQ + h * HEAD_DIM:CQ + (h + 1) * HEAD_DIM]
        k = pc_ref[0, :, CK + h * HEAD_DIM:CK + (h + 1) * HEAD_DIM]
        v = pc_ref[0, :, CV + h * HEAD_DIM:CV + (h + 1) * HEAD_DIM]
        o = _softmax_av([_dot_nt(q, k) * scale], [v])
        o_ref[0, :, YC + h * HEAD_DIM:YC + (h + 1) * HEAD_DIM] = o.astype(BF16)


def _ctx_attn(pc, sink_l, lamv, g, lam_init):
    bsz, n_ctx, _ = pc.shape
    return pl.pallas_call(
        functools.partial(_ctx_attn_kernel, lam_init=lam_init),
        grid=(bsz,),
        in_specs=[
            pl.BlockSpec(memory_space=pltpu.SMEM),
            pl.BlockSpec((1, n_ctx, PROJ_WIDTH), lambda b: (b, 0, 0)),
            pl.BlockSpec((4, B_QK_DIM), lambda b: (0, 0)),
            pl.BlockSpec((1, HEAD_DIM), lambda b: (0, 0)),
        ],
        out_specs=pl.BlockSpec((1, n_ctx, MIX_WIDTH), lambda b: (b, 0, 0)),
        out_shape=jax.ShapeDtypeStruct((bsz, n_ctx, MIX_WIDTH), BF16),
        compiler_params=pltpu.CompilerParams(vmem_limit_bytes=VMEM_LIMIT),
        name="ctx_attn",
    )(sink_l, pc, lamv, g)


def _na_bias_table(na_bias_l):
    cq = jnp.arange(GRID_W, dtype=jnp.int32)
    cs = jnp.clip(cq - NA_KW // 2, 0, GRID_W - NA_KW)
    col_valid = (cq[None, :] >= cs[:, None]) & (cq[None, :] < cs[:, None] + NA_KW)
    dc = jnp.clip(cq[None, :] - cq[:, None], -(NA_KW - 1), NA_KW - 1) + (NA_KW - 1)
    d = jnp.arange(NA_KH, dtype=jnp.int32)
    kh = jnp.arange(NA_KH, dtype=jnp.int32)
    dr = kh[None, :] + (NA_KH - 1) - d[:, None]
    bias = na_bias_l[:, dr[:, None, :, None], dc[None, :, None, :]].astype(F32)
    bias = jnp.where(col_valid[None, None, :, None, :], bias, NEG_INF)
    return bias.reshape(C_HEADS, NA_KH, GRID_W, NA_KH * GRID_W)


def _outffn_kernel(x_ref, y_ref, g1_ref, sh2_ref, sc2_ref, g2_ref, wo_ref, ln1g_ref, ln1b_ref, ln2g_ref, ln2b_ref,
                   wg_ref, wu_ref, wd_ref, o_ref, xn_scr, h_scr, acc_scr):
    f = pl.program_id(2)

    @pl.when(f == 0)
    def _():
        y = _dot(y_ref[0], wo_ref[...])
        xn = _layer_norm(ALPHA * x_ref[0] + g1_ref[0] * y, ln1g_ref[...], ln1b_ref[...])
        xn_scr[...] = xn
        h_scr[...] = (xn * (1.0 + sc2_ref[0]) + sh2_ref[0]).astype(BF16)

    h = h_scr[...]
    act = _silu(_dot(h, wg_ref[...])) * _dot(h, wu_ref[...])
    part = _dot(act.astype(BF16), wd_ref[...])

    @pl.when(f == 0)
    def _():
        acc_scr[...] = part

    @pl.when(f > 0)
    def _():
        acc_scr[...] += part

    @pl.when(f == pl.num_programs(2) - 1)
    def _():
        o_ref[0] = _layer_norm(ALPHA * xn_scr[...] + g2_ref[0] * acc_scr[...], ln2g_ref[...], ln2b_ref[...])


def _outffn(xs, y, mod, wo_b, ln1g, ln1b, ln2g, ln2b, wfi_b, wfo_b, *, ctx):
    bsz, seq, _ = xs.shape
    t = min(FFN_TILE, seq)
    nf = D_FF // FFN_CHUNK
    row = (lambda b: CTX_ROW) if ctx else (lambda b: b)

    def mod_spec(k):
        return pl.BlockSpec((1, 1, D_MODEL), lambda b, i, f: (row(b), 0, k))

    vec = pl.BlockSpec((1, D_MODEL), lambda b, i, f: (0, 0))
    return pl.pallas_call(
        _outffn_kernel,
        grid=(bsz, seq // t, nf),
        in_specs=[
            pl.BlockSpec((1, t, D_MODEL), lambda b, i, f: (b, i, 0)),
            pl.BlockSpec((1, t, MIX_WIDTH), lambda b, i, f: (b, i, 0)),
            mod_spec(2), mod_spec(3), mod_spec(4), mod_spec(5),
            pl.BlockSpec((MIX_WIDTH, D_MODEL), lambda b, i, f: (0, 0)),
            vec, vec, vec, vec,
            pl.BlockSpec((D_MODEL, FFN_CHUNK), lambda b, i, f: (0, f)),
            pl.BlockSpec((D_MODEL, FFN_CHUNK), lambda b, i, f: (0, nf + f)),
            pl.BlockSpec((FFN_CHUNK, D_MODEL), lambda b, i, f: (f, 0)),
        ],
        out_specs=pl.BlockSpec((1, t, D_MODEL), lambda b, i, f: (b, i, 0)),
        out_shape=jax.ShapeDtypeStruct((bsz, seq, D_MODEL), F32),
        scratch_shapes=[
            pltpu.VMEM((t, D_MODEL), F32),
            pltpu.VMEM((t, D_MODEL), BF16),
            pltpu.VMEM((t, D_MODEL), F32),
        ],
        compiler_params=pltpu.CompilerParams(vmem_limit_bytes=VMEM_LIMIT),
        name="outffn_ctx" if ctx else "outffn",
    )(xs, y, mod, mod, mod, mod, wo_b, ln1g, ln1b, ln2g, ln2b, wfi_b, wfi_b, wfo_b)


def kernel(x, c, ctx, c_ctx, w_ada, b_ada, w_in, w_o, sink, lam_q1, lam_k1, lam_q2, lam_k2, subln_g, na_bias,
           ln1_g, ln1_b, w_ffn_in, w_ffn_out, ln2_g, ln2_b):
    bsz, seq, _ = x.shape
    assert x.shape == (bsz, seq, D_MODEL) and seq % Q_BLK == 0 and seq // Q_BLK >= 3
    assert bsz < CTX_ROW + 1 <= MOD_ROWS
    cc = jnp.zeros((MOD_ROWS, D_MODEL), F32).at[:bsz].set(c).at[CTX_ROW].set(c_ctx)
    mod_all = _ada(cc, w_ada, b_ada).reshape(DEPTH, MOD_ROWS, 1, N_MOD * D_MODEL)
    tables = _rope_tables(seq)

    xs, cs = x, ctx
    for l in range(DEPTH):
        last = l == DEPTH - 1
        lam_init = 0.8 - 0.6 * math.exp(-0.3 * l)
        mod = mod_all[l]
        w_in_b = w_in[l].astype(BF16)
        wo_b = w_o[l].astype(BF16)
        wfi_b = w_ffn_in[l].astype(BF16)
        wfo_b = w_ffn_out[l].astype(BF16)
        lamv = jnp.stack([lam_q1[l], lam_k1[l], lam_q2[l], lam_k2[l]])
        g = subln_g[l].reshape(1, HEAD_DIM)
        lnp = [v[l].reshape(1, D_MODEL) for v in (ln1_g, ln1_b, ln2_g, ln2_b)]

        p = _inproj(xs, mod, w_in_b, tables, ctx=False)
        pc = _inproj(cs, mod, w_in_b, None, ctx=True)
        y = _attn(p, pc, sink[l], lamv, g, _na_bias_table(na_bias[l]), lam_init)
        xn = _outffn(xs, y, mod, wo_b, *lnp, wfi_b, wfo_b, ctx=False)
        if not last:
            yc = _ctx_attn(pc, sink[l], lamv, g, lam_init)
            cs = _outffn(cs, yc, mod, wo_b, *lnp, wfi_b, wfo_b, ctx=True)
        xs = xn
    return xs
```

```python
import functools
import math

import jax
import jax.numpy as jnp
from jax import lax
from jax.experimental import pallas as pl
from jax.experimental.pallas import tpu as pltpu

F32 = jnp.float32
BF16 = jnp.bfloat16

D_MODEL = 1024
DEPTH = 2
GRID_W = 64
HEAD_DIM = 64
A_HEADS = 6
A_KV_HEADS = 2
A_REP = A_HEADS // A_KV_HEADS
WINDOW = 128
WIN_BLK = 128
B_HEADS = 4
B_QK_DIM = 32
C_HEADS = 6
NA_KH = 8
NA_KW = 16
N_MOD = 6
D_FF = 2816
MIX_WIDTH = (A_HEADS + B_HEADS + C_HEADS) * HEAD_DIM
ROPE_BASE = 10000.0
LN_EPS = 1e-5
NEG_INF = -1e30
LOG2E = 1.4426950408889634
ALPHA = (2.0 * DEPTH) ** 0.25

AQ = 0
AK = AQ + A_HEADS * HEAD_DIM
AV = AK + A_KV_HEADS * HEAD_DIM
BQ = AV + A_KV_HEADS * HEAD_DIM
BK = BQ + B_HEADS * 2 * B_QK_DIM
BV = BK + B_HEADS * 2 * B_QK_DIM
CQ = BV + B_HEADS * HEAD_DIM
CK = CQ + C_HEADS * HEAD_DIM
CV = CK + C_HEADS * HEAD_DIM
PROJ_WIDTH = CV + C_HEADS * HEAD_DIM

YA = 0
YB = YA + A_HEADS * HEAD_DIM
YC = YB + B_HEADS * HEAD_DIM

LANES = 128
MOD_ROWS = 16
CTX_ROW = 8
VMEM_LIMIT = 56 * 1024 * 1024

Q_BLK = 128
ATTN_LOOKAHEAD = 2
IN_TILE = 512
FFN_TILE = 512
FFN_CHUNK = 1408


def _dot(a, b):
    return jnp.dot(a, b, preferred_element_type=F32)


def _dot_nt(a, b):
    return lax.dot_general(a, b, (((1,), (1,)), ((), ())), preferred_element_type=F32)


def _silu(v):
    return v / (1.0 + jnp.exp(-v))


def _layer_norm(v, g, b):
    mu = jnp.mean(v, axis=-1, keepdims=True)
    d = v - mu
    var = jnp.mean(d * d, axis=-1, keepdims=True)
    return d * lax.rsqrt(var + LN_EPS) * g + b


def _ada_kernel(c_ref, w_ref, b_ref, o_ref):
    s = _silu(c_ref[...])
    o_ref[0] = _dot(s.astype(BF16), w_ref[0].astype(BF16)) + b_ref[0]


def _ada(cc, w_ada, b_ada):
    tn = D_MODEL
    return pl.pallas_call(
        _ada_kernel,
        grid=(DEPTH, N_MOD * D_MODEL // tn),
        in_specs=[
            pl.BlockSpec((MOD_ROWS, D_MODEL), lambda l, j: (0, 0)),
            pl.BlockSpec((1, D_MODEL, tn), lambda l, j: (l, 0, j)),
            pl.BlockSpec((1, 1, tn), lambda l, j: (l, 0, j)),
        ],
        out_specs=pl.BlockSpec((1, MOD_ROWS, tn), lambda l, j: (l, 0, j)),
        out_shape=jax.ShapeDtypeStruct((DEPTH, MOD_ROWS, N_MOD * D_MODEL), F32),
        compiler_params=pltpu.CompilerParams(vmem_limit_bytes=VMEM_LIMIT),
        name="ada",
    )(cc, w_ada, b_ada.reshape(DEPTH, 1, N_MOD * D_MODEL))


def _rope_group(v, cos, sin, off):
    lane = lax.broadcasted_iota(jnp.int32, v.shape, 1)
    low = (lane % (2 * off)) < off
    partner = jnp.where(low, pltpu.roll(v, LANES - off, 1), pltpu.roll(v, off, 1))
    return v * cos + partner * sin


def _inproj_kernel(x_ref, sh_ref, sc_ref, w_ref, *rest, rope):
    if rope:
        cos_a, sin_a, cos_b, sin_b, o_ref = rest
    else:
        (o_ref,) = rest
    h = x_ref[0] * (1.0 + sc_ref[0]) + sh_ref[0]
    p = _dot(h.astype(BF16), w_ref[...])
    if not rope:
        o_ref[0] = p.astype(BF16)
        return
    a_groups = tuple(range(AQ // LANES, AV // LANES))
    b_groups = tuple(range(BQ // LANES, BV // LANES))
    for g in range(PROJ_WIDTH // LANES):
        v = p[:, g * LANES:(g + 1) * LANES]
        if g in a_groups:
            v = _rope_group(v, cos_a[...], sin_a[...], HEAD_DIM // 4)
        elif g in b_groups:
            v = _rope_group(v, cos_b[...], sin_b[...], B_QK_DIM // 4)
        o_ref[0, :, g * LANES:(g + 1) * LANES] = v.astype(BF16)


def _inproj(xs, mod, w_in_b, tables, *, ctx):
    bsz, seq, _ = xs.shape
    t = min(IN_TILE, seq)
    row = (lambda b: CTX_ROW) if ctx else (lambda b: b)
    in_specs = [
        pl.BlockSpec((1, t, D_MODEL), lambda b, i: (b, i, 0)),
        pl.BlockSpec((1, 1, D_MODEL), lambda b, i: (row(b), 0, 0)),
        pl.BlockSpec((1, 1, D_MODEL), lambda b, i: (row(b), 0, 1)),
        pl.BlockSpec((D_MODEL, PROJ_WIDTH), lambda b, i: (0, 0)),
    ]
    args = [xs, mod, mod, w_in_b]
    if tables is not None:
        in_specs += [pl.BlockSpec((t, LANES), lambda b, i: (i, 0))] * 4
        args += list(tables)
    return pl.pallas_call(
        functools.partial(_inproj_kernel, rope=tables is not None),
        grid=(bsz, seq // t),
        in_specs=in_specs,
        out_specs=pl.BlockSpec((1, t, PROJ_WIDTH), lambda b, i: (b, i, 0)),
        out_shape=jax.ShapeDtypeStruct((bsz, seq, PROJ_WIDTH), BF16),
        compiler_params=pltpu.CompilerParams(vmem_limit_bytes=VMEM_LIMIT),
        name="inproj_ctx" if ctx else "inproj",
    )(*args)


def _rope_tables(seq):
    tpos = jnp.arange(seq, dtype=jnp.int32)
    rows = (tpos // GRID_W).astype(F32)[:, None]
    cols = (tpos % GRID_W).astype(F32)[:, None]
    lane = jnp.arange(LANES, dtype=jnp.int32)

    def table(head_dim):
        quarter = head_dim // 4
        inv = ROPE_BASE ** (-jnp.arange(quarter, dtype=F32) / quarter)
        freq = inv[lane % quarter][None, :]
        use_cols = ((lane % head_dim) >= head_dim // 2)[None, :]
        ang = jnp.where(use_cols, cols * freq, rows * freq)
        sign = jnp.where((lane % (2 * quarter)) < quarter, -1.0, 1.0).astype(F32)[None, :]
        return jnp.cos(ang), jnp.sin(ang) * sign

    cos_a, sin_a = table(HEAD_DIM)
    cos_b, sin_b = table(B_QK_DIM)
    return cos_a, sin_a, cos_b, sin_b


def _lam_value(lam_ref, lam_init):
    v = lam_ref[...]
    s1 = jnp.sum(v[0:1, :] * v[1:2, :], axis=-1, keepdims=True)
    s2 = jnp.sum(v[2:3, :] * v[3:4, :], axis=-1, keepdims=True)
    return jnp.exp(s1) - jnp.exp(s2) + lam_init


def _softmax_av(pieces, c2, extra_logit=None):
    terms, m2 = [], None
    for s, bias2, _ in pieces:
        if bias2 is None:
            t = s
            ms = jnp.max(s, axis=-1, keepdims=True) * c2
        else:
            t = s * c2 + bias2
            ms = jnp.max(t, axis=-1, keepdims=True)
        terms.append(t)
        m2 = ms if m2 is None else jnp.maximum(m2, ms)
    den = None
    if extra_logit is not None:
        m2 = jnp.maximum(m2, extra_logit * LOG2E)
        den = jnp.exp2(extra_logit * LOG2E - m2)
    out = None
    for (_, bias2, v), t in zip(pieces, terms):
        e = jnp.exp2(t * c2 - m2) if bias2 is None else jnp.exp2(t - m2)
        ls = jnp.sum(e, axis=-1, keepdims=True)
        den = ls if den is None else den + ls
        o = _dot(e.astype(BF16), v)
        out = o if out is None else out + o
    return out / den


def _diff_scores(q_pairs, k_pieces):
    return [[_dot_nt(q_pairs[m], kp[m]) for kp in k_pieces] for m in range(2)]


def _diff_av(scores, v_pieces, lam, g, lam_init, v_off=0):
    c2 = B_QK_DIM ** -0.5 * LOG2E
    es, dens = [], []
    for ss in scores:
        mx = None
        for s in ss:
            ms = jnp.max(s, axis=-1, keepdims=True)
            mx = ms if mx is None else jnp.maximum(mx, ms)
        m2 = mx * c2
        e = [jnp.exp2(s * c2 - m2) for s in ss]
        den = None
        for ej in e:
            ls = jnp.sum(ej, axis=-1, keepdims=True)
            den = ls if den is None else den + ls
        es.append(e)
        dens.append(den)
    w = lam * dens[0] / dens[1]
    o = None
    for e1, e2, v in zip(es[0], es[1], v_pieces):
        t = _dot((e1 - w * e2).astype(BF16), v)
        o = t if o is None else o + t
    o = (o / dens[0])[:, v_off:v_off + HEAD_DIM]
    ms = jnp.mean(o * o, axis=-1, keepdims=True)
    return o * lax.rsqrt(ms + LN_EPS) * g * (1.0 - lam_init)


def _lane_group(col):
    g0 = col // LANES * LANES
    return slice(g0, g0 + LANES)


def _placed(q, col):
    t, w = q.shape
    off = col % LANES
    parts = [jnp.zeros((t, off), q.dtype)] if off else []
    parts.append(q)
    if LANES - off - w:
        parts.append(jnp.zeros((t, LANES - off - w), q.dtype))
    return jnp.concatenate(parts, axis=-1)


def _pipelined(units, depth):
    started = []
    for i, (start, _) in enumerate(units):
        started.append(start())
        if i >= depth:
            units[i - depth][1](started[i - depth])
    for i in range(max(len(units) - depth, 0), len(units)):
        units[i][1](started[i])


def _head_cols(base, h):
    return slice(base + h * HEAD_DIM, base + (h + 1) * HEAD_DIM)


def _pair_cols(base, h, m):
    c0 = base + h * 2 * B_QK_DIM + m * B_QK_DIM
    return slice(c0, c0 + B_QK_DIM)


def _attn_kernel(sink_ref, p_ref, pc_ref, lam_ref, g_ref, tb_ref, o_ref, *, lam_init, seq):
    n = pl.program_id(1)
    c2 = HEAD_DIM ** -0.5 * LOG2E
    units = []

    nb = seq // WIN_BLK
    for sub in range(Q_BLK // WIN_BLK):
        blk = n * (Q_BLK // WIN_BLK) + sub
        q0 = pl.multiple_of(blk * WIN_BLK, WIN_BLK)
        ws = pl.multiple_of(jnp.clip(blk - 1, 0, nb - 3) * WIN_BLK, WIN_BLK)
        qpos = q0 + lax.broadcasted_iota(jnp.int32, (WIN_BLK, 3 * WIN_BLK), 0)
        kpos = ws + lax.broadcasted_iota(jnp.int32, (WIN_BLK, 3 * WIN_BLK), 1)
        valid = jnp.abs(qpos - kpos) <= WINDOW
        rows = slice(sub * WIN_BLK, (sub + 1) * WIN_BLK)
        for h in range(A_HEADS):
            g = h // A_REP

            def start(q0=q0, ws=ws, valid=valid, h=h, g=g):
                kcol = AK + g * HEAD_DIM
                q = _placed(p_ref[0, pl.ds(q0, WIN_BLK), _head_cols(AQ, h)], kcol)
                kw = p_ref[0, pl.ds(ws, 3 * WIN_BLK), _lane_group(kcol)]
                kc = pc_ref[0, :, _lane_group(kcol)]
                return jnp.where(valid, _dot_nt(q, kw), NEG_INF), _dot_nt(q, kc)

            def finish(scores, ws=ws, rows=rows, h=h, g=g):
                vcol, ycol = AV + g * HEAD_DIM, YA + h * HEAD_DIM
                vw = p_ref[0, pl.ds(ws, 3 * WIN_BLK), _lane_group(vcol)]
                vc = pc_ref[0, :, _lane_group(vcol)]
                o = _softmax_av([(scores[0], None, vw), (scores[1], None, vc)], c2, extra_logit=sink_ref[h])
                if vcol % LANES != ycol % LANES:
                    o = pltpu.roll(o, LANES // 2, 1)
                o_ref[0, rows, _head_cols(YA, h)] = o[:, ycol % LANES:ycol % LANES + HEAD_DIM].astype(BF16)

            units.append((start, finish))

    lam = _lam_value(lam_ref, lam_init)
    qb = pl.multiple_of(n * Q_BLK, Q_BLK)
    for h in range(B_HEADS):
        def start(h=h):
            kcols = [_pair_cols(BK, h, m).start for m in range(2)]
            q_pairs = [_placed(p_ref[0, pl.ds(qb, Q_BLK), _pair_cols(BQ, h, m)], kcols[m]) for m in range(2)]
            k_lat = [p_ref[0, :, _lane_group(kcols[m])] for m in range(2)]
            k_ctx = [pc_ref[0, :, _lane_group(kcols[m])] for m in range(2)]
            return _diff_scores(q_pairs, [k_lat, k_ctx])

        def finish(scores, h=h):
            vcol = BV + h * HEAD_DIM
            v_lat = p_ref[0, :, _lane_group(vcol)]
            v_ctx = pc_ref[0, :, _lane_group(vcol)]
            o = _diff_av(scores, [v_lat, v_ctx], lam, g_ref[...], lam_init, v_off=vcol % LANES)
            o_ref[0, :, _head_cols(YB, h)] = o.astype(BF16)

        units.append((start, finish))

    n_rows = seq // GRID_W
    left = lax.broadcasted_iota(jnp.int32, (GRID_W, LANES), 1) < GRID_W
    for i in range(Q_BLK // GRID_W):
        r = n * (Q_BLK // GRID_W) + i
        rs = jnp.clip(r - NA_KH // 2, 0, n_rows - NA_KH)
        dr0 = (NA_KH - 1) - (r - rs)
        qr = pl.multiple_of(r * GRID_W, GRID_W)
        kr = pl.multiple_of(rs * GRID_W, GRID_W)
        for h in range(C_HEADS):
            def start(qr=qr, kr=kr, h=h):
                kcol = CK + h * HEAD_DIM
                q = _placed(p_ref[0, pl.ds(qr, GRID_W), _head_cols(CQ, h)], kcol)
                kw = p_ref[0, pl.ds(kr, NA_KH * GRID_W), _lane_group(kcol)]
                kc = pc_ref[0, :, _lane_group(kcol)]
                return _dot_nt(q, kw), _dot_nt(q, kc)

            def finish(scores, kr=kr, dr0=dr0, i=i, h=h):
                vcol = CV + h * HEAD_DIM
                vw = p_ref[0, pl.ds(kr, NA_KH * GRID_W), _lane_group(vcol)]
                vc = pc_ref[0, :, _lane_group(vcol)]
                bias2 = jnp.concatenate(
                    [jnp.where(left, tb_ref[h, dr0 + kh], tb_ref[h, dr0 + kh + 1]) for kh in range(0, NA_KH, 2)],
                    axis=-1)
                o = _softmax_av([(scores[0], bias2, vw), (scores[1], None, vc)], c2)
                o_ref[0, i * GRID_W:(i + 1) * GRID_W, _head_cols(YC, h)] = (
                    o[:, vcol % LANES:vcol % LANES + HEAD_DIM].astype(BF16))

            units.append((start, finish))

    _pipelined(units, ATTN_LOOKAHEAD)


def _attn(p, pc, sink_l, lamv, g, tb, lam_init):
    bsz, seq, _ = p.shape
    n_ctx = pc.shape[1]
    return pl.pallas_call(
        functools.partial(_attn_kernel, lam_init=lam_init, seq=seq),
        grid=(bsz, seq // Q_BLK),
        in_specs=[
            pl.BlockSpec(memory_space=pltpu.SMEM),
            pl.BlockSpec((1, seq, PROJ_WIDTH), lambda b, n: (b, 0, 0)),
            pl.BlockSpec((1, n_ctx, PROJ_WIDTH), lambda b, n: (b, 0, 0)),
            pl.BlockSpec((4, B_QK_DIM), lambda b, n: (0, 0)),
            pl.BlockSpec((1, HEAD_DIM), lambda b, n: (0, 0)),
            pl.BlockSpec(tb.shape, lambda b, n: (0, 0, 0, 0)),
        ],
        out_specs=pl.BlockSpec((1, Q_BLK, MIX_WIDTH), lambda b, n: (b, n, 0)),
        out_shape=jax.ShapeDtypeStruct((bsz, seq, MIX_WIDTH), BF16),
        compiler_params=pltpu.CompilerParams(vmem_limit_bytes=VMEM_LIMIT),
        name="attn",
    )(sink_l, p, pc, lamv, g, tb)


def _ctx_attn_kernel(sink_ref, pc_ref, lam_ref, g_ref, o_ref, *, lam_init):
    c2 = HEAD_DIM ** -0.5 * LOG2E
    for h in range(A_HEADS):
        g = h // A_REP
        q = pc_ref[0, :, _head_cols(AQ, h)]
        k = pc_ref[0, :, _head_cols(AK, g)]
        v = pc_ref[0, :, _head_cols(AV, g)]
        o = _softmax_av([(_dot_nt(q, k), None, v)], c2, extra_logit=sink_ref[h])
        o_ref[0, :, _head_cols(YA, h)] = o.astype(BF16)
    lam = _lam_value(lam_ref, lam_init)
    for h in range(B_HEADS):
        q_pairs = [pc_ref[0, :, _pair_cols(BQ, h, m)] for m in range(2)]
        k_ctx = [pc_ref[0, :, _pair_cols(BK, h, m)] for m in range(2)]
        v_ctx = pc_ref[0, :, _head_cols(BV, h)]
        o = _diff_av(_diff_scores(q_pairs, [k_ctx]), [v_ctx], lam, g_ref[...], lam_init)
        o_ref[0, :, _head_cols(YB, h)] = o.astype(BF16)
    for h in range(C_HEADS):
        q = pc_ref[0, :, _head_cols(CQ, h)]
        k = pc_ref[0, :, _head_cols(CK, h)]
        v = pc_ref[0, :, _head_cols(CV, h)]
        o = _softmax_av([(_dot_nt(q, k), None, v)], c2)
        o_ref[0, :, _head_cols(YC, h)] = o.astype(BF16)


def _ctx_attn(pc, sink_l, lamv, g, lam_init):
    bsz, n_ctx, _ = pc.shape
    return pl.pallas_call(
        functools.partial(_ctx_attn_kernel, lam_init=lam_init),
        grid=(bsz,),
        in_specs=[
            pl.BlockSpec(memory_space=pltpu.SMEM),
            pl.BlockSpec((1, n_ctx, PROJ_WIDTH), lambda b: (b, 0, 0)),
            pl.BlockSpec((4, B_QK_DIM), lambda b: (0, 0)),
            pl.BlockSpec((1, HEAD_DIM), lambda b: (0, 0)),
        ],
        out_specs=pl.BlockSpec((1, n_ctx, MIX_WIDTH), lambda b: (b, 0, 0)),
        out_shape=jax.ShapeDtypeStruct((bsz, n_ctx, MIX_WIDTH), BF16),
        compiler_params=pltpu.CompilerParams(vmem_limit_bytes=VMEM_LIMIT),
        name="ctx_attn",
    )(sink_l, pc, lamv, g)


N_DR = 2 * NA_KH - 1
N_DC = 2 * NA_KW - 1


def _na_table_kernel(nb_ref, o_ref):
    base = (pl.program_id(0) * C_HEADS + pl.program_id(1)) * (N_DR * N_DC)
    wq = lax.broadcasted_iota(jnp.int32, (GRID_W, LANES), 0)
    wk = lax.broadcasted_iota(jnp.int32, (GRID_W, LANES), 1) % GRID_W
    dc = jnp.clip(wk - wq, -(NA_KW - 1), NA_KW - 1) + (NA_KW - 1)
    cs = jnp.clip(wq - NA_KW // 2, 0, GRID_W - NA_KW)
    valid = (wk >= cs) & (wk < cs + NA_KW)
    for dr in range(N_DR):
        acc = jnp.zeros((GRID_W, LANES), F32)
        for c in range(N_DC):
            acc = jnp.where(dc == c, nb_ref[base + dr * N_DC + c], acc)
        o_ref[0, 0, dr] = jnp.where(valid, acc * LOG2E, NEG_INF)


def _na_table(na_bias):
    return pl.pallas_call(
        _na_table_kernel,
        grid=(DEPTH, C_HEADS),
        in_specs=[pl.BlockSpec(memory_space=pltpu.SMEM)],
        out_specs=pl.BlockSpec((1, 1, N_DR, GRID_W, LANES), lambda l, h: (l, h, 0, 0, 0)),
        out_shape=jax.ShapeDtypeStruct((DEPTH, C_HEADS, N_DR, GRID_W, LANES), F32),
        name="na_table",
    )(na_bias.reshape(-1))


def _outffn_kernel(x_ref, y_ref, g1_ref, sh2_ref, sc2_ref, g2_ref, wo_ref, ln1g_ref, ln1b_ref, ln2g_ref, ln2b_ref,
                   wg_ref, wu_ref, wd_ref, o_ref, xn_scr, h_scr, acc_scr):
    f = pl.program_id(2)

    @pl.when(f == 0)
    def _():
        y = _dot(y_ref[0], wo_ref[...])
        xn = _layer_norm(ALPHA * x_ref[0] + g1_ref[0] * y, ln1g_ref[...], ln1b_ref[...])
        xn_scr[...] = xn
        h_scr[...] = (xn * (1.0 + sc2_ref[0]) + sh2_ref[0]).astype(BF16)

    h = h_scr[...]
    act = _silu(_dot(h, wg_ref[...])) * _dot(h, wu_ref[...])
    part = _dot(act.astype(BF16), wd_ref[...])

    @pl.when(f == 0)
    def _():
        acc_scr[...] = part

    @pl.when(f > 0)
    def _():
        acc_scr[...] += part

    @pl.when(f == pl.num_programs(2) - 1)
    def _():
        o_ref[0] = _layer_norm(ALPHA * xn_scr[...] + g2_ref[0] * acc_scr[...], ln2g_ref[...], ln2b_ref[...])


def _outffn(xs, y, mod, wo_b, ln1g, ln1b, ln2g, ln2b, wfi_b, wfo_b, *, ctx):
    bsz, seq, _ = xs.shape
    t = min(FFN_TILE, seq)
    nf = D_FF // FFN_CHUNK
    row = (lambda b: CTX_ROW) if ctx else (lambda b: b)

    def mod_spec(k):
        return pl.BlockSpec((1, 1, D_MODEL), lambda b, i, f: (row(b), 0, k))

    vec = pl.BlockSpec((1, D_MODEL), lambda b, i, f: (0, 0))
    return pl.pallas_call(
        _outffn_kernel,
        grid=(bsz, seq // t, nf),
        in_specs=[
            pl.BlockSpec((1, t, D_MODEL), lambda b, i, f: (b, i, 0)),
            pl.BlockSpec((1, t, MIX_WIDTH), lambda b, i, f: (b, i, 0)),
            mod_spec(2), mod_spec(3), mod_spec(4), mod_spec(5),
            pl.BlockSpec((MIX_WIDTH, D_MODEL), lambda b, i, f: (0, 0)),
            vec, vec, vec, vec,
            pl.BlockSpec((D_MODEL, FFN_CHUNK), lambda b, i, f: (0, f)),
            pl.BlockSpec((D_MODEL, FFN_CHUNK), lambda b, i, f: (0, nf + f)),
            pl.BlockSpec((FFN_CHUNK, D_MODEL), lambda b, i, f: (f, 0)),
        ],
        out_specs=pl.BlockSpec((1, t, D_MODEL), lambda b, i, f: (b, i, 0)),
        out_shape=jax.ShapeDtypeStruct((bsz, seq, D_MODEL), F32),
        scratch_shapes=[
            pltpu.VMEM((t, D_MODEL), F32),
            pltpu.VMEM((t, D_MODEL), BF16),
            pltpu.VMEM((t, D_MODEL), F32),
        ],
        compiler_params=pltpu.CompilerParams(vmem_limit_bytes=VMEM_LIMIT),
        name="outffn_ctx" if ctx else "outffn",
    )(xs, y, mod, mod, mod, mod, wo_b, ln1g, ln1b, ln2g, ln2b, wfi_b, wfi_b, wfo_b)


def kernel(x, c, ctx, c_ctx, w_ada, b_ada, w_in, w_o, sink, lam_q1, lam_k1, lam_q2, lam_k2, subln_g, na_bias,
           ln1_g, ln1_b, w_ffn_in, w_ffn_out, ln2_g, ln2_b):
    bsz, seq, _ = x.shape
    assert x.shape == (bsz, seq, D_MODEL) and seq % Q_BLK == 0 and seq // Q_BLK >= 3
    assert bsz < CTX_ROW + 1 <= MOD_ROWS
    cc = jnp.zeros((MOD_ROWS, D_MODEL), F32).at[:bsz].set(c).at[CTX_ROW].set(c_ctx)
    mod_all = _ada(cc, w_ada, b_ada).reshape(DEPTH, MOD_ROWS, 1, N_MOD * D_MODEL)
    tables = _rope_tables(seq)
    na_tab = _na_table(na_bias)

    xs, cs = x, ctx
    for l in range(DEPTH):
        last = l == DEPTH - 1
        lam_init = 0.8 - 0.6 * math.exp(-0.3 * l)
        mod = mod_all[l]
        w_in_b = w_in[l].astype(BF16)
        wo_b = w_o[l].astype(BF16)
        wfi_b = w_ffn_in[l].astype(BF16)
        wfo_b = w_ffn_out[l].astype(BF16)
        lamv = jnp.stack([lam_q1[l], lam_k1[l], lam_q2[l], lam_k2[l]])
        g = subln_g[l].reshape(1, HEAD_DIM)
        lnp = [v[l].reshape(1, D_MODEL) for v in (ln1_g, ln1_b, ln2_g, ln2_b)]

        p = _inproj(xs, mod, w_in_b, tables, ctx=False)
        pc = _inproj(cs, mod, w_in_b, None, ctx=True)
        y = _attn(p, pc, sink[l], lamv, g, na_tab[l], lam_init)
        xn = _outffn(xs, y, mod, wo_b, *lnp, wfi_b, wfo_b, ctx=False)
        if not last:
            yc = _ctx_attn(pc, sink[l], lamv, g, lam_init)
            cs = _outffn(cs, yc, mod, wo_b, *lnp, wfi_b, wfo_b, ctx=True)
        xs = xn
    return xs
```

```python
import functools
import math

import jax
import jax.numpy as jnp
from jax import lax
from jax.experimental import pallas as pl
from jax.experimental.pallas import tpu as pltpu

F32 = jnp.float32
BF16 = jnp.bfloat16

D_MODEL = 1024
DEPTH = 2
GRID_W = 64
HEAD_DIM = 64
A_HEADS = 6
A_KV_HEADS = 2
A_REP = A_HEADS // A_KV_HEADS
WINDOW = 128
WIN_BLK = 128
B_HEADS = 4
B_QK_DIM = 32
C_HEADS = 6
NA_KH = 8
NA_KW = 16
N_MOD = 6
D_FF = 2816
MIX_WIDTH = (A_HEADS + B_HEADS + C_HEADS) * HEAD_DIM
ROPE_BASE = 10000.0
LN_EPS = 1e-5
NEG_INF = -1e30
LOG2E = 1.4426950408889634
ALPHA = (2.0 * DEPTH) ** 0.25

AQ = 0
AK = AQ + A_HEADS * HEAD_DIM
AV = AK + A_KV_HEADS * HEAD_DIM
BQ = AV + A_KV_HEADS * HEAD_DIM
BK = BQ + B_HEADS * 2 * B_QK_DIM
BV = BK + B_HEADS * 2 * B_QK_DIM
CQ = BV + B_HEADS * HEAD_DIM
CK = CQ + C_HEADS * HEAD_DIM
CV = CK + C_HEADS * HEAD_DIM
PROJ_WIDTH = CV + C_HEADS * HEAD_DIM

YA = 0
YB = YA + A_HEADS * HEAD_DIM
YC = YB + B_HEADS * HEAD_DIM

LANES = 128
MOD_ROWS = 16
CTX_ROW = 8
VMEM_LIMIT = 56 * 1024 * 1024

Q_BLK = 128
ATTN_LOOKAHEAD = 2
IN_TILE = 512
FFN_TILE = 512
SUB_TILE = 256


def _dot(a, b):
    return jnp.dot(a, b, preferred_element_type=F32)


def _dot_nt(a, b):
    return lax.dot_general(a, b, (((1,), (1,)), ((), ())), preferred_element_type=F32)


def _silu(v):
    return v / (1.0 + jnp.exp(-v))


def _layer_norm(v, g, b):
    mu = jnp.mean(v, axis=-1, keepdims=True)
    d = v - mu
    var = jnp.mean(d * d, axis=-1, keepdims=True)
    return d * lax.rsqrt(var + LN_EPS) * g + b


def _ada_kernel(c_ref, w_ref, b_ref, o_ref):
    s = _silu(c_ref[...])
    o_ref[0] = _dot(s.astype(BF16), w_ref[0].astype(BF16)) + b_ref[0]


def _ada(cc, w_ada, b_ada):
    tn = D_MODEL
    return pl.pallas_call(
        _ada_kernel,
        grid=(DEPTH, N_MOD * D_MODEL // tn),
        in_specs=[
            pl.BlockSpec((MOD_ROWS, D_MODEL), lambda l, j: (0, 0)),
            pl.BlockSpec((1, D_MODEL, tn), lambda l, j: (l, 0, j)),
            pl.BlockSpec((1, 1, tn), lambda l, j: (l, 0, j)),
        ],
        out_specs=pl.BlockSpec((1, MOD_ROWS, tn), lambda l, j: (l, 0, j)),
        out_shape=jax.ShapeDtypeStruct((DEPTH, MOD_ROWS, N_MOD * D_MODEL), F32),
        compiler_params=pltpu.CompilerParams(vmem_limit_bytes=VMEM_LIMIT),
        name="ada",
    )(cc, w_ada, b_ada.reshape(DEPTH, 1, N_MOD * D_MODEL))


def _rope_group(v, cos, sin, off):
    lane = lax.broadcasted_iota(jnp.int32, v.shape, 1)
    low = (lane % (2 * off)) < off
    partner = jnp.where(low, pltpu.roll(v, LANES - off, 1), pltpu.roll(v, off, 1))
    return v * cos + partner * sin


def _staged(n_units, stages):
    state = [None] * n_units
    for step in range(n_units + len(stages) - 1):
        for k, stage in enumerate(stages):
            u = step - k
            if 0 <= u < n_units:
                state[u] = stage(u, state[u])


def _inproj_kernel(x_ref, sh_ref, sc_ref, w_ref, *rest, rope, sub):
    if rope:
        cos_a, sin_a, cos_b, sin_b, o_ref = rest
    else:
        (o_ref,) = rest
    a_groups = tuple(range(AQ // LANES, AV // LANES))
    b_groups = tuple(range(BQ // LANES, BV // LANES))

    def rows(u):
        return slice(u * sub, (u + 1) * sub)

    def project(u, _):
        h = x_ref[0, rows(u)] * (1.0 + sc_ref[0]) + sh_ref[0]
        return _dot(h.astype(BF16), w_ref[...])

    def rotate_store(u, p):
        if not rope:
            o_ref[0, rows(u)] = p.astype(BF16)
            return
        for g in range(PROJ_WIDTH // LANES):
            v = p[:, g * LANES:(g + 1) * LANES]
            if g in a_groups:
                v = _rope_group(v, cos_a[rows(u)], sin_a[rows(u)], HEAD_DIM // 4)
            elif g in b_groups:
                v = _rope_group(v, cos_b[rows(u)], sin_b[rows(u)], B_QK_DIM // 4)
            o_ref[0, rows(u), g * LANES:(g + 1) * LANES] = v.astype(BF16)

    _staged(x_ref.shape[1] // sub, [project, rotate_store])


def _inproj(xs, mod, w_in_b, tables, *, ctx):
    bsz, seq, _ = xs.shape
    t = min(IN_TILE, seq)
    row = (lambda b: CTX_ROW) if ctx else (lambda b: b)
    in_specs = [
        pl.BlockSpec((1, t, D_MODEL), lambda b, i: (b, i, 0)),
        pl.BlockSpec((1, 1, D_MODEL), lambda b, i: (row(b), 0, 0)),
        pl.BlockSpec((1, 1, D_MODEL), lambda b, i: (row(b), 0, 1)),
        pl.BlockSpec((D_MODEL, PROJ_WIDTH), lambda b, i: (0, 0), pipeline_mode=pl.Buffered(1)),
    ]
    args = [xs, mod, mod, w_in_b]
    if tables is not None:
        in_specs += [pl.BlockSpec((t, LANES), lambda b, i: (i, 0))] * 4
        args += list(tables)
    return pl.pallas_call(
        functools.partial(_inproj_kernel, rope=tables is not None, sub=min(SUB_TILE, t)),
        grid=(bsz, seq // t),
        in_specs=in_specs,
        out_specs=pl.BlockSpec((1, t, PROJ_WIDTH), lambda b, i: (b, i, 0)),
        out_shape=jax.ShapeDtypeStruct((bsz, seq, PROJ_WIDTH), BF16),
        compiler_params=pltpu.CompilerParams(vmem_limit_bytes=VMEM_LIMIT),
        name="inproj_ctx" if ctx else "inproj",
    )(*args)


def _rope_tables(seq):
    tpos = jnp.arange(seq, dtype=jnp.int32)
    rows = (tpos // GRID_W).astype(F32)[:, None]
    cols = (tpos % GRID_W).astype(F32)[:, None]
    lane = jnp.arange(LANES, dtype=jnp.int32)

    def table(head_dim):
        quarter = head_dim // 4
        inv = ROPE_BASE ** (-jnp.arange(quarter, dtype=F32) / quarter)
        freq = inv[lane % quarter][None, :]
        use_cols = ((lane % head_dim) >= head_dim // 2)[None, :]
        ang = jnp.where(use_cols, cols * freq, rows * freq)
        sign = jnp.where((lane % (2 * quarter)) < quarter, -1.0, 1.0).astype(F32)[None, :]
        return jnp.cos(ang), jnp.sin(ang) * sign

    cos_a, sin_a = table(HEAD_DIM)
    cos_b, sin_b = table(B_QK_DIM)
    return cos_a, sin_a, cos_b, sin_b


def _lam_value(lam_ref, lam_init):
    v = lam_ref[...]
    s1 = jnp.sum(v[0:1, :] * v[1:2, :], axis=-1, keepdims=True)
    s2 = jnp.sum(v[2:3, :] * v[3:4, :], axis=-1, keepdims=True)
    return jnp.exp(s1) - jnp.exp(s2) + lam_init


def _softmax_av(pieces, c2, extra_logit=None):
    terms, m2 = [], None
    for s, bias2, _ in pieces:
        if bias2 is None:
            t = s
            ms = jnp.max(s, axis=-1, keepdims=True) * c2
        else:
            t = s * c2 + bias2
            ms = jnp.max(t, axis=-1, keepdims=True)
        terms.append(t)
        m2 = ms if m2 is None else jnp.maximum(m2, ms)
    den = None
    if extra_logit is not None:
        m2 = jnp.maximum(m2, extra_logit * LOG2E)
        den = jnp.exp2(extra_logit * LOG2E - m2)
    out = None
    for (_, bias2, v), t in zip(pieces, terms):
        e = jnp.exp2(t * c2 - m2) if bias2 is None else jnp.exp2(t - m2)
        ls = jnp.sum(e, axis=-1, keepdims=True)
        den = ls if den is None else den + ls
        o = _dot(e.astype(BF16), v)
        out = o if out is None else out + o
    return out / den


def _diff_scores(q_pairs, k_pieces):
    return [[_dot_nt(q_pairs[m], kp[m]) for kp in k_pieces] for m in range(2)]


def _diff_av(scores, v_pieces, lam, g, lam_init, v_off=0):
    c2 = B_QK_DIM ** -0.5 * LOG2E
    es, dens = [], []
    for ss in scores:
        mx = None
        for s in ss:
            ms = jnp.max(s, axis=-1, keepdims=True)
            mx = ms if mx is None else jnp.maximum(mx, ms)
        m2 = mx * c2
        e = [jnp.exp2(s * c2 - m2) for s in ss]
        den = None
        for ej in e:
            ls = jnp.sum(ej, axis=-1, keepdims=True)
            den = ls if den is None else den + ls
        es.append(e)
        dens.append(den)
    w = lam * dens[0] / dens[1]
    o = None
    for e1, e2, v in zip(es[0], es[1], v_pieces):
        t = _dot((e1 - w * e2).astype(BF16), v)
        o = t if o is None else o + t
    o = (o / dens[0])[:, v_off:v_off + HEAD_DIM]
    ms = jnp.mean(o * o, axis=-1, keepdims=True)
    return o * lax.rsqrt(ms + LN_EPS) * g * (1.0 - lam_init)


def _lane_group(col):
    g0 = col // LANES * LANES
    return slice(g0, g0 + LANES)


def _placed(q, col):
    t, w = q.shape
    off = col % LANES
    parts = [jnp.zeros((t, off), q.dtype)] if off else []
    parts.append(q)
    if LANES - off - w:
        parts.append(jnp.zeros((t, LANES - off - w), q.dtype))
    return jnp.concatenate(parts, axis=-1)


def _pipelined(units, depth):
    started = []
    for i, (start, _) in enumerate(units):
        started.append(start())
        if i >= depth:
            units[i - depth][1](started[i - depth])
    for i in range(max(len(units) - depth, 0), len(units)):
        units[i][1](started[i])


def _head_cols(base, h):
    return slice(base + h * HEAD_DIM, base + (h + 1) * HEAD_DIM)


def _pair_cols(base, h, m):
    c0 = base + h * 2 * B_QK_DIM + m * B_QK_DIM
    return slice(c0, c0 + B_QK_DIM)


def _attn_kernel(sink_ref, p_ref, pc_ref, lam_ref, g_ref, tb_ref, o_ref, *, lam_init, seq):
    n = pl.program_id(1)
    c2 = HEAD_DIM ** -0.5 * LOG2E
    units = []

    nb = seq // WIN_BLK
    for sub in range(Q_BLK // WIN_BLK):
        blk = n * (Q_BLK // WIN_BLK) + sub
        q0 = pl.multiple_of(blk * WIN_BLK, WIN_BLK)
        ws = pl.multiple_of(jnp.clip(blk - 1, 0, nb - 3) * WIN_BLK, WIN_BLK)
        qpos = q0 + lax.broadcasted_iota(jnp.int32, (WIN_BLK, 3 * WIN_BLK), 0)
        kpos = ws + lax.broadcasted_iota(jnp.int32, (WIN_BLK, 3 * WIN_BLK), 1)
        valid = jnp.abs(qpos - kpos) <= WINDOW
        rows = slice(sub * WIN_BLK, (sub + 1) * WIN_BLK)
        for h in range(A_HEADS):
            g = h // A_REP

            def start(q0=q0, ws=ws, valid=valid, h=h, g=g):
                kcol = AK + g * HEAD_DIM
                q = _placed(p_ref[0, pl.ds(q0, WIN_BLK), _head_cols(AQ, h)], kcol)
                kw = p_ref[0, pl.ds(ws, 3 * WIN_BLK), _lane_group(kcol)]
                kc = pc_ref[0, :, _lane_group(kcol)]
                return jnp.where(valid, _dot_nt(q, kw), NEG_INF), _dot_nt(q, kc)

            def finish(scores, ws=ws, rows=rows, h=h, g=g):
                vcol, ycol = AV + g * HEAD_DIM, YA + h * HEAD_DIM
                vw = p_ref[0, pl.ds(ws, 3 * WIN_BLK), _lane_group(vcol)]
                vc = pc_ref[0, :, _lane_group(vcol)]
                o = _softmax_av([(scores[0], None, vw), (scores[1], None, vc)], c2, extra_logit=sink_ref[h])
                if vcol % LANES != ycol % LANES:
                    o = pltpu.roll(o, LANES // 2, 1)
                o_ref[0, rows, _head_cols(YA, h)] = o[:, ycol % LANES:ycol % LANES + HEAD_DIM].astype(BF16)

            units.append((start, finish))

    lam = _lam_value(lam_ref, lam_init)
    qb = pl.multiple_of(n * Q_BLK, Q_BLK)
    for h in range(B_HEADS):
        def start(h=h):
            kcols = [_pair_cols(BK, h, m).start for m in range(2)]
            q_pairs = [_placed(p_ref[0, pl.ds(qb, Q_BLK), _pair_cols(BQ, h, m)], kcols[m]) for m in range(2)]
            k_lat = [p_ref[0, :, _lane_group(kcols[m])] for m in range(2)]
            k_ctx = [pc_ref[0, :, _lane_group(kcols[m])] for m in range(2)]
            return _diff_scores(q_pairs, [k_lat, k_ctx])

        def finish(scores, h=h):
            vcol = BV + h * HEAD_DIM
            v_lat = p_ref[0, :, _lane_group(vcol)]
            v_ctx = pc_ref[0, :, _lane_group(vcol)]
            o = _diff_av(scores, [v_lat, v_ctx], lam, g_ref[...], lam_init, v_off=vcol % LANES)
            o_ref[0, :, _head_cols(YB, h)] = o.astype(BF16)

        units.append((start, finish))

    n_rows = seq // GRID_W
    left = lax.broadcasted_iota(jnp.int32, (GRID_W, LANES), 1) < GRID_W
    for i in range(Q_BLK // GRID_W):
        r = n * (Q_BLK // GRID_W) + i
        rs = jnp.clip(r - NA_KH // 2, 0, n_rows - NA_KH)
        dr0 = (NA_KH - 1) - (r - rs)
        qr = pl.multiple_of(r * GRID_W, GRID_W)
        kr = pl.multiple_of(rs * GRID_W, GRID_W)
        for h in range(C_HEADS):
            def start(qr=qr, kr=kr, h=h):
                kcol = CK + h * HEAD_DIM
                q = _placed(p_ref[0, pl.ds(qr, GRID_W), _head_cols(CQ, h)], kcol)
                kw = p_ref[0, pl.ds(kr, NA_KH * GRID_W), _lane_group(kcol)]
                kc = pc_ref[0, :, _lane_group(kcol)]
                return _dot_nt(q, kw), _dot_nt(q, kc)

            def finish(scores, kr=kr, dr0=dr0, i=i, h=h):
                vcol = CV + h * HEAD_DIM
                vw = p_ref[0, pl.ds(kr, NA_KH * GRID_W), _lane_group(vcol)]
                vc = pc_ref[0, :, _lane_group(vcol)]
                bias2 = jnp.concatenate(
                    [jnp.where(left, tb_ref[h, dr0 + kh], tb_ref[h, dr0 + kh + 1]) for kh in range(0, NA_KH, 2)],
                    axis=-1)
                o = _softmax_av([(scores[0], bias2, vw), (scores[1], None, vc)], c2)
                o_ref[0, i * GRID_W:(i + 1) * GRID_W, _head_cols(YC, h)] = (
                    o[:, vcol % LANES:vcol % LANES + HEAD_DIM].astype(BF16))

            units.append((start, finish))

    _pipelined(units, ATTN_LOOKAHEAD)


def _attn(p, pc, sink_l, lamv, g, tb, lam_init):
    bsz, seq, _ = p.shape
    n_ctx = pc.shape[1]
    return pl.pallas_call(
        functools.partial(_attn_kernel, lam_init=lam_init, seq=seq),
        grid=(bsz, seq // Q_BLK),
        in_specs=[
            pl.BlockSpec(memory_space=pltpu.SMEM),
            pl.BlockSpec((1, seq, PROJ_WIDTH), lambda b, n: (b, 0, 0)),
            pl.BlockSpec((1, n_ctx, PROJ_WIDTH), lambda b, n: (b, 0, 0)),
            pl.BlockSpec((4, B_QK_DIM), lambda b, n: (0, 0)),
            pl.BlockSpec((1, HEAD_DIM), lambda b, n: (0, 0)),
            pl.BlockSpec(tb.shape, lambda b, n: (0, 0, 0, 0)),
        ],
        out_specs=pl.BlockSpec((1, Q_BLK, MIX_WIDTH), lambda b, n: (b, n, 0)),
        out_shape=jax.ShapeDtypeStruct((bsz, seq, MIX_WIDTH), BF16),
        compiler_params=pltpu.CompilerParams(vmem_limit_bytes=VMEM_LIMIT),
        name="attn",
    )(sink_l, p, pc, lamv, g, tb)


def _ctx_attn_kernel(sink_ref, pc_ref, lam_ref, g_ref, o_ref, *, lam_init):
    c2 = HEAD_DIM ** -0.5 * LOG2E
    for h in range(A_HEADS):
        g = h // A_REP
        q = pc_ref[0, :, _head_cols(AQ, h)]
        k = pc_ref[0, :, _head_cols(AK, g)]
        v = pc_ref[0, :, _head_cols(AV, g)]
        o = _softmax_av([(_dot_nt(q, k), None, v)], c2, extra_logit=sink_ref[h])
        o_ref[0, :, _head_cols(YA, h)] = o.astype(BF16)
    lam = _lam_value(lam_ref, lam_init)
    for h in range(B_HEADS):
        q_pairs = [pc_ref[0, :, _pair_cols(BQ, h, m)] for m in range(2)]
        k_ctx = [pc_ref[0, :, _pair_cols(BK, h, m)] for m in range(2)]
        v_ctx = pc_ref[0, :, _head_cols(BV, h)]
        o = _diff_av(_diff_scores(q_pairs, [k_ctx]), [v_ctx], lam, g_ref[...], lam_init)
        o_ref[0, :, _head_cols(YB, h)] = o.astype(BF16)
    for h in range(C_HEADS):
        q = pc_ref[0, :, _head_cols(CQ, h)]
        k = pc_ref[0, :, _head_cols(CK, h)]
        v = pc_ref[0, :, _head_cols(CV, h)]
        o = _softmax_av([(_dot_nt(q, k), None, v)], c2)
        o_ref[0, :, _head_cols(YC, h)] = o.astype(BF16)


def _ctx_attn(pc, sink_l, lamv, g, lam_init):
    bsz, n_ctx, _ = pc.shape
    return pl.pallas_call(
        functools.partial(_ctx_attn_kernel, lam_init=lam_init),
        grid=(bsz,),
        in_specs=[
            pl.BlockSpec(memory_space=pltpu.SMEM),
            pl.BlockSpec((1, n_ctx, PROJ_WIDTH), lambda b: (b, 0, 0)),
            pl.BlockSpec((4, B_QK_DIM), lambda b: (0, 0)),
            pl.BlockSpec((1, HEAD_DIM), lambda b: (0, 0)),
        ],
        out_specs=pl.BlockSpec((1, n_ctx, MIX_WIDTH), lambda b: (b, 0, 0)),
        out_shape=jax.ShapeDtypeStruct((bsz, n_ctx, MIX_WIDTH), BF16),
        compiler_params=pltpu.CompilerParams(vmem_limit_bytes=VMEM_LIMIT),
        name="ctx_attn",
    )(sink_l, pc, lamv, g)


N_DR = 2 * NA_KH - 1
N_DC = 2 * NA_KW - 1


def _na_table_kernel(nb_ref, o_ref):
    base = (pl.program_id(0) * C_HEADS + pl.program_id(1)) * (N_DR * N_DC)
    wq = lax.broadcasted_iota(jnp.int32, (GRID_W, LANES), 0)
    wk = lax.broadcasted_iota(jnp.int32, (GRID_W, LANES), 1) % GRID_W
    dc = jnp.clip(wk - wq, -(NA_KW - 1), NA_KW - 1) + (NA_KW - 1)
    cs = jnp.clip(wq - NA_KW // 2, 0, GRID_W - NA_KW)
    valid = (wk >= cs) & (wk < cs + NA_KW)
    for dr in range(N_DR):
        acc = jnp.zeros((GRID_W, LANES), F32)
        for c in range(N_DC):
            acc = jnp.where(dc == c, nb_ref[base + dr * N_DC + c], acc)
        o_ref[0, 0, dr] = jnp.where(valid, acc * LOG2E, NEG_INF)


def _na_table(na_bias):
    return pl.pallas_call(
        _na_table_kernel,
        grid=(DEPTH, C_HEADS),
        in_specs=[pl.BlockSpec(memory_space=pltpu.SMEM)],
        out_specs=pl.BlockSpec((1, 1, N_DR, GRID_W, LANES), lambda l, h: (l, h, 0, 0, 0)),
        out_shape=jax.ShapeDtypeStruct((DEPTH, C_HEADS, N_DR, GRID_W, LANES), F32),
        name="na_table",
    )(na_bias.reshape(-1))


def _outffn_kernel(x_ref, y_ref, g1_ref, sh2_ref, sc2_ref, g2_ref, wo_ref, ln1g_ref, ln1b_ref, ln2g_ref, ln2b_ref,
                   wg_ref, wu_ref, wd_ref, o_ref, *, sub):
    def rows(u):
        return slice(u * sub, (u + 1) * sub)

    def out_proj(u, _):
        return _dot(y_ref[0, rows(u)], wo_ref[...])

    def norm_gate_up(u, y):
        xn = _layer_norm(ALPHA * x_ref[0, rows(u)] + g1_ref[0] * y, ln1g_ref[...], ln1b_ref[...])
        h = (xn * (1.0 + sc2_ref[0]) + sh2_ref[0]).astype(BF16)
        return xn, _dot(h, wg_ref[...]), _dot(h, wu_ref[...])

    def act_down(u, st):
        xn, gate, up = st
        return xn, _dot((_silu(gate) * up).astype(BF16), wd_ref[...])

    def norm_store(u, st):
        xn, ff = st
        o_ref[0, rows(u)] = _layer_norm(ALPHA * xn + g2_ref[0] * ff, ln2g_ref[...], ln2b_ref[...])

    _staged(x_ref.shape[1] // sub, [out_proj, norm_gate_up, act_down, norm_store])


def _outffn(xs, y, mod, wo_b, ln1g, ln1b, ln2g, ln2b, wfi_b, wfo_b, *, ctx):
    bsz, seq, _ = xs.shape
    t = min(FFN_TILE, seq)
    row = (lambda b: CTX_ROW) if ctx else (lambda b: b)
    once = pl.Buffered(1)

    def mod_spec(k):
        return pl.BlockSpec((1, 1, D_MODEL), lambda b, i: (row(b), 0, k))

    vec = pl.BlockSpec((1, D_MODEL), lambda b, i: (0, 0))
    return pl.pallas_call(
        functools.partial(_outffn_kernel, sub=min(SUB_TILE, t)),
        grid=(bsz, seq // t),
        in_specs=[
            pl.BlockSpec((1, t, D_MODEL), lambda b, i: (b, i, 0)),
            pl.BlockSpec((1, t, MIX_WIDTH), lambda b, i: (b, i, 0)),
            mod_spec(2), mod_spec(3), mod_spec(4), mod_spec(5),
            pl.BlockSpec((MIX_WIDTH, D_MODEL), lambda b, i: (0, 0), pipeline_mode=once),
            vec, vec, vec, vec,
            pl.BlockSpec((D_MODEL, D_FF), lambda b, i: (0, 0), pipeline_mode=once),
            pl.BlockSpec((D_MODEL, D_FF), lambda b, i: (0, 1), pipeline_mode=once),
            pl.BlockSpec((D_FF, D_MODEL), lambda b, i: (0, 0), pipeline_mode=once),
        ],
        out_specs=pl.BlockSpec((1, t, D_MODEL), lambda b, i: (b, i, 0)),
        out_shape=jax.ShapeDtypeStruct((bsz, seq, D_MODEL), F32),
        compiler_params=pltpu.CompilerParams(vmem_limit_bytes=VMEM_LIMIT),
        name="outffn_ctx" if ctx else "outffn",
    )(xs, y, mod, mod, mod, mod, wo_b, ln1g, ln1b, ln2g, ln2b, wfi_b, wfi_b, wfo_b)


def kernel(x, c, ctx, c_ctx, w_ada, b_ada, w_in, w_o, sink, lam_q1, lam_k1, lam_q2, lam_k2, subln_g, na_bias,
           ln1_g, ln1_b, w_ffn_in, w_ffn_out, ln2_g, ln2_b):
    bsz, seq, _ = x.shape
    assert x.shape == (bsz, seq, D_MODEL) and seq % Q_BLK == 0 and seq // Q_BLK >= 3
    assert bsz < CTX_ROW + 1 <= MOD_ROWS
    cc = jnp.zeros((MOD_ROWS, D_MODEL), F32).at[:bsz].set(c).at[CTX_ROW].set(c_ctx)
    mod_all = _ada(cc, w_ada, b_ada).reshape(DEPTH, MOD_ROWS, 1, N_MOD * D_MODEL)
    tables = _rope_tables(seq)
    na_tab = _na_table(na_bias)

    xs, cs = x, ctx
    for l in range(DEPTH):
        last = l == DEPTH - 1
        lam_init = 0.8 - 0.6 * math.exp(-0.3 * l)
        mod = mod_all[l]
        w_in_b = w_in[l].astype(BF16)
        wo_b = w_o[l].astype(BF16)
        wfi_b = w_ffn_in[l].astype(BF16)
        wfo_b = w_ffn_out[l].astype(BF16)
        lamv = jnp.stack([lam_q1[l], lam_k1[l], lam_q2[l], lam_k2[l]])
        g = subln_g[l].reshape(1, HEAD_DIM)
        lnp = [v[l].reshape(1, D_MODEL) for v in (ln1_g, ln1_b, ln2_g, ln2_b)]

        p = _inproj(xs, mod, w_in_b, tables, ctx=False)
        pc = _inproj(cs, mod, w_in_b, None, ctx=True)
        y = _attn(p, pc, sink[l], lamv, g, na_tab[l], lam_init)
        xn = _outffn(xs, y, mod, wo_b, *lnp, wfi_b, wfo_b, ctx=False)
        if not last:
            yc = _ctx_attn(pc, sink[l], lamv, g, lam_init)
            cs = _outffn(cs, yc, mod, wo_b, *lnp, wfi_b, wfo_b, ctx=True)
        xs = xn
    return xs
```

```python
import functools
import math

import jax
import jax.numpy as jnp
from jax import lax
from jax.experimental import pallas as pl
from jax.experimental.pallas import tpu as pltpu

F32 = jnp.float32
BF16 = jnp.bfloat16

D_MODEL = 1024
DEPTH = 2
GRID_W = 64
HEAD_DIM = 64
A_HEADS = 6
A_KV_HEADS = 2
A_REP = A_HEADS // A_KV_HEADS
WINDOW = 128
WIN_BLK = 128
B_HEADS = 4
B_QK_DIM = 32
C_HEADS = 6
NA_KH = 8
NA_KW = 16
N_MOD = 6
D_FF = 2816
MIX_WIDTH = (A_HEADS + B_HEADS + C_HEADS) * HEAD_DIM
ROPE_BASE = 10000.0
LN_EPS = 1e-5
NEG_INF = -1e30
LOG2E = 1.4426950408889634
QK_SCALE2_A = HEAD_DIM ** -0.5 * LOG2E
QK_SCALE2_B = B_QK_DIM ** -0.5 * LOG2E
ALPHA = (2.0 * DEPTH) ** 0.25

AQ = 0
AK = AQ + A_HEADS * HEAD_DIM
AV = AK + A_KV_HEADS * HEAD_DIM
BQ = AV + A_KV_HEADS * HEAD_DIM
BK = BQ + B_HEADS * 2 * B_QK_DIM
BV = BK + B_HEADS * 2 * B_QK_DIM
CQ = BV + B_HEADS * HEAD_DIM
CK = CQ + C_HEADS * HEAD_DIM
CV = CK + C_HEADS * HEAD_DIM
PROJ_WIDTH = CV + C_HEADS * HEAD_DIM

YA = 0
YB = YA + A_HEADS * HEAD_DIM
YC = YB + B_HEADS * HEAD_DIM

LANES = 128
MOD_ROWS = 16
CTX_ROW = 8
VMEM_LIMIT = 56 * 1024 * 1024

Q_BLK = 128
ATTN_LOOKAHEAD = 2
IN_TILE = 512
FFN_TILE = 512
SUB_TILE = 256


def _dot(a, b):
    return jnp.dot(a, b, preferred_element_type=F32)


def _dot_nt(a, b):
    return lax.dot_general(a, b, (((1,), (1,)), ((), ())), preferred_element_type=F32)


def _silu(v):
    return v / (1.0 + jnp.exp(-v))


def _layer_norm(v, g, b):
    mu = jnp.mean(v, axis=-1, keepdims=True)
    d = v - mu
    var = jnp.mean(d * d, axis=-1, keepdims=True)
    return d * lax.rsqrt(var + LN_EPS) * g + b


def _ada_kernel(c_ref, w_ref, b_ref, o_ref):
    s = _silu(c_ref[...])
    o_ref[0] = _dot(s.astype(BF16), w_ref[0].astype(BF16)) + b_ref[0]


def _ada(cc, w_ada, b_ada):
    tn = D_MODEL
    return pl.pallas_call(
        _ada_kernel,
        grid=(DEPTH, N_MOD * D_MODEL // tn),
        in_specs=[
            pl.BlockSpec((MOD_ROWS, D_MODEL), lambda l, j: (0, 0)),
            pl.BlockSpec((1, D_MODEL, tn), lambda l, j: (l, 0, j)),
            pl.BlockSpec((1, 1, tn), lambda l, j: (l, 0, j)),
        ],
        out_specs=pl.BlockSpec((1, MOD_ROWS, tn), lambda l, j: (l, 0, j)),
        out_shape=jax.ShapeDtypeStruct((DEPTH, MOD_ROWS, N_MOD * D_MODEL), F32),
        compiler_params=pltpu.CompilerParams(vmem_limit_bytes=VMEM_LIMIT),
        name="ada",
    )(cc, w_ada, b_ada.reshape(DEPTH, 1, N_MOD * D_MODEL))


def _rope_group(v, cos, sin, off):
    lane = lax.broadcasted_iota(jnp.int32, v.shape, 1)
    low = (lane % (2 * off)) < off
    partner = jnp.where(low, pltpu.roll(v, LANES - off, 1), pltpu.roll(v, off, 1))
    return v * cos + partner * sin


def _staged(n_units, stages):
    state = [None] * n_units
    for step in range(n_units + len(stages) - 1):
        for k, stage in enumerate(stages):
            u = step - k
            if 0 <= u < n_units:
                state[u] = stage(u, state[u])


def _lane_groups(lo, hi):
    return tuple(range(lo // LANES, hi // LANES))


_GROUP_PLAN = {}
for _g in _lane_groups(AQ, AK):
    _GROUP_PLAN[_g] = (0, HEAD_DIM // 4, QK_SCALE2_A)
for _g in _lane_groups(AK, AV):
    _GROUP_PLAN[_g] = (1, HEAD_DIM // 4, None)
for _g in _lane_groups(BQ, BK):
    _GROUP_PLAN[_g] = (2, B_QK_DIM // 4, QK_SCALE2_B)
for _g in _lane_groups(BK, BV):
    _GROUP_PLAN[_g] = (3, B_QK_DIM // 4, None)
for _g in _lane_groups(CQ, CK):
    _GROUP_PLAN[_g] = (None, None, QK_SCALE2_A)


def _inproj_kernel(x_ref, sh_ref, sc_ref, w_ref, *rest, rope, sub):
    if rope:
        tab_ref, o_ref = rest
    else:
        (o_ref,) = rest

    def rows(u):
        return slice(u * sub, (u + 1) * sub)

    def project(u, _):
        h = x_ref[0, rows(u)] * (1.0 + sc_ref[0]) + sh_ref[0]
        return _dot(h.astype(BF16), w_ref[...])

    def rotate_store(u, p):
        for g in range(PROJ_WIDTH // LANES):
            v = p[:, g * LANES:(g + 1) * LANES]
            pair, quarter, qscale = _GROUP_PLAN.get(g, (None, None, None))
            if rope and pair is not None:
                v = _rope_group(v, tab_ref[2 * pair, rows(u)], tab_ref[2 * pair + 1, rows(u)], quarter)
            elif qscale is not None:
                v = v * qscale
            o_ref[0, rows(u), g * LANES:(g + 1) * LANES] = v.astype(BF16)

    _staged(x_ref.shape[1] // sub, [project, rotate_store])


def _inproj(xs, mod, w_in_b, tables, *, ctx):
    bsz, seq, _ = xs.shape
    t = min(IN_TILE, seq)
    row = (lambda b: CTX_ROW) if ctx else (lambda b: b)
    in_specs = [
        pl.BlockSpec((1, t, D_MODEL), lambda b, i: (b, i, 0)),
        pl.BlockSpec((1, 1, D_MODEL), lambda b, i: (row(b), 0, 0)),
        pl.BlockSpec((1, 1, D_MODEL), lambda b, i: (row(b), 0, 1)),
        pl.BlockSpec((D_MODEL, PROJ_WIDTH), lambda b, i: (0, 0), pipeline_mode=pl.Buffered(1)),
    ]
    args = [xs, mod, mod, w_in_b]
    if tables is not None:
        in_specs.append(pl.BlockSpec((tables.shape[0], t, LANES), lambda b, i: (0, i, 0)))
        args.append(tables)
    return pl.pallas_call(
        functools.partial(_inproj_kernel, rope=tables is not None, sub=min(SUB_TILE, t)),
        grid=(bsz, seq // t),
        in_specs=in_specs,
        out_specs=pl.BlockSpec((1, t, PROJ_WIDTH), lambda b, i: (b, i, 0)),
        out_shape=jax.ShapeDtypeStruct((bsz, seq, PROJ_WIDTH), BF16),
        compiler_params=pltpu.CompilerParams(vmem_limit_bytes=VMEM_LIMIT),
        name="inproj_ctx" if ctx else "inproj",
    )(*args)


def _rope_tables(seq):
    tpos = jnp.arange(seq, dtype=jnp.int32)
    rows = (tpos // GRID_W).astype(F32)[:, None]
    cols = (tpos % GRID_W).astype(F32)[:, None]
    lane = jnp.arange(LANES, dtype=jnp.int32)

    def table(head_dim):
        quarter = head_dim // 4
        inv = ROPE_BASE ** (-jnp.arange(quarter, dtype=F32) / quarter)
        freq = inv[lane % quarter][None, :]
        use_cols = ((lane % head_dim) >= head_dim // 2)[None, :]
        ang = jnp.where(use_cols, cols * freq, rows * freq)
        sign = jnp.where((lane % (2 * quarter)) < quarter, -1.0, 1.0).astype(F32)[None, :]
        return jnp.cos(ang), jnp.sin(ang) * sign

    cos_a, sin_a = table(HEAD_DIM)
    cos_b, sin_b = table(B_QK_DIM)
    return jnp.stack([cos_a * QK_SCALE2_A, sin_a * QK_SCALE2_A, cos_a, sin_a,
                      cos_b * QK_SCALE2_B, sin_b * QK_SCALE2_B, cos_b, sin_b])


def _lam_value(lam_ref, lam_init):
    v = lam_ref[...]
    s1 = jnp.sum(v[0:1, :] * v[1:2, :], axis=-1, keepdims=True)
    s2 = jnp.sum(v[2:3, :] * v[3:4, :], axis=-1, keepdims=True)
    return jnp.exp(s1) - jnp.exp(s2) + lam_init


def _row_stack(parts):
    return jnp.concatenate(parts, axis=0)


def _dot_blocks(e, v):
    if not isinstance(v, (list, tuple)):
        return _dot(e, v)
    t = e.shape[0] // len(v)
    return _row_stack([_dot(e[k * t:(k + 1) * t], vk) for k, vk in enumerate(v)])


def _softmax_av(pieces, extra_logit=None):
    terms, m2 = [], None
    for s, bias2, _ in pieces:
        t = s if bias2 is None else s + bias2
        ms = jnp.max(t, axis=-1, keepdims=True)
        terms.append(t)
        m2 = ms if m2 is None else jnp.maximum(m2, ms)
    den = None
    if extra_logit is not None:
        m2 = jnp.maximum(m2, extra_logit * LOG2E)
        den = jnp.exp2(extra_logit * LOG2E - m2)
    out = None
    for (_, _, v), t in zip(pieces, terms):
        e = jnp.exp2(t - m2)
        ls = jnp.sum(e, axis=-1, keepdims=True)
        den = ls if den is None else den + ls
        o = _dot_blocks(e.astype(BF16), v)
        out = o if out is None else out + o
    return out / den


def _diff_scores(q_pairs, k_pieces):
    return [[_dot_nt(q_pairs[m], kp[m]) for kp in k_pieces] for m in range(2)]


def _diff_av(scores, v_pieces, lam, g, lam_init, v_off=0):
    es, dens = [], []
    for ss in scores:
        mx = None
        for s in ss:
            ms = jnp.max(s, axis=-1, keepdims=True)
            mx = ms if mx is None else jnp.maximum(mx, ms)
        e = [jnp.exp2(s - mx) for s in ss]
        den = None
        for ej in e:
            ls = jnp.sum(ej, axis=-1, keepdims=True)
            den = ls if den is None else den + ls
        es.append(e)
        dens.append(den)
    w = lam * dens[0] / dens[1]
    o = None
    for e1, e2, v in zip(es[0], es[1], v_pieces):
        t = _dot((e1 - w * e2).astype(BF16), v)
        o = t if o is None else o + t
    o = (o / dens[0])[:, v_off:v_off + HEAD_DIM]
    ms = jnp.mean(o * o, axis=-1, keepdims=True)
    return o * lax.rsqrt(ms + LN_EPS) * g * (1.0 - lam_init)


def _lane_group(col):
    g0 = col // LANES * LANES
    return slice(g0, g0 + LANES)


def _placed(q, col):
    t, w = q.shape
    off = col % LANES
    parts = [jnp.zeros((t, off), q.dtype)] if off else []
    parts.append(q)
    if LANES - off - w:
        parts.append(jnp.zeros((t, LANES - off - w), q.dtype))
    return jnp.concatenate(parts, axis=-1)


def _pipelined(units, depth):
    started = []
    for i, (start, _) in enumerate(units):
        started.append(start())
        if i >= depth:
            units[i - depth][1](started[i - depth])
    for i in range(max(len(units) - depth, 0), len(units)):
        units[i][1](started[i])


def _head_cols(base, h):
    return slice(base + h * HEAD_DIM, base + (h + 1) * HEAD_DIM)


def _pair_cols(base, h, m):
    c0 = base + h * 2 * B_QK_DIM + m * B_QK_DIM
    return slice(c0, c0 + B_QK_DIM)


def _diff_head_av(scores, v_pieces, lam, g, lam_init, v_off):
    t = scores[0].shape[0] // 2
    mx = None
    for s in scores:
        ms = jnp.max(s, axis=-1, keepdims=True)
        mx = ms if mx is None else jnp.maximum(mx, ms)
    es = [jnp.exp2(s - mx) for s in scores]
    den = None
    for e in es:
        ls = jnp.sum(e, axis=-1, keepdims=True)
        den = ls if den is None else den + ls
    w = lam * den[0:t] / den[t:2 * t]
    o = None
    for e, v in zip(es, v_pieces):
        part = _dot((e[0:t] - w * e[t:2 * t]).astype(BF16), v)
        o = part if o is None else o + part
    o = (o / den[0:t])[:, v_off:v_off + HEAD_DIM]
    ms = jnp.mean(o * o, axis=-1, keepdims=True)
    return o * lax.rsqrt(ms + LN_EPS) * g * (1.0 - lam_init)


def _attn_kernel(sink_ref, p_ref, pc_ref, lam_ref, g_ref, tb_ref, o_ref, *, lam_init, seq):
    n = pl.program_id(1)
    units_a, units_b, units_c = [], [], []
    lane = lax.broadcasted_iota(jnp.int32, (1, LANES), 1)
    left = lane < HEAD_DIM

    nb = seq // WIN_BLK
    for sub in range(Q_BLK // WIN_BLK):
        blk = n * (Q_BLK // WIN_BLK) + sub
        q0 = pl.multiple_of(blk * WIN_BLK, WIN_BLK)
        ws = pl.multiple_of(jnp.clip(blk - 1, 0, nb - 3) * WIN_BLK, WIN_BLK)
        qpos = q0 + lax.broadcasted_iota(jnp.int32, (A_HEADS * WIN_BLK, 3 * WIN_BLK), 0) % WIN_BLK
        kpos = ws + lax.broadcasted_iota(jnp.int32, (A_HEADS * WIN_BLK, 3 * WIN_BLK), 1)
        valid = jnp.abs(qpos - kpos) <= WINDOW
        rows = slice(sub * WIN_BLK, (sub + 1) * WIN_BLK)

        def start(q0=q0, ws=ws, valid=valid):
            q = _row_stack([_placed(p_ref[0, pl.ds(q0, WIN_BLK), _head_cols(AQ, h)], AK + (h // A_REP) * HEAD_DIM)
                            for h in range(A_HEADS)])
            kw = p_ref[0, pl.ds(ws, 3 * WIN_BLK), _lane_group(AK)]
            kc = pc_ref[0, :, _lane_group(AK)]
            return jnp.where(valid, _dot_nt(q, kw), NEG_INF), _dot_nt(q, kc)

        def finish(scores, ws=ws, rows=rows):
            vw = p_ref[0, pl.ds(ws, 3 * WIN_BLK), _lane_group(AV)]
            vc = pc_ref[0, :, _lane_group(AV)]
            sink = _row_stack([jnp.full((WIN_BLK, 1), sink_ref[h], F32) for h in range(A_HEADS)])
            o = _softmax_av([(scores[0], None, vw), (scores[1], None, vc)], extra_logit=sink)
            for j in range(A_HEADS // 2):
                halves = []
                for h in (2 * j, 2 * j + 1):
                    oh = o[h * WIN_BLK:(h + 1) * WIN_BLK]
                    if h // A_REP != h % 2:
                        oh = pltpu.roll(oh, HEAD_DIM, 1)
                    halves.append(oh)
                o_ref[0, rows, YA + j * LANES:YA + (j + 1) * LANES] = (
                    jnp.where(left, halves[0], halves[1]).astype(BF16))

        units_a.append((start, finish))

    lam = _lam_value(lam_ref, lam_init)
    qb = pl.multiple_of(n * Q_BLK, Q_BLK)
    for h in range(B_HEADS):
        def start(h=h):
            qcol = BQ + h * 2 * B_QK_DIM
            qg = p_ref[0, pl.ds(qb, Q_BLK), _lane_group(qcol)]
            zero = jnp.zeros_like(qg)
            offs = [qcol % LANES + m * B_QK_DIM for m in range(2)]
            q = _row_stack([jnp.where((lane >= off) & (lane < off + B_QK_DIM), qg, zero) for off in offs])
            kcol = BK + h * 2 * B_QK_DIM
            k_lat = p_ref[0, :, _lane_group(kcol)]
            k_ctx = pc_ref[0, :, _lane_group(kcol)]
            return _dot_nt(q, k_lat), _dot_nt(q, k_ctx)

        def finish(scores, h=h):
            vcol = BV + h * HEAD_DIM
            v_lat = p_ref[0, :, _lane_group(vcol)]
            v_ctx = pc_ref[0, :, _lane_group(vcol)]
            o = _diff_head_av(scores, [v_lat, v_ctx], lam, g_ref[...], lam_init, vcol % LANES)
            o_ref[0, :, _head_cols(YB, h)] = o.astype(BF16)

        units_b.append((start, finish))

    n_rows = seq // GRID_W
    blocks = []
    for i in range(Q_BLK // GRID_W):
        r = n * (Q_BLK // GRID_W) + i
        rs = jnp.clip(r - NA_KH // 2, 0, n_rows - NA_KH)
        dr0 = (NA_KH - 1) - (r - rs)
        qr = pl.multiple_of(r * GRID_W, GRID_W)
        kr = pl.multiple_of(rs * GRID_W, GRID_W)
        blocks += [(i, j, qr, kr, dr0) for j in range(C_HEADS // 2)]

    def start_c():
        s_w, s_c = [], []
        for _, j, qr, kr, _ in blocks:
            qg = p_ref[0, pl.ds(qr, GRID_W), _lane_group(CQ + j * LANES)]
            zero = jnp.zeros_like(qg)
            q = _row_stack([jnp.where(left, qg, zero), jnp.where(left, zero, qg)])
            s_w.append(_dot_nt(q, p_ref[0, pl.ds(kr, NA_KH * GRID_W), _lane_group(CK + j * LANES)]))
            s_c.append(_dot_nt(q, pc_ref[0, :, _lane_group(CK + j * LANES)]))
        return _row_stack(s_w), _row_stack(s_c)

    def finish_c(scores):
        bias2 = _row_stack([
            jnp.concatenate([jnp.where(left, tb_ref[h, dr0 + kh], tb_ref[h, dr0 + kh + 1])
                             for kh in range(0, NA_KH, 2)], axis=-1)
            for _, j, _, _, dr0 in blocks for h in (2 * j, 2 * j + 1)])
        vw = [p_ref[0, pl.ds(kr, NA_KH * GRID_W), _lane_group(CV + j * LANES)] for _, j, _, kr, _ in blocks]
        vc = [pc_ref[0, :, _lane_group(CV + j * LANES)] for _, j, _, _, _ in blocks]
        o = _softmax_av([(scores[0], bias2, vw), (scores[1], None, vc)])
        for k, (i, j, _, _, _) in enumerate(blocks):
            ok = o[2 * k * GRID_W:2 * (k + 1) * GRID_W]
            o_ref[0, i * GRID_W:(i + 1) * GRID_W, YC + j * LANES:YC + (j + 1) * LANES] = (
                jnp.where(left, ok[0:GRID_W], ok[GRID_W:2 * GRID_W]).astype(BF16))

    _pipelined([(start_c, finish_c)] + units_a + units_b, ATTN_LOOKAHEAD)


def _attn(p, pc, sink_l, lamv, g, tb, lam_init):
    bsz, seq, _ = p.shape
    n_ctx = pc.shape[1]
    return pl.pallas_call(
        functools.partial(_attn_kernel, lam_init=lam_init, seq=seq),
        grid=(bsz, seq // Q_BLK),
        in_specs=[
            pl.BlockSpec(memory_space=pltpu.SMEM),
            pl.BlockSpec((1, seq, PROJ_WIDTH), lambda b, n: (b, 0, 0)),
            pl.BlockSpec((1, n_ctx, PROJ_WIDTH), lambda b, n: (b, 0, 0)),
            pl.BlockSpec((4, B_QK_DIM), lambda b, n: (0, 0)),
            pl.BlockSpec((1, HEAD_DIM), lambda b, n: (0, 0)),
            pl.BlockSpec(tb.shape, lambda b, n: (0, 0, 0, 0)),
        ],
        out_specs=pl.BlockSpec((1, Q_BLK, MIX_WIDTH), lambda b, n: (b, n, 0)),
        out_shape=jax.ShapeDtypeStruct((bsz, seq, MIX_WIDTH), BF16),
        compiler_params=pltpu.CompilerParams(vmem_limit_bytes=VMEM_LIMIT),
        name="attn",
    )(sink_l, p, pc, lamv, g, tb)


def _ctx_attn_kernel(sink_ref, pc_ref, lam_ref, g_ref, o_ref, *, lam_init):
    for h in range(A_HEADS):
        g = h // A_REP
        q = pc_ref[0, :, _head_cols(AQ, h)]
        k = pc_ref[0, :, _head_cols(AK, g)]
        v = pc_ref[0, :, _head_cols(AV, g)]
        o = _softmax_av([(_dot_nt(q, k), None, v)], extra_logit=sink_ref[h])
        o_ref[0, :, _head_cols(YA, h)] = o.astype(BF16)
    lam = _lam_value(lam_ref, lam_init)
    for h in range(B_HEADS):
        q_pairs = [pc_ref[0, :, _pair_cols(BQ, h, m)] for m in range(2)]
        k_ctx = [pc_ref[0, :, _pair_cols(BK, h, m)] for m in range(2)]
        v_ctx = pc_ref[0, :, _head_cols(BV, h)]
        o = _diff_av(_diff_scores(q_pairs, [k_ctx]), [v_ctx], lam, g_ref[...], lam_init)
        o_ref[0, :, _head_cols(YB, h)] = o.astype(BF16)
    for h in range(C_HEADS):
        q = pc_ref[0, :, _head_cols(CQ, h)]
        k = pc_ref[0, :, _head_cols(CK, h)]
        v = pc_ref[0, :, _head_cols(CV, h)]
        o = _softmax_av([(_dot_nt(q, k), None, v)])
        o_ref[0, :, _head_cols(YC, h)] = o.astype(BF16)


def _ctx_attn(pc, sink_l, lamv, g, lam_init):
    bsz, n_ctx, _ = pc.shape
    return pl.pallas_call(
        functools.partial(_ctx_attn_kernel, lam_init=lam_init),
        grid=(bsz,),
        in_specs=[
            pl.BlockSpec(memory_space=pltpu.SMEM),
            pl.BlockSpec((1, n_ctx, PROJ_WIDTH), lambda b: (b, 0, 0)),
            pl.BlockSpec((4, B_QK_DIM), lambda b: (0, 0)),
            pl.BlockSpec((1, HEAD_DIM), lambda b: (0, 0)),
        ],
        out_specs=pl.BlockSpec((1, n_ctx, MIX_WIDTH), lambda b: (b, 0, 0)),
        out_shape=jax.ShapeDtypeStruct((bsz, n_ctx, MIX_WIDTH), BF16),
        compiler_params=pltpu.CompilerParams(vmem_limit_bytes=VMEM_LIMIT),
        name="ctx_attn",
    )(sink_l, pc, lamv, g)


N_DR = 2 * NA_KH - 1
N_DC = 2 * NA_KW - 1


def _na_table_kernel(nb_ref, o_ref):
    base = (pl.program_id(0) * C_HEADS + pl.program_id(1)) * (N_DR * N_DC)
    wq = lax.broadcasted_iota(jnp.int32, (GRID_W, LANES), 0)
    wk = lax.broadcasted_iota(jnp.int32, (GRID_W, LANES), 1) % GRID_W
    dc = jnp.clip(wk - wq, -(NA_KW - 1), NA_KW - 1) + (NA_KW - 1)
    cs = jnp.clip(wq - NA_KW // 2, 0, GRID_W - NA_KW)
    valid = (wk >= cs) & (wk < cs + NA_KW)
    for dr in range(N_DR):
        acc = jnp.zeros((GRID_W, LANES), F32)
        for c in range(N_DC):
            acc = jnp.where(dc == c, nb_ref[base + dr * N_DC + c], acc)
        o_ref[0, 0, dr] = jnp.where(valid, acc * LOG2E, NEG_INF)


def _na_table(na_bias):
    return pl.pallas_call(
        _na_table_kernel,
        grid=(DEPTH, C_HEADS),
        in_specs=[pl.BlockSpec(memory_space=pltpu.SMEM)],
        out_specs=pl.BlockSpec((1, 1, N_DR, GRID_W, LANES), lambda l, h: (l, h, 0, 0, 0)),
        out_shape=jax.ShapeDtypeStruct((DEPTH, C_HEADS, N_DR, GRID_W, LANES), F32),
        name="na_table",
    )(na_bias.reshape(-1))


def _outffn_kernel(x_ref, y_ref, g1_ref, sh2_ref, sc2_ref, g2_ref, wo_ref, ln1g_ref, ln1b_ref, ln2g_ref, ln2b_ref,
                   wg_ref, wu_ref, wd_ref, o_ref, *, sub):
    def rows(u):
        return slice(u * sub, (u + 1) * sub)

    def out_proj(u, _):
        return _dot(y_ref[0, rows(u)], wo_ref[...])

    def norm_gate_up(u, y):
        xn = _layer_norm(ALPHA * x_ref[0, rows(u)] + g1_ref[0] * y, ln1g_ref[...], ln1b_ref[...])
        h = (xn * (1.0 + sc2_ref[0]) + sh2_ref[0]).astype(BF16)
        return xn, _dot(h, wg_ref[...]), _dot(h, wu_ref[...])

    def act_down(u, st):
        xn, gate, up = st
        return xn, _dot((_silu(gate) * up).astype(BF16), wd_ref[...])

    def norm_store(u, st):
        xn, ff = st
        o_ref[0, rows(u)] = _layer_norm(ALPHA * xn + g2_ref[0] * ff, ln2g_ref[...], ln2b_ref[...])

    _staged(x_ref.shape[1] // sub, [out_proj, norm_gate_up, act_down, norm_store])


def _outffn(xs, y, mod, wo_b, ln1g, ln1b, ln2g, ln2b, wfi_b, wfo_b, *, ctx):
    bsz, seq, _ = xs.shape
    t = min(FFN_TILE, seq)
    row = (lambda b: CTX_ROW) if ctx else (lambda b: b)
    once = pl.Buffered(1)

    def mod_spec(k):
        return pl.BlockSpec((1, 1, D_MODEL), lambda b, i: (row(b), 0, k))

    vec = pl.BlockSpec((1, D_MODEL), lambda b, i: (0, 0))
    return pl.pallas_call(
        functools.partial(_outffn_kernel, sub=min(SUB_TILE, t)),
        grid=(bsz, seq // t),
        in_specs=[
            pl.BlockSpec((1, t, D_MODEL), lambda b, i: (b, i, 0)),
            pl.BlockSpec((1, t, MIX_WIDTH), lambda b, i: (b, i, 0)),
            mod_spec(2), mod_spec(3), mod_spec(4), mod_spec(5),
            pl.BlockSpec((MIX_WIDTH, D_MODEL), lambda b, i: (0, 0), pipeline_mode=once),
            vec, vec, vec, vec,
            pl.BlockSpec((D_MODEL, D_FF), lambda b, i: (0, 0), pipeline_mode=once),
            pl.BlockSpec((D_MODEL, D_FF), lambda b, i: (0, 1), pipeline_mode=once),
            pl.BlockSpec((D_FF, D_MODEL), lambda b, i: (0, 0), pipeline_mode=once),
        ],
        out_specs=pl.BlockSpec((1, t, D_MODEL), lambda b, i: (b, i, 0)),
        out_shape=jax.ShapeDtypeStruct((bsz, seq, D_MODEL), F32),
        compiler_params=pltpu.CompilerParams(vmem_limit_bytes=VMEM_LIMIT),
        name="outffn_ctx" if ctx else "outffn",
    )(xs, y, mod, mod, mod, mod, wo_b, ln1g, ln1b, ln2g, ln2b, wfi_b, wfi_b, wfo_b)


def kernel(x, c, ctx, c_ctx, w_ada, b_ada, w_in, w_o, sink, lam_q1, lam_k1, lam_q2, lam_k2, subln_g, na_bias,
           ln1_g, ln1_b, w_ffn_in, w_ffn_out, ln2_g, ln2_b):
    bsz, seq, _ = x.shape
    assert x.shape == (bsz, seq, D_MODEL) and seq % Q_BLK == 0 and seq // Q_BLK >= 3
    assert bsz < CTX_ROW + 1 <= MOD_ROWS
    cc = jnp.zeros((MOD_ROWS, D_MODEL), F32).at[:bsz].set(c).at[CTX_ROW].set(c_ctx)
    mod_all = _ada(cc, w_ada, b_ada).reshape(DEPTH, MOD_ROWS, 1, N_MOD * D_MODEL)
    tables = _rope_tables(seq)
    na_tab = _na_table(na_bias)

    xs, cs = x, ctx
    for l in range(DEPTH):
        last = l == DEPTH - 1
        lam_init = 0.8 - 0.6 * math.exp(-0.3 * l)
        mod = mod_all[l]
        w_in_b = w_in[l].astype(BF16)
        wo_b = w_o[l].astype(BF16)
        wfi_b = w_ffn_in[l].astype(BF16)
        wfo_b = w_ffn_out[l].astype(BF16)
        lamv = jnp.stack([lam_q1[l], lam_k1[l], lam_q2[l], lam_k2[l]])
        g = subln_g[l].reshape(1, HEAD_DIM)
        lnp = [v[l].reshape(1, D_MODEL) for v in (ln1_g, ln1_b, ln2_g, ln2_b)]

        p = _inproj(xs, mod, w_in_b, tables, ctx=False)
        pc = _inproj(cs, mod, w_in_b, None, ctx=True)
        y = _attn(p, pc, sink[l], lamv, g, na_tab[l], lam_init)
        xn = _outffn(xs, y, mod, wo_b, *lnp, wfi_b, wfo_b, ctx=False)
        if not last:
            yc = _ctx_attn(pc, sink[l], lamv, g, lam_init)
            cs = _outffn(cs, yc, mod, wo_b, *lnp, wfi_b, wfo_b, ctx=True)
        xs = xn
    return xs
```

```python
import functools
import math

import jax
import jax.numpy as jnp
from jax import lax
from jax.experimental import pallas as pl
from jax.experimental.pallas import tpu as pltpu

F32 = jnp.float32
BF16 = jnp.bfloat16

D_MODEL = 1024
DEPTH = 2
GRID_W = 64
HEAD_DIM = 64
A_HEADS = 6
A_KV_HEADS = 2
A_REP = A_HEADS // A_KV_HEADS
WINDOW = 128
WIN_BLK = 128
B_HEADS = 4
B_QK_DIM = 32
C_HEADS = 6
NA_KH = 8
NA_KW = 16
N_MOD = 6
D_FF = 2816
MIX_WIDTH = (A_HEADS + B_HEADS + C_HEADS) * HEAD_DIM
ROPE_BASE = 10000.0
LN_EPS = 1e-5
NEG_INF = -1e30
LOG2E = 1.4426950408889634
QK_SCALE2_A = HEAD_DIM ** -0.5 * LOG2E
QK_SCALE2_B = B_QK_DIM ** -0.5 * LOG2E
ALPHA = (2.0 * DEPTH) ** 0.25

AQ = 0
AK = AQ + A_HEADS * HEAD_DIM
AV = AK + A_KV_HEADS * HEAD_DIM
BQ = AV + A_KV_HEADS * HEAD_DIM
BK = BQ + B_HEADS * 2 * B_QK_DIM
BV = BK + B_HEADS * 2 * B_QK_DIM
CQ = BV + B_HEADS * HEAD_DIM
CK = CQ + C_HEADS * HEAD_DIM
CV = CK + C_HEADS * HEAD_DIM
PROJ_WIDTH = CV + C_HEADS * HEAD_DIM

YA = 0
YB = YA + A_HEADS * HEAD_DIM
YC = YB + B_HEADS * HEAD_DIM

LANES = 128
MOD_ROWS = 16
CTX_ROW = 8
VMEM_LIMIT = 56 * 1024 * 1024

Q_BLK = 256
ATTN_LOOKAHEAD = 2
IN_TILE = 512
FFN_TILE = 512
SUB_TILE = 256


def _dot(a, b):
    return jnp.dot(a, b, preferred_element_type=F32)


def _dot_nt(a, b):
    return lax.dot_general(a, b, (((1,), (1,)), ((), ())), preferred_element_type=F32)


def _silu(v):
    return v / (1.0 + jnp.exp(-v))


def _layer_norm(v, g, b):
    mu = jnp.mean(v, axis=-1, keepdims=True)
    d = v - mu
    var = jnp.mean(d * d, axis=-1, keepdims=True)
    return d * lax.rsqrt(var + LN_EPS) * g + b


def _ada_kernel(c_ref, w_ref, b_ref, o_ref):
    s = _silu(c_ref[...])
    o_ref[0] = _dot(s.astype(BF16), w_ref[0].astype(BF16)) + b_ref[0]


def _ada(cc, w_ada, b_ada):
    tn = D_MODEL
    return pl.pallas_call(
        _ada_kernel,
        grid=(DEPTH, N_MOD * D_MODEL // tn),
        in_specs=[
            pl.BlockSpec((MOD_ROWS, D_MODEL), lambda l, j: (0, 0)),
            pl.BlockSpec((1, D_MODEL, tn), lambda l, j: (l, 0, j)),
            pl.BlockSpec((1, 1, tn), lambda l, j: (l, 0, j)),
        ],
        out_specs=pl.BlockSpec((1, MOD_ROWS, tn), lambda l, j: (l, 0, j)),
        out_shape=jax.ShapeDtypeStruct((DEPTH, MOD_ROWS, N_MOD * D_MODEL), F32),
        compiler_params=pltpu.CompilerParams(vmem_limit_bytes=VMEM_LIMIT),
        name="ada",
    )(cc, w_ada, b_ada.reshape(DEPTH, 1, N_MOD * D_MODEL))


def _rope_group(v, cos, sin, off):
    lane = lax.broadcasted_iota(jnp.int32, v.shape, 1)
    low = (lane % (2 * off)) < off
    partner = jnp.where(low, pltpu.roll(v, LANES - off, 1), pltpu.roll(v, off, 1))
    return v * cos + partner * sin


def _staged(n_units, stages):
    state = [None] * n_units
    for step in range(n_units + len(stages) - 1):
        for k, stage in enumerate(stages):
            u = step - k
            if 0 <= u < n_units:
                state[u] = stage(u, state[u])


def _lane_groups(lo, hi):
    return tuple(range(lo // LANES, hi // LANES))


_GROUP_PLAN = {}
for _g in _lane_groups(AQ, AK):
    _GROUP_PLAN[_g] = (0, HEAD_DIM // 4, QK_SCALE2_A)
for _g in _lane_groups(AK, AV):
    _GROUP_PLAN[_g] = (1, HEAD_DIM // 4, None)
for _g in _lane_groups(BQ, BK):
    _GROUP_PLAN[_g] = (2, B_QK_DIM // 4, QK_SCALE2_B)
for _g in _lane_groups(BK, BV):
    _GROUP_PLAN[_g] = (3, B_QK_DIM // 4, None)
for _g in _lane_groups(CQ, CK):
    _GROUP_PLAN[_g] = (None, None, QK_SCALE2_A)


def _inproj_kernel(x_ref, sh_ref, sc_ref, w_ref, *rest, rope, sub):
    if rope:
        tab_ref, o_ref = rest
    else:
        (o_ref,) = rest

    def rows(u):
        return slice(u * sub, (u + 1) * sub)

    def project(u, _):
        h = x_ref[0, rows(u)] * (1.0 + sc_ref[0]) + sh_ref[0]
        return _dot(h.astype(BF16), w_ref[...])

    def rotate_store(u, p):
        for g in range(PROJ_WIDTH // LANES):
            v = p[:, g * LANES:(g + 1) * LANES]
            pair, quarter, qscale = _GROUP_PLAN.get(g, (None, None, None))
            if rope and pair is not None:
                v = _rope_group(v, tab_ref[2 * pair, rows(u)], tab_ref[2 * pair + 1, rows(u)], quarter)
            elif qscale is not None:
                v = v * qscale
            o_ref[0, rows(u), g * LANES:(g + 1) * LANES] = v.astype(BF16)

    _staged(x_ref.shape[1] // sub, [project, rotate_store])


def _inproj(xs, mod, w_in_b, tables, *, ctx):
    bsz, seq, _ = xs.shape
    t = min(IN_TILE, seq)
    row = (lambda b: CTX_ROW) if ctx else (lambda b: b)
    in_specs = [
        pl.BlockSpec((1, t, D_MODEL), lambda b, i: (b, i, 0)),
        pl.BlockSpec((1, 1, D_MODEL), lambda b, i: (row(b), 0, 0)),
        pl.BlockSpec((1, 1, D_MODEL), lambda b, i: (row(b), 0, 1)),
        pl.BlockSpec((D_MODEL, PROJ_WIDTH), lambda b, i: (0, 0), pipeline_mode=pl.Buffered(1)),
    ]
    args = [xs, mod, mod, w_in_b]
    if tables is not None:
        in_specs.append(pl.BlockSpec((tables.shape[0], t, LANES), lambda b, i: (0, i, 0)))
        args.append(tables)
    return pl.pallas_call(
        functools.partial(_inproj_kernel, rope=tables is not None, sub=min(SUB_TILE, t)),
        grid=(bsz, seq // t),
        in_specs=in_specs,
        out_specs=pl.BlockSpec((1, t, PROJ_WIDTH), lambda b, i: (b, i, 0)),
        out_shape=jax.ShapeDtypeStruct((bsz, seq, PROJ_WIDTH), BF16),
        compiler_params=pltpu.CompilerParams(vmem_limit_bytes=VMEM_LIMIT),
        name="inproj_ctx" if ctx else "inproj",
    )(*args)


def _rope_tables(seq):
    tpos = jnp.arange(seq, dtype=jnp.int32)
    rows = (tpos // GRID_W).astype(F32)[:, None]
    cols = (tpos % GRID_W).astype(F32)[:, None]
    lane = jnp.arange(LANES, dtype=jnp.int32)

    def table(head_dim):
        quarter = head_dim // 4
        inv = ROPE_BASE ** (-jnp.arange(quarter, dtype=F32) / quarter)
        freq = inv[lane % quarter][None, :]
        use_cols = ((lane % head_dim) >= head_dim // 2)[None, :]
        ang = jnp.where(use_cols, cols * freq, rows * freq)
        sign = jnp.where((lane % (2 * quarter)) < quarter, -1.0, 1.0).astype(F32)[None, :]
        return jnp.cos(ang), jnp.sin(ang) * sign

    cos_a, sin_a = table(HEAD_DIM)
    cos_b, sin_b = table(B_QK_DIM)
    return jnp.stack([cos_a * QK_SCALE2_A, sin_a * QK_SCALE2_A, cos_a, sin_a,
                      cos_b * QK_SCALE2_B, sin_b * QK_SCALE2_B, cos_b, sin_b])


def _lam_value(lam_ref, lam_init):
    v = lam_ref[...]
    s1 = jnp.sum(v[0:1, :] * v[1:2, :], axis=-1, keepdims=True)
    s2 = jnp.sum(v[2:3, :] * v[3:4, :], axis=-1, keepdims=True)
    return jnp.exp(s1) - jnp.exp(s2) + lam_init


def _row_stack(parts):
    return jnp.concatenate(parts, axis=0)


def _dot_blocks(e, v):
    if not isinstance(v, (list, tuple)):
        return _dot(e, v)
    t = e.shape[0] // len(v)
    return _row_stack([_dot(e[k * t:(k + 1) * t], vk) for k, vk in enumerate(v)])


def _softmax_av(pieces, extra_logit=None):
    terms, m2 = [], None
    for s, bias2, _ in pieces:
        t = s if bias2 is None else s + bias2
        ms = jnp.max(t, axis=-1, keepdims=True)
        terms.append(t)
        m2 = ms if m2 is None else jnp.maximum(m2, ms)
    den = None
    if extra_logit is not None:
        m2 = jnp.maximum(m2, extra_logit * LOG2E)
        den = jnp.exp2(extra_logit * LOG2E - m2)
    out = None
    for (_, _, v), t in zip(pieces, terms):
        e = jnp.exp2(t - m2)
        ls = jnp.sum(e, axis=-1, keepdims=True)
        den = ls if den is None else den + ls
        o = _dot_blocks(e.astype(BF16), v)
        out = o if out is None else out + o
    return out / den


def _lane_group(col):
    g0 = col // LANES * LANES
    return slice(g0, g0 + LANES)


def _placed(q, col):
    t, w = q.shape
    off = col % LANES
    parts = [jnp.zeros((t, off), q.dtype)] if off else []
    parts.append(q)
    if LANES - off - w:
        parts.append(jnp.zeros((t, LANES - off - w), q.dtype))
    return jnp.concatenate(parts, axis=-1)


def _pipelined(units, depth):
    started = []
    for i, (start, _) in enumerate(units):
        started.append(start())
        if i >= depth:
            units[i - depth][1](started[i - depth])
    for i in range(max(len(units) - depth, 0), len(units)):
        units[i][1](started[i])


def _head_cols(base, h):
    return slice(base + h * HEAD_DIM, base + (h + 1) * HEAD_DIM)


def _diff_head_av(scores, v_pieces, lam, g, lam_init, v_off):
    t = scores[0].shape[0] // 2
    mx = None
    for s in scores:
        ms = jnp.max(s, axis=-1, keepdims=True)
        mx = ms if mx is None else jnp.maximum(mx, ms)
    es = [jnp.exp2(s - mx) for s in scores]
    den = None
    for e in es:
        ls = jnp.sum(e, axis=-1, keepdims=True)
        den = ls if den is None else den + ls
    w = lam * den[0:t] / den[t:2 * t]
    o = None
    for e, v in zip(es, v_pieces):
        part = _dot((e[0:t] - w * e[t:2 * t]).astype(BF16), v)
        o = part if o is None else o + part
    o = (o / den[0:t])[:, v_off:v_off + HEAD_DIM]
    ms = jnp.mean(o * o, axis=-1, keepdims=True)
    return o * lax.rsqrt(ms + LN_EPS) * g * (1.0 - lam_init)


def _a_queries(ref, rows):
    return _row_stack([_placed(ref[0, rows, _head_cols(AQ, h)], AK + (h // A_REP) * HEAD_DIM) for h in range(A_HEADS)])


def _a_sink(sink_ref, t):
    return _row_stack([jnp.full((t, 1), sink_ref[h], F32) for h in range(A_HEADS)])


def _a_store(o, o_ref, rows, left):
    t = o.shape[0] // A_HEADS
    for j in range(A_HEADS // 2):
        halves = []
        for h in (2 * j, 2 * j + 1):
            oh = o[h * t:(h + 1) * t]
            if h // A_REP != h % 2:
                oh = pltpu.roll(oh, HEAD_DIM, 1)
            halves.append(oh)
        o_ref[0, rows, YA + j * LANES:YA + (j + 1) * LANES] = jnp.where(left, halves[0], halves[1]).astype(BF16)


def _b_queries(ref, rows, h, lane):
    qcol = BQ + h * 2 * B_QK_DIM
    qg = ref[0, rows, _lane_group(qcol)]
    zero = jnp.zeros_like(qg)
    offs = [qcol % LANES + m * B_QK_DIM for m in range(2)]
    return _row_stack([jnp.where((lane >= off) & (lane < off + B_QK_DIM), qg, zero) for off in offs])


def _c_queries(ref, rows, j, left):
    qg = ref[0, rows, _lane_group(CQ + j * LANES)]
    zero = jnp.zeros_like(qg)
    return _row_stack([jnp.where(left, qg, zero), jnp.where(left, zero, qg)])


def _attn_kernel(sink_ref, p_ref, pc_ref, lam_ref, g_ref, tb_ref, o_ref, *, lam_init, seq):
    n = pl.program_id(1)
    units_a, units_b = [], []
    lane = lax.broadcasted_iota(jnp.int32, (1, LANES), 1)
    left = lane < HEAD_DIM

    nb = seq // WIN_BLK
    for sub in range(Q_BLK // WIN_BLK):
        blk = n * (Q_BLK // WIN_BLK) + sub
        q0 = pl.multiple_of(blk * WIN_BLK, WIN_BLK)
        ws = pl.multiple_of(jnp.clip(blk - 1, 0, nb - 3) * WIN_BLK, WIN_BLK)
        qpos = q0 + lax.broadcasted_iota(jnp.int32, (A_HEADS * WIN_BLK, 3 * WIN_BLK), 0) % WIN_BLK
        kpos = ws + lax.broadcasted_iota(jnp.int32, (A_HEADS * WIN_BLK, 3 * WIN_BLK), 1)
        valid = jnp.abs(qpos - kpos) <= WINDOW
        rows = slice(sub * WIN_BLK, (sub + 1) * WIN_BLK)

        def start(q0=q0, ws=ws, valid=valid):
            q = _a_queries(p_ref, pl.ds(q0, WIN_BLK))
            kw = p_ref[0, pl.ds(ws, 3 * WIN_BLK), _lane_group(AK)]
            kc = pc_ref[0, :, _lane_group(AK)]
            return jnp.where(valid, _dot_nt(q, kw), NEG_INF), _dot_nt(q, kc)

        def finish(scores, ws=ws, rows=rows):
            vw = p_ref[0, pl.ds(ws, 3 * WIN_BLK), _lane_group(AV)]
            vc = pc_ref[0, :, _lane_group(AV)]
            o = _softmax_av([(scores[0], None, vw), (scores[1], None, vc)], extra_logit=_a_sink(sink_ref, WIN_BLK))
            _a_store(o, o_ref, rows, left)

        units_a.append((start, finish))

    lam = _lam_value(lam_ref, lam_init)
    qb = pl.multiple_of(n * Q_BLK, Q_BLK)
    for h in range(B_HEADS):
        def start(h=h):
            q = _b_queries(p_ref, pl.ds(qb, Q_BLK), h, lane)
            kcol = BK + h * 2 * B_QK_DIM
            k_lat = p_ref[0, :, _lane_group(kcol)]
            k_ctx = pc_ref[0, :, _lane_group(kcol)]
            return _dot_nt(q, k_lat), _dot_nt(q, k_ctx)

        def finish(scores, h=h):
            vcol = BV + h * HEAD_DIM
            v_lat = p_ref[0, :, _lane_group(vcol)]
            v_ctx = pc_ref[0, :, _lane_group(vcol)]
            o = _diff_head_av(scores, [v_lat, v_ctx], lam, g_ref[...], lam_init, vcol % LANES)
            o_ref[0, :, _head_cols(YB, h)] = o.astype(BF16)

        units_b.append((start, finish))

    n_rows = seq // GRID_W
    blocks = []
    for i in range(Q_BLK // GRID_W):
        r = n * (Q_BLK // GRID_W) + i
        rs = jnp.clip(r - NA_KH // 2, 0, n_rows - NA_KH)
        dr0 = (NA_KH - 1) - (r - rs)
        qr = pl.multiple_of(r * GRID_W, GRID_W)
        kr = pl.multiple_of(rs * GRID_W, GRID_W)
        blocks += [(i, j, qr, kr, dr0) for j in range(C_HEADS // 2)]

    def start_c():
        s_w, s_c = [], []
        for _, j, qr, kr, _ in blocks:
            q = _c_queries(p_ref, pl.ds(qr, GRID_W), j, left)
            s_w.append(_dot_nt(q, p_ref[0, pl.ds(kr, NA_KH * GRID_W), _lane_group(CK + j * LANES)]))
            s_c.append(_dot_nt(q, pc_ref[0, :, _lane_group(CK + j * LANES)]))
        return _row_stack(s_w), _row_stack(s_c)

    def finish_c(scores):
        bias2 = _row_stack([
            jnp.concatenate([jnp.where(left, tb_ref[h, dr0 + kh], tb_ref[h, dr0 + kh + 1])
                             for kh in range(0, NA_KH, 2)], axis=-1)
            for _, j, _, _, dr0 in blocks for h in (2 * j, 2 * j + 1)])
        vw = [p_ref[0, pl.ds(kr, NA_KH * GRID_W), _lane_group(CV + j * LANES)] for _, j, _, kr, _ in blocks]
        vc = [pc_ref[0, :, _lane_group(CV + j * LANES)] for _, j, _, _, _ in blocks]
        o = _softmax_av([(scores[0], bias2, vw), (scores[1], None, vc)])
        for k, (i, j, _, _, _) in enumerate(blocks):
            ok = o[2 * k * GRID_W:2 * (k + 1) * GRID_W]
            o_ref[0, i * GRID_W:(i + 1) * GRID_W, YC + j * LANES:YC + (j + 1) * LANES] = (
                jnp.where(left, ok[0:GRID_W], ok[GRID_W:2 * GRID_W]).astype(BF16))

    _pipelined([(start_c, finish_c)] + units_a + units_b, ATTN_LOOKAHEAD)


def _attn(p, pc, sink_l, lamv, g, tb, lam_init):
    bsz, seq, _ = p.shape
    n_ctx = pc.shape[1]
    return pl.pallas_call(
        functools.partial(_attn_kernel, lam_init=lam_init, seq=seq),
        grid=(bsz, seq // Q_BLK),
        in_specs=[
            pl.BlockSpec(memory_space=pltpu.SMEM),
            pl.BlockSpec((1, seq, PROJ_WIDTH), lambda b, n: (b, 0, 0)),
            pl.BlockSpec((1, n_ctx, PROJ_WIDTH), lambda b, n: (b, 0, 0)),
            pl.BlockSpec((4, B_QK_DIM), lambda b, n: (0, 0)),
            pl.BlockSpec((1, HEAD_DIM), lambda b, n: (0, 0)),
            pl.BlockSpec(tb.shape, lambda b, n: (0, 0, 0, 0)),
        ],
        out_specs=pl.BlockSpec((1, Q_BLK, MIX_WIDTH), lambda b, n: (b, n, 0)),
        out_shape=jax.ShapeDtypeStruct((bsz, seq, MIX_WIDTH), BF16),
        compiler_params=pltpu.CompilerParams(vmem_limit_bytes=VMEM_LIMIT),
        name="attn",
    )(sink_l, p, pc, lamv, g, tb)


def _ctx_attn_kernel(sink_ref, pc_ref, lam_ref, g_ref, o_ref, *, lam_init):
    t = pc_ref.shape[1]
    every = slice(0, t)
    lane = lax.broadcasted_iota(jnp.int32, (1, LANES), 1)
    left = lane < HEAD_DIM
    lam = _lam_value(lam_ref, lam_init)

    def start_a():
        return _dot_nt(_a_queries(pc_ref, every), pc_ref[0, :, _lane_group(AK)])

    def finish_a(s):
        o = _softmax_av([(s, None, pc_ref[0, :, _lane_group(AV)])], extra_logit=_a_sink(sink_ref, t))
        _a_store(o, o_ref, every, left)

    units = [(start_a, finish_a)]
    for h in range(B_HEADS):
        def start(h=h):
            return (_dot_nt(_b_queries(pc_ref, every, h, lane), pc_ref[0, :, _lane_group(BK + h * HEAD_DIM)]),)

        def finish(scores, h=h):
            vcol = BV + h * HEAD_DIM
            o = _diff_head_av(scores, [pc_ref[0, :, _lane_group(vcol)]], lam, g_ref[...], lam_init, vcol % LANES)
            o_ref[0, :, _head_cols(YB, h)] = o.astype(BF16)

        units.append((start, finish))

    def start_c():
        return _row_stack([_dot_nt(_c_queries(pc_ref, every, j, left), pc_ref[0, :, _lane_group(CK + j * LANES)])
                           for j in range(C_HEADS // 2)])

    def finish_c(s):
        o = _softmax_av([(s, None, [pc_ref[0, :, _lane_group(CV + j * LANES)] for j in range(C_HEADS // 2)])])
        for j in range(C_HEADS // 2):
            oj = o[2 * j * t:2 * (j + 1) * t]
            o_ref[0, :, YC + j * LANES:YC + (j + 1) * LANES] = jnp.where(left, oj[0:t], oj[t:2 * t]).astype(BF16)

    units.append((start_c, finish_c))
    _pipelined(units, ATTN_LOOKAHEAD)


def _ctx_attn(pc, sink_l, lamv, g, lam_init):
    bsz, n_ctx, _ = pc.shape
    return pl.pallas_call(
        functools.partial(_ctx_attn_kernel, lam_init=lam_init),
        grid=(bsz,),
        in_specs=[
            pl.BlockSpec(memory_space=pltpu.SMEM),
            pl.BlockSpec((1, n_ctx, PROJ_WIDTH), lambda b: (b, 0, 0)),
            pl.BlockSpec((4, B_QK_DIM), lambda b: (0, 0)),
            pl.BlockSpec((1, HEAD_DIM), lambda b: (0, 0)),
        ],
        out_specs=pl.BlockSpec((1, n_ctx, MIX_WIDTH), lambda b: (b, 0, 0)),
        out_shape=jax.ShapeDtypeStruct((bsz, n_ctx, MIX_WIDTH), BF16),
        compiler_params=pltpu.CompilerParams(vmem_limit_bytes=VMEM_LIMIT),
        name="ctx_attn",
    )(sink_l, pc, lamv, g)


N_DR = 2 * NA_KH - 1
N_DC = 2 * NA_KW - 1


def _na_table_kernel(nb_ref, o_ref):
    base = (pl.program_id(0) * C_HEADS + pl.program_id(1)) * (N_DR * N_DC)
    wq = lax.broadcasted_iota(jnp.int32, (GRID_W, LANES), 0)
    wk = lax.broadcasted_iota(jnp.int32, (GRID_W, LANES), 1) % GRID_W
    dc = jnp.clip(wk - wq, -(NA_KW - 1), NA_KW - 1) + (NA_KW - 1)
    cs = jnp.clip(wq - NA_KW // 2, 0, GRID_W - NA_KW)
    valid = (wk >= cs) & (wk < cs + NA_KW)
    for dr in range(N_DR):
        acc = jnp.zeros((GRID_W, LANES), F32)
        for c in range(N_DC):
            acc = jnp.where(dc == c, nb_ref[base + dr * N_DC + c], acc)
        o_ref[0, 0, dr] = jnp.where(valid, acc * LOG2E, NEG_INF)


def _na_table(na_bias):
    return pl.pallas_call(
        _na_table_kernel,
        grid=(DEPTH, C_HEADS),
        in_specs=[pl.BlockSpec(memory_space=pltpu.SMEM)],
        out_specs=pl.BlockSpec((1, 1, N_DR, GRID_W, LANES), lambda l, h: (l, h, 0, 0, 0)),
        out_shape=jax.ShapeDtypeStruct((DEPTH, C_HEADS, N_DR, GRID_W, LANES), F32),
        name="na_table",
    )(na_bias.reshape(-1))


def _outffn_kernel(x_ref, y_ref, g1_ref, sh2_ref, sc2_ref, g2_ref, wo_ref, ln1g_ref, ln1b_ref, ln2g_ref, ln2b_ref,
                   wg_ref, wu_ref, wd_ref, o_ref, *, sub):
    def rows(u):
        return slice(u * sub, (u + 1) * sub)

    def out_proj(u, _):
        return _dot(y_ref[0, rows(u)], wo_ref[...])

    def norm_gate_up(u, y):
        xn = _layer_norm(ALPHA * x_ref[0, rows(u)] + g1_ref[0] * y, ln1g_ref[...], ln1b_ref[...])
        h = (xn * (1.0 + sc2_ref[0]) + sh2_ref[0]).astype(BF16)
        return xn, _dot(h, wg_ref[...]), _dot(h, wu_ref[...])

    def act_down(u, st):
        xn, gate, up = st
        return xn, _dot((_silu(gate) * up).astype(BF16), wd_ref[...])

    def norm_store(u, st):
        xn, ff = st
        o_ref[0, rows(u)] = _layer_norm(ALPHA * xn + g2_ref[0] * ff, ln2g_ref[...], ln2b_ref[...])

    _staged(x_ref.shape[1] // sub, [out_proj, norm_gate_up, act_down, norm_store])


def _outffn(xs, y, mod, wo_b, ln1g, ln1b, ln2g, ln2b, wfi_b, wfo_b, *, ctx):
    bsz, seq, _ = xs.shape
    t = min(FFN_TILE, seq)
    row = (lambda b: CTX_ROW) if ctx else (lambda b: b)
    once = pl.Buffered(1)

    def mod_spec(k):
        return pl.BlockSpec((1, 1, D_MODEL), lambda b, i: (row(b), 0, k))

    vec = pl.BlockSpec((1, D_MODEL), lambda b, i: (0, 0))
    return pl.pallas_call(
        functools.partial(_outffn_kernel, sub=min(SUB_TILE, t)),
        grid=(bsz, seq // t),
        in_specs=[
            pl.BlockSpec((1, t, D_MODEL), lambda b, i: (b, i, 0)),
            pl.BlockSpec((1, t, MIX_WIDTH), lambda b, i: (b, i, 0)),
            mod_spec(2), mod_spec(3), mod_spec(4), mod_spec(5),
            pl.BlockSpec((MIX_WIDTH, D_MODEL), lambda b, i: (0, 0), pipeline_mode=once),
            vec, vec, vec, vec,
            pl.BlockSpec((D_MODEL, D_FF), lambda b, i: (0, 0), pipeline_mode=once),
            pl.BlockSpec((D_MODEL, D_FF), lambda b, i: (0, 1), pipeline_mode=once),
            pl.BlockSpec((D_FF, D_MODEL), lambda b, i: (0, 0), pipeline_mode=once),
        ],
        out_specs=pl.BlockSpec((1, t, D_MODEL), lambda b, i: (b, i, 0)),
        out_shape=jax.ShapeDtypeStruct((bsz, seq, D_MODEL), F32),
        compiler_params=pltpu.CompilerParams(vmem_limit_bytes=VMEM_LIMIT),
        name="outffn_ctx" if ctx else "outffn",
    )(xs, y, mod, mod, mod, mod, wo_b, ln1g, ln1b, ln2g, ln2b, wfi_b, wfi_b, wfo_b)


def kernel(x, c, ctx, c_ctx, w_ada, b_ada, w_in, w_o, sink, lam_q1, lam_k1, lam_q2, lam_k2, subln_g, na_bias,
           ln1_g, ln1_b, w_ffn_in, w_ffn_out, ln2_g, ln2_b):
    bsz, seq, _ = x.shape
    assert x.shape == (bsz, seq, D_MODEL) and seq % Q_BLK == 0 and seq // Q_BLK >= 3
    assert bsz < CTX_ROW + 1 <= MOD_ROWS
    cc = jnp.zeros((MOD_ROWS, D_MODEL), F32).at[:bsz].set(c).at[CTX_ROW].set(c_ctx)
    mod_all = _ada(cc, w_ada, b_ada).reshape(DEPTH, MOD_ROWS, 1, N_MOD * D_MODEL)
    tables = _rope_tables(seq)
    na_tab = _na_table(na_bias)

    xs, cs = x, ctx
    for l in range(DEPTH):
        last = l == DEPTH - 1
        lam_init = 0.8 - 0.6 * math.exp(-0.3 * l)
        mod = mod_all[l]
        w_in_b = w_in[l].astype(BF16)
        wo_b = w_o[l].astype(BF16)
        wfi_b = w_ffn_in[l].astype(BF16)
        wfo_b = w_ffn_out[l].astype(BF16)
        lamv = jnp.stack([lam_q1[l], lam_k1[l], lam_q2[l], lam_k2[l]])
        g = subln_g[l].reshape(1, HEAD_DIM)
        lnp = [v[l].reshape(1, D_MODEL) for v in (ln1_g, ln1_b, ln2_g, ln2_b)]

        p = _inproj(xs, mod, w_in_b, tables, ctx=False)
        pc = _inproj(cs, mod, w_in_b, None, ctx=True)
        y = _attn(p, pc, sink[l], lamv, g, na_tab[l], lam_init)
        xn = _outffn(xs, y, mod, wo_b, *lnp, wfi_b, wfo_b, ctx=False)
        if not last:
            yc = _ctx_attn(pc, sink[l], lamv, g, lam_init)
            cs = _outffn(cs, yc, mod, wo_b, *lnp, wfi_b, wfo_b, ctx=True)
        xs = xn
    return xs
```

```python
import functools
import math

import jax
import jax.numpy as jnp
import numpy as np
from jax import lax
from jax.experimental import pallas as pl
from jax.experimental.pallas import tpu as pltpu

F32 = jnp.float32
BF16 = jnp.bfloat16

D_MODEL = 1024
DEPTH = 2
GRID_W = 64
HEAD_DIM = 64
A_HEADS = 6
A_KV_HEADS = 2
A_REP = A_HEADS // A_KV_HEADS
WINDOW = 128
WIN_BLK = 128
B_HEADS = 4
B_QK_DIM = 32
C_HEADS = 6
NA_KH = 8
NA_KW = 16
N_MOD = 6
D_FF = 2816
MIX_WIDTH = (A_HEADS + B_HEADS + C_HEADS) * HEAD_DIM
ROPE_BASE = 10000.0
LN_EPS = 1e-5
NEG_INF = -1e30
LOG2E = 1.4426950408889634
QK_SCALE2_A = HEAD_DIM ** -0.5 * LOG2E
QK_SCALE2_B = B_QK_DIM ** -0.5 * LOG2E
ALPHA = (2.0 * DEPTH) ** 0.25

AQ = 0
AK = AQ + A_HEADS * HEAD_DIM
AV = AK + A_KV_HEADS * HEAD_DIM
BQ = AV + A_KV_HEADS * HEAD_DIM
BK = BQ + B_HEADS * 2 * B_QK_DIM
BV = BK + B_HEADS * 2 * B_QK_DIM
CQ = BV + B_HEADS * HEAD_DIM
CK = CQ + C_HEADS * HEAD_DIM
CV = CK + C_HEADS * HEAD_DIM
PROJ_WIDTH = CV + C_HEADS * HEAD_DIM

YA = 0
YB = YA + A_HEADS * HEAD_DIM
YC = YB + B_HEADS * HEAD_DIM

LANES = 128
MOD_ROWS = 16
CTX_ROW = 8
VMEM_LIMIT = 56 * 1024 * 1024

Q_BLK = 256
ATTN_LOOKAHEAD = 2
IN_TILE = 512
FFN_TILE = 512
CAST_BLOCK_BYTES =4 * 1024 * 1024
SUB_TILE = 256


def _dot(a, b):
    return jnp.dot(a, b, preferred_element_type=F32)


def _dot_nt(a, b):
    return lax.dot_general(a, b, (((1,), (1,)), ((), ())), preferred_element_type=F32)


def _silu(v):
    return v / (1.0 + jnp.exp(-v))


def _layer_norm(v, g, b):
    mu = jnp.mean(v, axis=-1, keepdims=True)
    d = v - mu
    var = jnp.mean(d * d, axis=-1, keepdims=True)
    return d * lax.rsqrt(var + LN_EPS) * g + b


def _ada_kernel(c_ref, w_ref, b_ref, o_ref):
    s = _silu(c_ref[...])
    o_ref[0] = _dot(s.astype(BF16), w_ref[0].astype(BF16)) + b_ref[0]


def _ada(cc, w_ada, b_ada):
    tn = D_MODEL
    return pl.pallas_call(
        _ada_kernel,
        grid=(DEPTH, N_MOD * D_MODEL // tn),
        in_specs=[
            pl.BlockSpec((MOD_ROWS, D_MODEL), lambda l, j: (0, 0)),
            pl.BlockSpec((1, D_MODEL, tn), lambda l, j: (l, 0, j)),
            pl.BlockSpec((1, 1, tn), lambda l, j: (l, 0, j)),
        ],
        out_specs=pl.BlockSpec((1, MOD_ROWS, tn), lambda l, j: (l, 0, j)),
        out_shape=jax.ShapeDtypeStruct((DEPTH, MOD_ROWS, N_MOD * D_MODEL), F32),
        compiler_params=pltpu.CompilerParams(vmem_limit_bytes=VMEM_LIMIT),
        name="ada",
    )(cc, w_ada, b_ada.reshape(DEPTH, 1, N_MOD * D_MODEL))


def _cast_kernel(w_ref, o_ref):
    o_ref[...] = w_ref[...].astype(BF16)


def _cast_bf16(w):
    depth, rows, cols = w.shape
    blk = rows
    while blk * cols * 4 > CAST_BLOCK_BYTES and blk % 32 == 0:
        blk //= 2
    return pl.pallas_call(
        _cast_kernel,
        grid=(depth, rows // blk),
        in_specs=[pl.BlockSpec((1, blk, cols), lambda l, i: (l, i, 0))],
        out_specs=pl.BlockSpec((1, blk, cols), lambda l, i: (l, i, 0)),
        out_shape=jax.ShapeDtypeStruct(w.shape, BF16),
        name="cast_bf16",
    )(w)


def _rope_group(v, cos, sin, off):
    lane = lax.broadcasted_iota(jnp.int32, v.shape, 1)
    low = (lane % (2 * off)) < off
    partner = jnp.where(low, pltpu.roll(v, LANES - off, 1), pltpu.roll(v, off, 1))
    return v * cos + partner * sin


def _staged(n_units, stages):
    state = [None] * n_units
    for step in range(n_units + len(stages) - 1):
        for k, stage in enumerate(stages):
            u = step - k
            if 0 <= u < n_units:
                state[u] = stage(u, state[u])


def _lane_groups(lo, hi):
    return tuple(range(lo // LANES, hi // LANES))


_GROUP_PLAN = {}
for _g in _lane_groups(AQ, AK):
    _GROUP_PLAN[_g] = (0, HEAD_DIM // 4, QK_SCALE2_A)
for _g in _lane_groups(AK, AV):
    _GROUP_PLAN[_g] = (1, HEAD_DIM // 4, None)
for _g in _lane_groups(BQ, BK):
    _GROUP_PLAN[_g] = (2, B_QK_DIM // 4, QK_SCALE2_B)
for _g in _lane_groups(BK, BV):
    _GROUP_PLAN[_g] = (3, B_QK_DIM // 4, None)
for _g in _lane_groups(CQ, CK):
    _GROUP_PLAN[_g] = (None, None, QK_SCALE2_A)


def _inproj_kernel(x_ref, sh_ref, sc_ref, w_ref, *rest, rope, sub):
    if rope:
        tab_ref, o_ref = rest
    else:
        (o_ref,) = rest

    def rows(u):
        return slice(u * sub, (u + 1) * sub)

    def project(u, _):
        h = x_ref[0, rows(u)] * (1.0 + sc_ref[0]) + sh_ref[0]
        return _dot(h.astype(BF16), w_ref[0])

    def rotate_store(u, p):
        for g in range(PROJ_WIDTH // LANES):
            v = p[:, g * LANES:(g + 1) * LANES]
            pair, quarter, qscale = _GROUP_PLAN.get(g, (None, None, None))
            if rope and pair is not None:
                v = _rope_group(v, tab_ref[2 * pair, rows(u)], tab_ref[2 * pair + 1, rows(u)], quarter)
            elif qscale is not None:
                v = v * qscale
            o_ref[0, rows(u), g * LANES:(g + 1) * LANES] = v.astype(BF16)

    _staged(x_ref.shape[1] // sub, [project, rotate_store])


def _inproj(xs, mod, w_in_b, layer, tables, *, ctx):
    bsz, seq, _ = xs.shape
    t = min(IN_TILE, seq)
    row = (lambda b: CTX_ROW) if ctx else (lambda b: b)
    in_specs = [
        pl.BlockSpec((1, t, D_MODEL), lambda b, i: (b, i, 0)),
        pl.BlockSpec((1, 1, D_MODEL), lambda b, i: (row(b), 0, 0)),
        pl.BlockSpec((1, 1, D_MODEL), lambda b, i: (row(b), 0, 1)),
        pl.BlockSpec((1, D_MODEL, PROJ_WIDTH), lambda b, i: (layer, 0, 0), pipeline_mode=pl.Buffered(1)),
    ]
    args = [xs, mod, mod, w_in_b]
    if tables is not None:
        in_specs.append(pl.BlockSpec((tables.shape[0], t, LANES), lambda b, i: (0, i, 0)))
        args.append(tables)
    return pl.pallas_call(
        functools.partial(_inproj_kernel, rope=tables is not None, sub=min(SUB_TILE, t)),
        grid=(bsz, seq // t),
        in_specs=in_specs,
        out_specs=pl.BlockSpec((1, t, PROJ_WIDTH), lambda b, i: (b, i, 0)),
        out_shape=jax.ShapeDtypeStruct((bsz, seq, PROJ_WIDTH), BF16),
        compiler_params=pltpu.CompilerParams(vmem_limit_bytes=VMEM_LIMIT),
        name="inproj_ctx" if ctx else "inproj",
    )(*args)


def _rope_tables(seq):
    f32 = np.float32
    tpos = np.arange(seq, dtype=np.int32)
    rows = (tpos // GRID_W).astype(f32)[:, None]
    cols = (tpos % GRID_W).astype(f32)[:, None]
    lane = np.arange(LANES, dtype=np.int32)

    def table(head_dim):
        quarter = head_dim // 4
        inv = f32(ROPE_BASE) ** (-np.arange(quarter, dtype=f32) / f32(quarter))
        freq = inv[lane % quarter][None, :].astype(f32)
        use_cols = ((lane % head_dim) >= head_dim // 2)[None, :]
        ang = np.where(use_cols, cols * freq, rows * freq).astype(f32)
        sign = np.where((lane % (2 * quarter)) < quarter, f32(-1.0), f32(1.0))[None, :]
        return np.cos(ang).astype(f32), (np.sin(ang) * sign).astype(f32)

    cos_a, sin_a = table(HEAD_DIM)
    cos_b, sin_b = table(B_QK_DIM)
    return np.stack([cos_a * f32(QK_SCALE2_A), sin_a * f32(QK_SCALE2_A), cos_a, sin_a,
                     cos_b * f32(QK_SCALE2_B), sin_b * f32(QK_SCALE2_B), cos_b, sin_b]).astype(f32)


def _lam_value(lam_ref, lam_init):
    v = lam_ref[...]
    s1 = jnp.sum(v[0:1, :] * v[1:2, :], axis=-1, keepdims=True)
    s2 = jnp.sum(v[2:3, :] * v[3:4, :], axis=-1, keepdims=True)
    return jnp.exp(s1) - jnp.exp(s2) + lam_init


def _row_stack(parts):
    return jnp.concatenate(parts, axis=0)


def _dot_blocks(e, v):
    if not isinstance(v, (list, tuple)):
        return _dot(e, v)
    t = e.shape[0] // len(v)
    return _row_stack([_dot(e[k * t:(k + 1) * t], vk) for k, vk in enumerate(v)])


def _softmax_av(pieces, extra_logit=None):
    terms, m2 = [], None
    for s, bias2, _ in pieces:
        t = s if bias2 is None else s + bias2
        ms = jnp.max(t, axis=-1, keepdims=True)
        terms.append(t)
        m2 = ms if m2 is None else jnp.maximum(m2, ms)
    den = None
    if extra_logit is not None:
        m2 = jnp.maximum(m2, extra_logit * LOG2E)
        den = jnp.exp2(extra_logit * LOG2E - m2)
    out = None
    for (_, _, v), t in zip(pieces, terms):
        e = jnp.exp2(t - m2)
        ls = jnp.sum(e, axis=-1, keepdims=True)
        den = ls if den is None else den + ls
        o = _dot_blocks(e.astype(BF16), v)
        out = o if out is None else out + o
    return out / den


def _lane_group(col):
    g0 = col // LANES * LANES
    return slice(g0, g0 + LANES)


def _placed(q, col):
    t, w = q.shape
    off = col % LANES
    parts = [jnp.zeros((t, off), q.dtype)] if off else []
    parts.append(q)
    if LANES - off - w:
        parts.append(jnp.zeros((t, LANES - off - w), q.dtype))
    return jnp.concatenate(parts, axis=-1)


def _pipelined(units, depth):
    started = []
    for i, (start, _) in enumerate(units):
        started.append(start())
        if i >= depth:
            units[i - depth][1](started[i - depth])
    for i in range(max(len(units) - depth, 0), len(units)):
        units[i][1](started[i])


def _head_cols(base, h):
    return slice(base + h * HEAD_DIM, base + (h + 1) * HEAD_DIM)


def _diff_head_av(scores, v_pieces, lam, g, lam_init, v_off):
    t = scores[0].shape[0] // 2
    mx = None
    for s in scores:
        ms = jnp.max(s, axis=-1, keepdims=True)
        mx = ms if mx is None else jnp.maximum(mx, ms)
    es = [jnp.exp2(s - mx) for s in scores]
    den = None
    for e in es:
        ls = jnp.sum(e, axis=-1, keepdims=True)
        den = ls if den is None else den + ls
    w = lam * den[0:t] / den[t:2 * t]
    o = None
    for e, v in zip(es, v_pieces):
        part = _dot((e[0:t] - w * e[t:2 * t]).astype(BF16), v)
        o = part if o is None else o + part
    o = (o / den[0:t])[:, v_off:v_off + HEAD_DIM]
    ms = jnp.mean(o * o, axis=-1, keepdims=True)
    return o * lax.rsqrt(ms + LN_EPS) * g * (1.0 - lam_init)


def _a_queries(ref, rows):
    return _row_stack([_placed(ref[0, rows, _head_cols(AQ, h)], AK + (h // A_REP) * HEAD_DIM) for h in range(A_HEADS)])


def _a_sink(sink_ref, t):
    return _row_stack([jnp.full((t, 1), sink_ref[h], F32) for h in range(A_HEADS)])


def _a_store(o, o_ref, rows, left):
    t = o.shape[0] // A_HEADS
    for j in range(A_HEADS // 2):
        halves = []
        for h in (2 * j, 2 * j + 1):
            oh = o[h * t:(h + 1) * t]
            if h // A_REP != h % 2:
                oh = pltpu.roll(oh, HEAD_DIM, 1)
            halves.append(oh)
        o_ref[0, rows, YA + j * LANES:YA + (j + 1) * LANES] = jnp.where(left, halves[0], halves[1]).astype(BF16)


def _b_queries(ref, rows, h, lane):
    qcol = BQ + h * 2 * B_QK_DIM
    qg = ref[0, rows, _lane_group(qcol)]
    zero = jnp.zeros_like(qg)
    offs = [qcol % LANES + m * B_QK_DIM for m in range(2)]
    return _row_stack([jnp.where((lane >= off) & (lane < off + B_QK_DIM), qg, zero) for off in offs])


def _c_queries(ref, rows, j, left):
    qg = ref[0, rows, _lane_group(CQ + j * LANES)]
    zero = jnp.zeros_like(qg)
    return _row_stack([jnp.where(left, qg, zero), jnp.where(left, zero, qg)])


def _attn_kernel(sink_ref, p_ref, pc_ref, lam_ref, g_ref, tb_ref, o_ref, *, lam_init, seq):
    n = pl.program_id(1)
    units_a, units_b = [], []
    lane = lax.broadcasted_iota(jnp.int32, (1, LANES), 1)
    left = lane < HEAD_DIM

    nb = seq // WIN_BLK
    for sub in range(Q_BLK // WIN_BLK):
        blk = n * (Q_BLK // WIN_BLK) + sub
        q0 = pl.multiple_of(blk * WIN_BLK, WIN_BLK)
        ws = pl.multiple_of(jnp.clip(blk - 1, 0, nb - 3) * WIN_BLK, WIN_BLK)
        qpos = q0 + lax.broadcasted_iota(jnp.int32, (A_HEADS * WIN_BLK, 3 * WIN_BLK), 0) % WIN_BLK
        kpos = ws + lax.broadcasted_iota(jnp.int32, (A_HEADS * WIN_BLK, 3 * WIN_BLK), 1)
        valid = jnp.abs(qpos - kpos) <= WINDOW
        rows = slice(sub * WIN_BLK, (sub + 1) * WIN_BLK)

        def start(q0=q0, ws=ws, valid=valid):
            q = _a_queries(p_ref, pl.ds(q0, WIN_BLK))
            kw = p_ref[0, pl.ds(ws, 3 * WIN_BLK), _lane_group(AK)]
            kc = pc_ref[0, :, _lane_group(AK)]
            return jnp.where(valid, _dot_nt(q, kw), NEG_INF), _dot_nt(q, kc)

        def finish(scores, ws=ws, rows=rows):
            vw = p_ref[0, pl.ds(ws, 3 * WIN_BLK), _lane_group(AV)]
            vc = pc_ref[0, :, _lane_group(AV)]
            o = _softmax_av([(scores[0], None, vw), (scores[1], None, vc)], extra_logit=_a_sink(sink_ref, WIN_BLK))
            _a_store(o, o_ref, rows, left)

        units_a.append((start, finish))

    lam = _lam_value(lam_ref, lam_init)
    qb = pl.multiple_of(n * Q_BLK, Q_BLK)
    for h in range(B_HEADS):
        def start(h=h):
            q = _b_queries(p_ref, pl.ds(qb, Q_BLK), h, lane)
            kcol = BK + h * 2 * B_QK_DIM
            k_lat = p_ref[0, :, _lane_group(kcol)]
            k_ctx = pc_ref[0, :, _lane_group(kcol)]
            return _dot_nt(q, k_lat), _dot_nt(q, k_ctx)

        def finish(scores, h=h):
            vcol = BV + h * HEAD_DIM
            v_lat = p_ref[0, :, _lane_group(vcol)]
            v_ctx = pc_ref[0, :, _lane_group(vcol)]
            o = _diff_head_av(scores, [v_lat, v_ctx], lam, g_ref[...], lam_init, vcol % LANES)
            o_ref[0, :, _head_cols(YB, h)] = o.astype(BF16)

        units_b.append((start, finish))

    n_rows = seq // GRID_W
    blocks = []
    for i in range(Q_BLK // GRID_W):
        r = n * (Q_BLK // GRID_W) + i
        rs = jnp.clip(r - NA_KH // 2, 0, n_rows - NA_KH)
        dr0 = (NA_KH - 1) - (r - rs)
        qr = pl.multiple_of(r * GRID_W, GRID_W)
        kr = pl.multiple_of(rs * GRID_W, GRID_W)
        blocks += [(i, j, qr, kr, dr0) for j in range(C_HEADS // 2)]

    def start_c():
        s_w, s_c = [], []
        for _, j, qr, kr, _ in blocks:
            q = _c_queries(p_ref, pl.ds(qr, GRID_W), j, left)
            s_w.append(_dot_nt(q, p_ref[0, pl.ds(kr, NA_KH * GRID_W), _lane_group(CK + j * LANES)]))
            s_c.append(_dot_nt(q, pc_ref[0, :, _lane_group(CK + j * LANES)]))
        return _row_stack(s_w), _row_stack(s_c)

    def finish_c(scores):
        bias2 = _row_stack([
            jnp.concatenate([jnp.where(left, tb_ref[h, dr0 + kh], tb_ref[h, dr0 + kh + 1])
                             for kh in range(0, NA_KH, 2)], axis=-1)
            for _, j, _, _, dr0 in blocks for h in (2 * j, 2 * j + 1)])
        vw = [p_ref[0, pl.ds(kr, NA_KH * GRID_W), _lane_group(CV + j * LANES)] for _, j, _, kr, _ in blocks]
        vc = [pc_ref[0, :, _lane_group(CV + j * LANES)] for _, j, _, _, _ in blocks]
        o = _softmax_av([(scores[0], bias2, vw), (scores[1], None, vc)])
        for k, (i, j, _, _, _) in enumerate(blocks):
            ok = o[2 * k * GRID_W:2 * (k + 1) * GRID_W]
            o_ref[0, i * GRID_W:(i + 1) * GRID_W, YC + j * LANES:YC + (j + 1) * LANES] = (
                jnp.where(left, ok[0:GRID_W], ok[GRID_W:2 * GRID_W]).astype(BF16))

    _pipelined([(start_c, finish_c)] + units_a + units_b, ATTN_LOOKAHEAD)


def _attn(p, pc, sink_l, lamv, g, tb, lam_init):
    bsz, seq, _ = p.shape
    n_ctx = pc.shape[1]
    return pl.pallas_call(
        functools.partial(_attn_kernel, lam_init=lam_init, seq=seq),
        grid=(bsz, seq // Q_BLK),
        in_specs=[
            pl.BlockSpec(memory_space=pltpu.SMEM),
            pl.BlockSpec((1, seq, PROJ_WIDTH), lambda b, n: (b, 0, 0)),
            pl.BlockSpec((1, n_ctx, PROJ_WIDTH), lambda b, n: (b, 0, 0)),
            pl.BlockSpec((4, B_QK_DIM), lambda b, n: (0, 0)),
            pl.BlockSpec((1, HEAD_DIM), lambda b, n: (0, 0)),
            pl.BlockSpec(tb.shape, lambda b, n: (0, 0, 0, 0)),
        ],
        out_specs=pl.BlockSpec((1, Q_BLK, MIX_WIDTH), lambda b, n: (b, n, 0)),
        out_shape=jax.ShapeDtypeStruct((bsz, seq, MIX_WIDTH), BF16),
        compiler_params=pltpu.CompilerParams(vmem_limit_bytes=VMEM_LIMIT),
        name="attn",
    )(sink_l, p, pc, lamv, g, tb)


def _ctx_attn_kernel(sink_ref, pc_ref, lam_ref, g_ref, o_ref, *, lam_init):
    t = pc_ref.shape[1]
    every = slice(0, t)
    lane = lax.broadcasted_iota(jnp.int32, (1, LANES), 1)
    left = lane < HEAD_DIM
    lam = _lam_value(lam_ref, lam_init)

    def start_a():
        return _dot_nt(_a_queries(pc_ref, every), pc_ref[0, :, _lane_group(AK)])

    def finish_a(s):
        o = _softmax_av([(s, None, pc_ref[0, :, _lane_group(AV)])], extra_logit=_a_sink(sink_ref, t))
        _a_store(o, o_ref, every, left)

    units = [(start_a, finish_a)]
    for h in range(B_HEADS):
        def start(h=h):
            return (_dot_nt(_b_queries(pc_ref, every, h, lane), pc_ref[0, :, _lane_group(BK + h * HEAD_DIM)]),)

        def finish(scores, h=h):
            vcol = BV + h * HEAD_DIM
            o = _diff_head_av(scores, [pc_ref[0, :, _lane_group(vcol)]], lam, g_ref[...], lam_init, vcol % LANES)
            o_ref[0, :, _head_cols(YB, h)] = o.astype(BF16)

        units.append((start, finish))

    def start_c():
        return _row_stack([_dot_nt(_c_queries(pc_ref, every, j, left), pc_ref[0, :, _lane_group(CK + j * LANES)])
                           for j in range(C_HEADS // 2)])

    def finish_c(s):
        o = _softmax_av([(s, None, [pc_ref[0, :, _lane_group(CV + j * LANES)] for j in range(C_HEADS // 2)])])
        for j in range(C_HEADS // 2):
            oj = o[2 * j * t:2 * (j + 1) * t]
            o_ref[0, :, YC + j * LANES:YC + (j + 1) * LANES] = jnp.where(left, oj[0:t], oj[t:2 * t]).astype(BF16)

    units.append((start_c, finish_c))
    _pipelined(units, ATTN_LOOKAHEAD)


def _ctx_attn(pc, sink_l, lamv, g, lam_init):
    bsz, n_ctx, _ = pc.shape
    return pl.pallas_call(
        functools.partial(_ctx_attn_kernel, lam_init=lam_init),
        grid=(bsz,),
        in_specs=[
            pl.BlockSpec(memory_space=pltpu.SMEM),
            pl.BlockSpec((1, n_ctx, PROJ_WIDTH), lambda b: (b, 0, 0)),
            pl.BlockSpec((4, B_QK_DIM), lambda b: (0, 0)),
            pl.BlockSpec((1, HEAD_DIM), lambda b: (0, 0)),
        ],
        out_specs=pl.BlockSpec((1, n_ctx, MIX_WIDTH), lambda b: (b, 0, 0)),
        out_shape=jax.ShapeDtypeStruct((bsz, n_ctx, MIX_WIDTH), BF16),
        compiler_params=pltpu.CompilerParams(vmem_limit_bytes=VMEM_LIMIT),
        name="ctx_attn",
    )(sink_l, pc, lamv, g)


N_DR = 2 * NA_KH - 1
N_DC = 2 * NA_KW - 1


def _na_table_kernel(nb_ref, o_ref):
    base = (pl.program_id(0) * C_HEADS + pl.program_id(1)) * (N_DR * N_DC)
    wq = lax.broadcasted_iota(jnp.int32, (GRID_W, LANES), 0)
    wk = lax.broadcasted_iota(jnp.int32, (GRID_W, LANES), 1) % GRID_W
    dc = jnp.clip(wk - wq, -(NA_KW - 1), NA_KW - 1) + (NA_KW - 1)
    cs = jnp.clip(wq - NA_KW // 2, 0, GRID_W - NA_KW)
    valid = (wk >= cs) & (wk < cs + NA_KW)
    for dr in range(N_DR):
        acc = jnp.zeros((GRID_W, LANES), F32)
        for c in range(N_DC):
            acc = jnp.where(dc == c, nb_ref[base + dr * N_DC + c], acc)
        o_ref[0, 0, dr] = jnp.where(valid, acc * LOG2E, NEG_INF)


def _na_table(na_bias):
    return pl.pallas_call(
        _na_table_kernel,
        grid=(DEPTH, C_HEADS),
        in_specs=[pl.BlockSpec(memory_space=pltpu.SMEM)],
        out_specs=pl.BlockSpec((1, 1, N_DR, GRID_W, LANES), lambda l, h: (l, h, 0, 0, 0)),
        out_shape=jax.ShapeDtypeStruct((DEPTH, C_HEADS, N_DR, GRID_W, LANES), F32),
        name="na_table",
    )(na_bias.reshape(-1))


def _outffn_kernel(x_ref, y_ref, g1_ref, sh2_ref, sc2_ref, g2_ref, wo_ref, ln1g_ref, ln1b_ref, ln2g_ref, ln2b_ref,
                   wg_ref, wu_ref, wd_ref, o_ref, *, sub):
    def rows(u):
        return slice(u * sub, (u + 1) * sub)

    def out_proj(u, _):
        return _dot(y_ref[0, rows(u)], wo_ref[0])

    def norm_gate_up(u, y):
        xn = _layer_norm(ALPHA * x_ref[0, rows(u)] + g1_ref[0] * y, ln1g_ref[...], ln1b_ref[...])
        h = (xn * (1.0 + sc2_ref[0]) + sh2_ref[0]).astype(BF16)
        return xn, _dot(h, wg_ref[0]), _dot(h, wu_ref[0])

    def act_down(u, st):
        xn, gate, up = st
        return xn, _dot((_silu(gate) * up).astype(BF16), wd_ref[0])

    def norm_store(u, st):
        xn, ff = st
        o_ref[0, rows(u)] = _layer_norm(ALPHA * xn + g2_ref[0] * ff, ln2g_ref[...], ln2b_ref[...])

    _staged(x_ref.shape[1] // sub, [out_proj, norm_gate_up, act_down, norm_store])


def _outffn(xs, y, mod, layer, wo_b, ln1g, ln1b, ln2g, ln2b, wfi_b, wfo_b, *, ctx):
    bsz, seq, _ = xs.shape
    t = min(FFN_TILE, seq)
    row = (lambda b: CTX_ROW) if ctx else (lambda b: b)
    once = pl.Buffered(1)

    def mod_spec(k):
        return pl.BlockSpec((1, 1, D_MODEL), lambda b, i: (row(b), 0, k))

    vec = pl.BlockSpec((1, D_MODEL), lambda b, i: (0, 0))
    return pl.pallas_call(
        functools.partial(_outffn_kernel, sub=min(SUB_TILE, t)),
        grid=(bsz, seq // t),
        in_specs=[
            pl.BlockSpec((1, t, D_MODEL), lambda b, i: (b, i, 0)),
            pl.BlockSpec((1, t, MIX_WIDTH), lambda b, i: (b, i, 0)),
            mod_spec(2), mod_spec(3), mod_spec(4), mod_spec(5),
            pl.BlockSpec((1, MIX_WIDTH, D_MODEL), lambda b, i: (layer, 0, 0), pipeline_mode=once),
            vec, vec, vec, vec,
            pl.BlockSpec((1, D_MODEL, D_FF), lambda b, i: (layer, 0, 0), pipeline_mode=once),
            pl.BlockSpec((1, D_MODEL, D_FF), lambda b, i: (layer, 0, 1), pipeline_mode=once),
            pl.BlockSpec((1, D_FF, D_MODEL), lambda b, i: (layer, 0, 0), pipeline_mode=once),
        ],
        out_specs=pl.BlockSpec((1, t, D_MODEL), lambda b, i: (b, i, 0)),
        out_shape=jax.ShapeDtypeStruct((bsz, seq, D_MODEL), F32),
        compiler_params=pltpu.CompilerParams(vmem_limit_bytes=VMEM_LIMIT),
        name="outffn_ctx" if ctx else "outffn",
    )(xs, y, mod, mod, mod, mod, wo_b, ln1g, ln1b, ln2g, ln2b, wfi_b, wfi_b, wfo_b)


def kernel(x, c, ctx, c_ctx, w_ada, b_ada, w_in, w_o, sink, lam_q1, lam_k1, lam_q2, lam_k2, subln_g, na_bias,
           ln1_g, ln1_b, w_ffn_in, w_ffn_out, ln2_g, ln2_b):
    bsz, seq, _ = x.shape
    assert x.shape == (bsz, seq, D_MODEL) and seq % Q_BLK == 0 and seq // Q_BLK >= 3
    assert bsz < CTX_ROW + 1 <= MOD_ROWS
    cc = jnp.zeros((MOD_ROWS, D_MODEL), F32).at[:bsz].set(c).at[CTX_ROW].set(c_ctx)
    mod_all = _ada(cc, w_ada, b_ada).reshape(DEPTH, MOD_ROWS, 1, N_MOD * D_MODEL)
    tables = _rope_tables(seq)
    na_tab = _na_table(na_bias)
    w_in_b, wo_b, wfi_b, wfo_b = (_cast_bf16(w) for w in (w_in, w_o, w_ffn_in, w_ffn_out))

    xs, cs = x, ctx
    for l in range(DEPTH):
        last = l == DEPTH - 1
        lam_init = 0.8 - 0.6 * math.exp(-0.3 * l)
        mod = mod_all[l]
        lamv = jnp.stack([lam_q1[l], lam_k1[l], lam_q2[l], lam_k2[l]])
        g = subln_g[l].reshape(1, HEAD_DIM)
        lnp = [v[l].reshape(1, D_MODEL) for v in (ln1_g, ln1_b, ln2_g, ln2_b)]

        p = _inproj(xs, mod, w_in_b, l, tables, ctx=False)
        pc = _inproj(cs, mod, w_in_b, l, None, ctx=True)
        y = _attn(p, pc, sink[l], lamv, g, na_tab[l], lam_init)
        xn = _outffn(xs, y, mod, l, wo_b, *lnp, wfi_b, wfo_b, ctx=False)
        if not last:
            yc = _ctx_attn(pc, sink[l], lamv, g, lam_init)
            cs = _outffn(cs, yc, mod, l, wo_b, *lnp, wfi_b, wfo_b, ctx=True)
        xs = xn
    return xs
```

```python
import functools
import math

import jax
import jax.numpy as jnp
import numpy as np
from jax import lax
from jax.experimental import pallas as pl
from jax.experimental.pallas import tpu as pltpu

F32 = jnp.float32
BF16 = jnp.bfloat16

D_MODEL = 1024
DEPTH = 2
GRID_W = 64
HEAD_DIM = 64
A_HEADS = 6
A_KV_HEADS = 2
A_REP = A_HEADS // A_KV_HEADS
WINDOW = 128
WIN_BLK = 128
B_HEADS = 4
B_QK_DIM = 32
C_HEADS = 6
NA_KH = 8
NA_KW = 16
N_MOD = 6
D_FF = 2816
MIX_WIDTH = (A_HEADS + B_HEADS + C_HEADS) * HEAD_DIM
ROPE_BASE = 10000.0
LN_EPS = 1e-5
NEG_INF = -1e30
LOG2E = 1.4426950408889634
QK_SCALE2_A = HEAD_DIM ** -0.5 * LOG2E
QK_SCALE2_B = B_QK_DIM ** -0.5 * LOG2E
ALPHA = (2.0 * DEPTH) ** 0.25

AQ = 0
AK = AQ + A_HEADS * HEAD_DIM
AV = AK + A_KV_HEADS * HEAD_DIM
BQ = AV + A_KV_HEADS * HEAD_DIM
BK = BQ + B_HEADS * 2 * B_QK_DIM
BV = BK + B_HEADS * 2 * B_QK_DIM
CQ = BV + B_HEADS * HEAD_DIM
CK = CQ + C_HEADS * HEAD_DIM
CV = CK + C_HEADS * HEAD_DIM
PROJ_WIDTH = CV + C_HEADS * HEAD_DIM

YA = 0
YB = YA + A_HEADS * HEAD_DIM
YC = YB + B_HEADS * HEAD_DIM

LANES = 128
MOD_ROWS = 16
CTX_ROW = 8
VMEM_LIMIT = 56 * 1024 * 1024

Q_BLK = 256
CTX_BATCH = 2
ATTN_LOOKAHEAD = 2
IN_TILE = 1024
FFN_TILE = 1024
CAST_BLOCK_BYTES =4 * 1024 * 1024
SUB_TILE = 256


def _dot(a, b):
    return jnp.dot(a, b, preferred_element_type=F32)


def _dot_nt(a, b):
    return lax.dot_general(a, b, (((1,), (1,)), ((), ())), preferred_element_type=F32)


def _silu(v):
    return v / (1.0 + jnp.exp(-v))


def _layer_norm(v, g, b):
    mu = jnp.mean(v, axis=-1, keepdims=True)
    d = v - mu
    var = jnp.mean(d * d, axis=-1, keepdims=True)
    return d * lax.rsqrt(var + LN_EPS) * g + b


def _ada_kernel(c_ref, w_ref, b_ref, o_ref):
    s = _silu(c_ref[...])
    o_ref[0] = _dot(s.astype(BF16), w_ref[0].astype(BF16)) + b_ref[0]


def _ada(cc, w_ada, b_ada):
    tn = D_MODEL
    return pl.pallas_call(
        _ada_kernel,
        grid=(DEPTH, N_MOD * D_MODEL // tn),
        in_specs=[
            pl.BlockSpec((MOD_ROWS, D_MODEL), lambda l, j: (0, 0)),
            pl.BlockSpec((1, D_MODEL, tn), lambda l, j: (l, 0, j)),
            pl.BlockSpec((1, 1, tn), lambda l, j: (l, 0, j)),
        ],
        out_specs=pl.BlockSpec((1, MOD_ROWS, tn), lambda l, j: (l, 0, j)),
        out_shape=jax.ShapeDtypeStruct((DEPTH, MOD_ROWS, N_MOD * D_MODEL), F32),
        compiler_params=pltpu.CompilerParams(vmem_limit_bytes=VMEM_LIMIT),
        name="ada",
    )(cc, w_ada, b_ada.reshape(DEPTH, 1, N_MOD * D_MODEL))


def _cast_kernel(w_ref, o_ref):
    o_ref[...] = w_ref[...].astype(BF16)


def _cast_bf16(w):
    depth, rows, cols = w.shape
    blk = rows
    while blk * cols * 4 > CAST_BLOCK_BYTES and blk % 32 == 0:
        blk //= 2
    return pl.pallas_call(
        _cast_kernel,
        grid=(depth, rows // blk),
        in_specs=[pl.BlockSpec((1, blk, cols), lambda l, i: (l, i, 0))],
        out_specs=pl.BlockSpec((1, blk, cols), lambda l, i: (l, i, 0)),
        out_shape=jax.ShapeDtypeStruct(w.shape, BF16),
        name="cast_bf16",
    )(w)


def _rope_group(v, cos, sin, off):
    lane = lax.broadcasted_iota(jnp.int32, v.shape, 1)
    low = (lane % (2 * off)) < off
    partner = jnp.where(low, pltpu.roll(v, LANES - off, 1), pltpu.roll(v, off, 1))
    return v * cos + partner * sin


def _staged(n_units, stages):
    state = [None] * n_units
    for step in range(n_units + len(stages) - 1):
        for k, stage in enumerate(stages):
            u = step - k
            if 0 <= u < n_units:
                state[u] = stage(u, state[u])


def _lane_groups(lo, hi):
    return tuple(range(lo // LANES, hi // LANES))


_GROUP_PLAN = {}
for _g in _lane_groups(AQ, AK):
    _GROUP_PLAN[_g] = (0, HEAD_DIM // 4, QK_SCALE2_A)
for _g in _lane_groups(AK, AV):
    _GROUP_PLAN[_g] = (1, HEAD_DIM // 4, None)
for _g in _lane_groups(BQ, BK):
    _GROUP_PLAN[_g] = (2, B_QK_DIM // 4, QK_SCALE2_B)
for _g in _lane_groups(BK, BV):
    _GROUP_PLAN[_g] = (3, B_QK_DIM // 4, None)
for _g in _lane_groups(CQ, CK):
    _GROUP_PLAN[_g] = (None, None, QK_SCALE2_A)


def _inproj_kernel(x_ref, sh_ref, sc_ref, w_ref, *rest, rope, sub):
    if rope:
        tab_ref, o_ref = rest
    else:
        (o_ref,) = rest

    def rows(u):
        return slice(u * sub, (u + 1) * sub)

    def project(u, _):
        h = x_ref[0, rows(u)] * (1.0 + sc_ref[0]) + sh_ref[0]
        return _dot(h.astype(BF16), w_ref[0])

    def rotate_store(u, p):
        for g in range(PROJ_WIDTH // LANES):
            v = p[:, g * LANES:(g + 1) * LANES]
            pair, quarter, qscale = _GROUP_PLAN.get(g, (None, None, None))
            if rope and pair is not None:
                v = _rope_group(v, tab_ref[2 * pair, rows(u)], tab_ref[2 * pair + 1, rows(u)], quarter)
            elif qscale is not None:
                v = v * qscale
            o_ref[0, rows(u), g * LANES:(g + 1) * LANES] = v.astype(BF16)

    _staged(x_ref.shape[1] // sub, [project, rotate_store])


def _inproj(xs, mod, w_in_b, layer, tables, *, ctx):
    bsz, seq, _ = xs.shape
    t = min(IN_TILE, seq)
    row = (lambda b: CTX_ROW) if ctx else (lambda b: b)
    in_specs = [
        pl.BlockSpec((1, t, D_MODEL), lambda b, i: (b, i, 0)),
        pl.BlockSpec((1, 1, D_MODEL), lambda b, i: (row(b), 0, 0)),
        pl.BlockSpec((1, 1, D_MODEL), lambda b, i: (row(b), 0, 1)),
        pl.BlockSpec((1, D_MODEL, PROJ_WIDTH), lambda b, i: (layer, 0, 0), pipeline_mode=pl.Buffered(1)),
    ]
    args = [xs, mod, mod, w_in_b]
    if tables is not None:
        in_specs.append(pl.BlockSpec((tables.shape[0], t, LANES), lambda b, i: (0, i, 0)))
        args.append(tables)
    return pl.pallas_call(
        functools.partial(_inproj_kernel, rope=tables is not None, sub=min(SUB_TILE, t)),
        grid=(bsz, seq // t),
        in_specs=in_specs,
        out_specs=pl.BlockSpec((1, t, PROJ_WIDTH), lambda b, i: (b, i, 0)),
        out_shape=jax.ShapeDtypeStruct((bsz, seq, PROJ_WIDTH), BF16),
        compiler_params=pltpu.CompilerParams(vmem_limit_bytes=VMEM_LIMIT),
        name="inproj_ctx" if ctx else "inproj",
    )(*args)


def _rope_tables(seq):
    f32 = np.float32
    tpos = np.arange(seq, dtype=np.int32)
    rows = (tpos // GRID_W).astype(f32)[:, None]
    cols = (tpos % GRID_W).astype(f32)[:, None]
    lane = np.arange(LANES, dtype=np.int32)

    def table(head_dim):
        quarter = head_dim // 4
        inv = f32(ROPE_BASE) ** (-np.arange(quarter, dtype=f32) / f32(quarter))
        freq = inv[lane % quarter][None, :].astype(f32)
        use_cols = ((lane % head_dim) >= head_dim // 2)[None, :]
        ang = np.where(use_cols, cols * freq, rows * freq).astype(f32)
        sign = np.where((lane % (2 * quarter)) < quarter, f32(-1.0), f32(1.0))[None, :]
        return np.cos(ang).astype(f32), (np.sin(ang) * sign).astype(f32)

    cos_a, sin_a = table(HEAD_DIM)
    cos_b, sin_b = table(B_QK_DIM)
    return np.stack([cos_a * f32(QK_SCALE2_A), sin_a * f32(QK_SCALE2_A), cos_a, sin_a,
                     cos_b * f32(QK_SCALE2_B), sin_b * f32(QK_SCALE2_B), cos_b, sin_b]).astype(f32)


def _lam_value(lam_ref, lam_init):
    v = lam_ref[...]
    s1 = jnp.sum(v[0:1, :] * v[1:2, :], axis=-1, keepdims=True)
    s2 = jnp.sum(v[2:3, :] * v[3:4, :], axis=-1, keepdims=True)
    return jnp.exp(s1) - jnp.exp(s2) + lam_init


def _row_stack(parts):
    return jnp.concatenate(parts, axis=0)


def _dot_blocks(e, v):
    if not isinstance(v, (list, tuple)):
        return _dot(e, v)
    t = e.shape[0] // len(v)
    return _row_stack([_dot(e[k * t:(k + 1) * t], vk) for k, vk in enumerate(v)])


def _softmax_av(pieces, extra_logit=None):
    terms, m2 = [], None
    for s, bias2, _ in pieces:
        t = s if bias2 is None else s + bias2
        ms = jnp.max(t, axis=-1, keepdims=True)
        terms.append(t)
        m2 = ms if m2 is None else jnp.maximum(m2, ms)
    den = None
    if extra_logit is not None:
        m2 = jnp.maximum(m2, extra_logit * LOG2E)
        den = jnp.exp2(extra_logit * LOG2E - m2)
    out = None
    for (_, _, v), t in zip(pieces, terms):
        e = jnp.exp2(t - m2)
        ls = jnp.sum(e, axis=-1, keepdims=True)
        den = ls if den is None else den + ls
        o = _dot_blocks(e.astype(BF16), v)
        out = o if out is None else out + o
    return out / den


def _lane_group(col):
    g0 = col // LANES * LANES
    return slice(g0, g0 + LANES)


def _placed(q, col):
    t, w = q.shape
    off = col % LANES
    parts = [jnp.zeros((t, off), q.dtype)] if off else []
    parts.append(q)
    if LANES - off - w:
        parts.append(jnp.zeros((t, LANES - off - w), q.dtype))
    return jnp.concatenate(parts, axis=-1)


def _pipelined(units, depth):
    started = []
    for i, (start, _) in enumerate(units):
        started.append(start())
        if i >= depth:
            units[i - depth][1](started[i - depth])
    for i in range(max(len(units) - depth, 0), len(units)):
        units[i][1](started[i])


def _head_cols(base, h):
    return slice(base + h * HEAD_DIM, base + (h + 1) * HEAD_DIM)


def _diff_head_av(scores, v_pieces, lam, g, lam_init, v_off):
    t = scores[0].shape[0] // 2
    mx = None
    for s in scores:
        ms = jnp.max(s, axis=-1, keepdims=True)
        mx = ms if mx is None else jnp.maximum(mx, ms)
    es = [jnp.exp2(s - mx) for s in scores]
    den = None
    for e in es:
        ls = jnp.sum(e, axis=-1, keepdims=True)
        den = ls if den is None else den + ls
    w = lam * den[0:t] / den[t:2 * t]
    o = None
    for e, v in zip(es, v_pieces):
        part = _dot((e[0:t] - w * e[t:2 * t]).astype(BF16), v)
        o = part if o is None else o + part
    o = (o / den[0:t])[:, v_off:v_off + HEAD_DIM]
    ms = jnp.mean(o * o, axis=-1, keepdims=True)
    return o * lax.rsqrt(ms + LN_EPS) * g * (1.0 - lam_init)


def _a_queries(ref, rows):
    return _row_stack([_placed(ref[0, rows, _head_cols(AQ, h)], AK + (h // A_REP) * HEAD_DIM) for h in range(A_HEADS)])


def _a_sink(sink_ref, t):
    return _row_stack([jnp.full((t, 1), sink_ref[h], F32) for h in range(A_HEADS)])


def _a_store(o, o_ref, rows, left):
    t = o.shape[0] // A_HEADS
    for j in range(A_HEADS // 2):
        halves = []
        for h in (2 * j, 2 * j + 1):
            oh = o[h * t:(h + 1) * t]
            if h // A_REP != h % 2:
                oh = pltpu.roll(oh, HEAD_DIM, 1)
            halves.append(oh)
        o_ref[0, rows, YA + j * LANES:YA + (j + 1) * LANES] = jnp.where(left, halves[0], halves[1]).astype(BF16)


def _b_queries(ref, rows, h, lane):
    qcol = BQ + h * 2 * B_QK_DIM
    qg = ref[0, rows, _lane_group(qcol)]
    zero = jnp.zeros_like(qg)
    offs = [qcol % LANES + m * B_QK_DIM for m in range(2)]
    return _row_stack([jnp.where((lane >= off) & (lane < off + B_QK_DIM), qg, zero) for off in offs])


def _c_queries(ref, rows, j, left):
    qg = ref[0, rows, _lane_group(CQ + j * LANES)]
    zero = jnp.zeros_like(qg)
    return _row_stack([jnp.where(left, qg, zero), jnp.where(left, zero, qg)])


def _attn_kernel(sink_ref, p_ref, pc_ref, lam_ref, g_ref, tb_ref, o_ref, *, lam_init, seq):
    n = pl.program_id(1)
    units_a, units_b = [], []
    lane = lax.broadcasted_iota(jnp.int32, (1, LANES), 1)
    left = lane < HEAD_DIM

    nb = seq // WIN_BLK
    for sub in range(Q_BLK // WIN_BLK):
        blk = n * (Q_BLK // WIN_BLK) + sub
        q0 = pl.multiple_of(blk * WIN_BLK, WIN_BLK)
        ws = pl.multiple_of(jnp.clip(blk - 1, 0, nb - 3) * WIN_BLK, WIN_BLK)
        qpos = q0 + lax.broadcasted_iota(jnp.int32, (A_HEADS * WIN_BLK, 3 * WIN_BLK), 0) % WIN_BLK
        kpos = ws + lax.broadcasted_iota(jnp.int32, (A_HEADS * WIN_BLK, 3 * WIN_BLK), 1)
        valid = jnp.abs(qpos - kpos) <= WINDOW
        rows = slice(sub * WIN_BLK, (sub + 1) * WIN_BLK)

        def start(q0=q0, ws=ws, valid=valid):
            q = _a_queries(p_ref, pl.ds(q0, WIN_BLK))
            kw = p_ref[0, pl.ds(ws, 3 * WIN_BLK), _lane_group(AK)]
            kc = pc_ref[0, :, _lane_group(AK)]
            return jnp.where(valid, _dot_nt(q, kw), NEG_INF), _dot_nt(q, kc)

        def finish(scores, ws=ws, rows=rows):
            vw = p_ref[0, pl.ds(ws, 3 * WIN_BLK), _lane_group(AV)]
            vc = pc_ref[0, :, _lane_group(AV)]
            o = _softmax_av([(scores[0], None, vw), (scores[1], None, vc)], extra_logit=_a_sink(sink_ref, WIN_BLK))
            _a_store(o, o_ref, rows, left)

        units_a.append((start, finish))

    lam = _lam_value(lam_ref, lam_init)
    qb = pl.multiple_of(n * Q_BLK, Q_BLK)
    for h in range(B_HEADS):
        def start(h=h):
            q = _b_queries(p_ref, pl.ds(qb, Q_BLK), h, lane)
            kcol = BK + h * 2 * B_QK_DIM
            k_lat = p_ref[0, :, _lane_group(kcol)]
            k_ctx = pc_ref[0, :, _lane_group(kcol)]
            return _dot_nt(q, k_lat), _dot_nt(q, k_ctx)

        def finish(scores, h=h):
            vcol = BV + h * HEAD_DIM
            v_lat = p_ref[0, :, _lane_group(vcol)]
            v_ctx = pc_ref[0, :, _lane_group(vcol)]
            o = _diff_head_av(scores, [v_lat, v_ctx], lam, g_ref[...], lam_init, vcol % LANES)
            o_ref[0, :, _head_cols(YB, h)] = o.astype(BF16)

        units_b.append((start, finish))

    n_rows = seq // GRID_W
    blocks = []
    for i in range(Q_BLK // GRID_W):
        r = n * (Q_BLK // GRID_W) + i
        rs = jnp.clip(r - NA_KH // 2, 0, n_rows - NA_KH)
        dr0 = (NA_KH - 1) - (r - rs)
        qr = pl.multiple_of(r * GRID_W, GRID_W)
        kr = pl.multiple_of(rs * GRID_W, GRID_W)
        blocks += [(i, j, qr, kr, dr0) for j in range(C_HEADS // 2)]

    def start_c():
        s_w, s_c = [], []
        for _, j, qr, kr, _ in blocks:
            q = _c_queries(p_ref, pl.ds(qr, GRID_W), j, left)
            s_w.append(_dot_nt(q, p_ref[0, pl.ds(kr, NA_KH * GRID_W), _lane_group(CK + j * LANES)]))
            s_c.append(_dot_nt(q, pc_ref[0, :, _lane_group(CK + j * LANES)]))
        return _row_stack(s_w), _row_stack(s_c)

    def finish_c(scores):
        bias2 = _row_stack([
            jnp.concatenate([jnp.where(left, tb_ref[h, dr0 + kh], tb_ref[h, dr0 + kh + 1])
                             for kh in range(0, NA_KH, 2)], axis=-1)
            for _, j, _, _, dr0 in blocks for h in (2 * j, 2 * j + 1)])
        vw = [p_ref[0, pl.ds(kr, NA_KH * GRID_W), _lane_group(CV + j * LANES)] for _, j, _, kr, _ in blocks]
        vc = [pc_ref[0, :, _lane_group(CV + j * LANES)] for _, j, _, _, _ in blocks]
        o = _softmax_av([(scores[0], bias2, vw), (scores[1], None, vc)])
        for k, (i, j, _, _, _) in enumerate(blocks):
            ok = o[2 * k * GRID_W:2 * (k + 1) * GRID_W]
            o_ref[0, i * GRID_W:(i + 1) * GRID_W, YC + j * LANES:YC + (j + 1) * LANES] = (
                jnp.where(left, ok[0:GRID_W], ok[GRID_W:2 * GRID_W]).astype(BF16))

    _pipelined([(start_c, finish_c)] + units_a + units_b, ATTN_LOOKAHEAD)


def _attn(p, pc, sink_l, lamv, g, tb, lam_init):
    bsz, seq, _ = p.shape
    n_ctx = pc.shape[1]
    return pl.pallas_call(
        functools.partial(_attn_kernel, lam_init=lam_init, seq=seq),
        grid=(bsz, seq // Q_BLK),
        in_specs=[
            pl.BlockSpec(memory_space=pltpu.SMEM),
            pl.BlockSpec((1, seq, PROJ_WIDTH), lambda b, n: (b, 0, 0)),
            pl.BlockSpec((1, n_ctx, PROJ_WIDTH), lambda b, n: (b, 0, 0)),
            pl.BlockSpec((4, B_QK_DIM), lambda b, n: (0, 0)),
            pl.BlockSpec((1, HEAD_DIM), lambda b, n: (0, 0)),
            pl.BlockSpec(tb.shape, lambda b, n: (0, 0, 0, 0)),
        ],
        out_specs=pl.BlockSpec((1, Q_BLK, MIX_WIDTH), lambda b, n: (b, n, 0)),
        out_shape=jax.ShapeDtypeStruct((bsz, seq, MIX_WIDTH), BF16),
        compiler_params=pltpu.CompilerParams(vmem_limit_bytes=VMEM_LIMIT),
        name="attn",
    )(sink_l, p, pc, lamv, g, tb)


def _ctx_attn_kernel(sink_ref, pc_ref, lam_ref, g_ref, o_ref, *, lam_init):
    t = pc_ref.shape[1]
    every = slice(0, t)
    lane = lax.broadcasted_iota(jnp.int32, (1, LANES), 1)
    left = lane < HEAD_DIM
    lam = _lam_value(lam_ref, lam_init)

    def units_of(pc, out):
        def start_a():
            return _dot_nt(_a_queries(pc, every), pc[0, :, _lane_group(AK)])

        def finish_a(s):
            o = _softmax_av([(s, None, pc[0, :, _lane_group(AV)])], extra_logit=_a_sink(sink_ref, t))
            _a_store(o, out, every, left)

        units = [(start_a, finish_a)]
        for h in range(B_HEADS):
            def start(h=h):
                return (_dot_nt(_b_queries(pc, every, h, lane), pc[0, :, _lane_group(BK + h * HEAD_DIM)]),)

            def finish(scores, h=h):
                vcol = BV + h * HEAD_DIM
                o = _diff_head_av(scores, [pc[0, :, _lane_group(vcol)]], lam, g_ref[...], lam_init, vcol % LANES)
                out[0, :, _head_cols(YB, h)] = o.astype(BF16)

            units.append((start, finish))

        def start_c():
            return _row_stack([_dot_nt(_c_queries(pc, every, j, left), pc[0, :, _lane_group(CK + j * LANES)])
                               for j in range(C_HEADS // 2)])

        def finish_c(s):
            o = _softmax_av([(s, None, [pc[0, :, _lane_group(CV + j * LANES)] for j in range(C_HEADS // 2)])])
            for j in range(C_HEADS // 2):
                oj = o[2 * j * t:2 * (j + 1) * t]
                out[0, :, YC + j * LANES:YC + (j + 1) * LANES] = jnp.where(left, oj[0:t], oj[t:2 * t]).astype(BF16)

        return units + [(start_c, finish_c)]

    units = []
    for bi in range(pc_ref.shape[0]):
        units += units_of(pc_ref.at[pl.ds(bi, 1)], o_ref.at[pl.ds(bi, 1)])
    _pipelined(units, ATTN_LOOKAHEAD)


def _ctx_attn(pc, sink_l, lamv, g, lam_init):
    bsz, n_ctx, _ = pc.shape
    return pl.pallas_call(
        functools.partial(_ctx_attn_kernel, lam_init=lam_init),
        grid=(bsz // CTX_BATCH,),
        in_specs=[
            pl.BlockSpec(memory_space=pltpu.SMEM),
            pl.BlockSpec((CTX_BATCH, n_ctx, PROJ_WIDTH), lambda b: (b, 0, 0)),
            pl.BlockSpec((4, B_QK_DIM), lambda b: (0, 0)),
            pl.BlockSpec((1, HEAD_DIM), lambda b: (0, 0)),
        ],
        out_specs=pl.BlockSpec((CTX_BATCH, n_ctx, MIX_WIDTH), lambda b: (b, 0, 0)),
        out_shape=jax.ShapeDtypeStruct((bsz, n_ctx, MIX_WIDTH), BF16),
        compiler_params=pltpu.CompilerParams(vmem_limit_bytes=VMEM_LIMIT),
        name="ctx_attn",
    )(sink_l, pc, lamv, g)


N_DR = 2 * NA_KH - 1
N_DC = 2 * NA_KW - 1


def _na_table_kernel(nb_ref, o_ref):
    base = (pl.program_id(0) * C_HEADS + pl.program_id(1)) * (N_DR * N_DC)
    wq = lax.broadcasted_iota(jnp.int32, (GRID_W, LANES), 0)
    wk = lax.broadcasted_iota(jnp.int32, (GRID_W, LANES), 1) % GRID_W
    dc = jnp.clip(wk - wq, -(NA_KW - 1), NA_KW - 1) + (NA_KW - 1)
    cs = jnp.clip(wq - NA_KW // 2, 0, GRID_W - NA_KW)
    valid = (wk >= cs) & (wk < cs + NA_KW)
    for dr in range(N_DR):
        acc = jnp.zeros((GRID_W, LANES), F32)
        for c in range(N_DC):
            acc = jnp.where(dc == c, nb_ref[base + dr * N_DC + c], acc)
        o_ref[0, 0, dr] = jnp.where(valid, acc * LOG2E, NEG_INF)


def _na_table(na_bias):
    return pl.pallas_call(
        _na_table_kernel,
        grid=(DEPTH, C_HEADS),
        in_specs=[pl.BlockSpec(memory_space=pltpu.SMEM)],
        out_specs=pl.BlockSpec((1, 1, N_DR, GRID_W, LANES), lambda l, h: (l, h, 0, 0, 0)),
        out_shape=jax.ShapeDtypeStruct((DEPTH, C_HEADS, N_DR, GRID_W, LANES), F32),
        name="na_table",
    )(na_bias.reshape(-1))


def _outffn_kernel(x_ref, y_ref, g1_ref, sh2_ref, sc2_ref, g2_ref, wo_ref, ln1g_ref, ln1b_ref, ln2g_ref, ln2b_ref,
                   wg_ref, wu_ref, wd_ref, o_ref, *, sub):
    def rows(u):
        return slice(u * sub, (u + 1) * sub)

    def out_proj(u, _):
        return _dot(y_ref[0, rows(u)], wo_ref[0])

    def norm_gate_up(u, y):
        xn = _layer_norm(ALPHA * x_ref[0, rows(u)] + g1_ref[0] * y, ln1g_ref[...], ln1b_ref[...])
        h = (xn * (1.0 + sc2_ref[0]) + sh2_ref[0]).astype(BF16)
        return xn, _dot(h, wg_ref[0]), _dot(h, wu_ref[0])

    def act_down(u, st):
        xn, gate, up = st
        return xn, _dot((_silu(gate) * up).astype(BF16), wd_ref[0])

    def norm_store(u, st):
        xn, ff = st
        o_ref[0, rows(u)] = _layer_norm(ALPHA * xn + g2_ref[0] * ff, ln2g_ref[...], ln2b_ref[...])

    _staged(x_ref.shape[1] // sub, [out_proj, norm_gate_up, act_down, norm_store])


def _outffn(xs, y, mod, layer, wo_b, ln1g, ln1b, ln2g, ln2b, wfi_b, wfo_b, *, ctx):
    bsz, seq, _ = xs.shape
    t = min(FFN_TILE, seq)
    row = (lambda b: CTX_ROW) if ctx else (lambda b: b)
    once = pl.Buffered(1)

    def mod_spec(k):
        return pl.BlockSpec((1, 1, D_MODEL), lambda b, i: (row(b), 0, k))

    vec = pl.BlockSpec((1, D_MODEL), lambda b, i: (0, 0))
    return pl.pallas_call(
        functools.partial(_outffn_kernel, sub=min(SUB_TILE, t // 2)),
        grid=(bsz, seq // t),
        in_specs=[
            pl.BlockSpec((1, t, D_MODEL), lambda b, i: (b, i, 0)),
            pl.BlockSpec((1, t, MIX_WIDTH), lambda b, i: (b, i, 0)),
            mod_spec(2), mod_spec(3), mod_spec(4), mod_spec(5),
            pl.BlockSpec((1, MIX_WIDTH, D_MODEL), lambda b, i: (layer, 0, 0), pipeline_mode=once),
            vec, vec, vec, vec,
            pl.BlockSpec((1, D_MODEL, D_FF), lambda b, i: (layer, 0, 0), pipeline_mode=once),
            pl.BlockSpec((1, D_MODEL, D_FF), lambda b, i: (layer, 0, 1), pipeline_mode=once),
            pl.BlockSpec((1, D_FF, D_MODEL), lambda b, i: (layer, 0, 0), pipeline_mode=once),
        ],
        out_specs=pl.BlockSpec((1, t, D_MODEL), lambda b, i: (b, i, 0)),
        out_shape=jax.ShapeDtypeStruct((bsz, seq, D_MODEL), F32),
        compiler_params=pltpu.CompilerParams(vmem_limit_bytes=VMEM_LIMIT),
        name="outffn_ctx" if ctx else "outffn",
    )(xs, y, mod, mod, mod, mod, wo_b, ln1g, ln1b, ln2g, ln2b, wfi_b, wfi_b, wfo_b)


def kernel(x, c, ctx, c_ctx, w_ada, b_ada, w_in, w_o, sink, lam_q1, lam_k1, lam_q2, lam_k2, subln_g, na_bias,
           ln1_g, ln1_b, w_ffn_in, w_ffn_out, ln2_g, ln2_b):
    bsz, seq, _ = x.shape
    assert x.shape == (bsz, seq, D_MODEL) and seq % Q_BLK == 0 and seq // Q_BLK >= 3
    assert bsz < CTX_ROW + 1 <= MOD_ROWS and bsz % CTX_BATCH == 0
    cc = jnp.zeros((MOD_ROWS, D_MODEL), F32).at[:bsz].set(c).at[CTX_ROW].set(c_ctx)
    mod_all = _ada(cc, w_ada, b_ada).reshape(DEPTH, MOD_ROWS, 1, N_MOD * D_MODEL)
    tables = _rope_tables(seq)
    na_tab = _na_table(na_bias)
    w_in_b, wo_b, wfi_b, wfo_b = (_cast_bf16(w) for w in (w_in, w_o, w_ffn_in, w_ffn_out))

    xs, cs = x, ctx
    for l in range(DEPTH):
        last = l == DEPTH - 1
        lam_init = 0.8 - 0.6 * math.exp(-0.3 * l)
        mod = mod_all[l]
        lamv = jnp.stack([lam_q1[l], lam_k1[l], lam_q2[l], lam_k2[l]])
        g = subln_g[l].reshape(1, HEAD_DIM)
        lnp = [v[l].reshape(1, D_MODEL) for v in (ln1_g, ln1_b, ln2_g, ln2_b)]

        p = _inproj(xs, mod, w_in_b, l, tables, ctx=False)
        pc = _inproj(cs, mod, w_in_b, l, None, ctx=True)
        y = _attn(p, pc, sink[l], lamv, g, na_tab[l], lam_init)
        xn = _outffn(xs, y, mod, l, wo_b, *lnp, wfi_b, wfo_b, ctx=False)
        if not last:
            yc = _ctx_attn(pc, sink[l], lamv, g, lam_init)
            cs = _outffn(cs, yc, mod, l, wo_b, *lnp, wfi_b, wfo_b, ctx=True)
        xs = xn
    return xs
```

```python
import functools
import math

import jax
import jax.numpy as jnp
import numpy as np
from jax import lax
from jax.experimental import pallas as pl
from jax.experimental.pallas import tpu as pltpu

F32 = jnp.float32
BF16 = jnp.bfloat16

D_MODEL = 1024
DEPTH = 2
GRID_W = 64
HEAD_DIM = 64
A_HEADS = 6
A_KV_HEADS = 2
A_REP = A_HEADS // A_KV_HEADS
WINDOW = 128
WIN_BLK = 128
B_HEADS = 4
B_QK_DIM = 32
C_HEADS = 6
NA_KH = 8
NA_KW = 16
N_MOD = 6
D_FF = 2816
MIX_WIDTH = (A_HEADS + B_HEADS + C_HEADS) * HEAD_DIM
ROPE_BASE = 10000.0
LN_EPS = 1e-5
NEG_INF = -1e30
LOG2E = 1.4426950408889634
QK_SCALE2_A = HEAD_DIM ** -0.5 * LOG2E
QK_SCALE2_B = B_QK_DIM ** -0.5 * LOG2E
ALPHA = (2.0 * DEPTH) ** 0.25

AQ = 0
AK = AQ + A_HEADS * HEAD_DIM
AV = AK + A_KV_HEADS * HEAD_DIM
BQ = AV + A_KV_HEADS * HEAD_DIM
BK = BQ + B_HEADS * 2 * B_QK_DIM
BV = BK + B_HEADS * 2 * B_QK_DIM
CQ = BV + B_HEADS * HEAD_DIM
CK = CQ + C_HEADS * HEAD_DIM
CV = CK + C_HEADS * HEAD_DIM
PROJ_WIDTH = CV + C_HEADS * HEAD_DIM

YA = 0
YB = YA + A_HEADS * HEAD_DIM
YC = YB + B_HEADS * HEAD_DIM

LANES = 128
MOD_ROWS = 16
CTX_ROW = 8
VMEM_LIMIT = 56 * 1024 * 1024

Q_BLK = 256
CTX_BATCH = 2
ATTN_LOOKAHEAD = 2
IN_TILE = 1024
FFN_TILE = 1024
SUB_TILE = 256
CAST_BLOCK_BYTES = 4 * 1024 * 1024


def _dot(a, b):
    return jnp.dot(a, b, preferred_element_type=F32)


def _dot_nt(a, b):
    return lax.dot_general(a, b, (((1,), (1,)), ((), ())), preferred_element_type=F32)


def _silu(v):
    return v / (1.0 + jnp.exp(-v))


def _layer_norm(v, g, b):
    mu = jnp.mean(v, axis=-1, keepdims=True)
    d = v - mu
    var = jnp.mean(d * d, axis=-1, keepdims=True)
    return d * lax.rsqrt(var + LN_EPS) * g + b


def _ada_kernel(c_ref, w_ref, b_ref, o_ref):
    s = _silu(c_ref[...])
    o_ref[0] = _dot(s.astype(BF16), w_ref[0].astype(BF16)) + b_ref[0]


def _ada(cc, w_ada, b_ada):
    tn = D_MODEL
    return pl.pallas_call(
        _ada_kernel,
        grid=(DEPTH, N_MOD * D_MODEL // tn),
        in_specs=[
            pl.BlockSpec((MOD_ROWS, D_MODEL), lambda l, j: (0, 0)),
            pl.BlockSpec((1, D_MODEL, tn), lambda l, j: (l, 0, j)),
            pl.BlockSpec((1, 1, tn), lambda l, j: (l, 0, j)),
        ],
        out_specs=pl.BlockSpec((1, MOD_ROWS, tn), lambda l, j: (l, 0, j)),
        out_shape=jax.ShapeDtypeStruct((DEPTH, MOD_ROWS, N_MOD * D_MODEL), F32),
        compiler_params=pltpu.CompilerParams(vmem_limit_bytes=VMEM_LIMIT),
        name="ada",
    )(cc, w_ada, b_ada.reshape(DEPTH, 1, N_MOD * D_MODEL))


def _cast_kernel(w_ref, o_ref):
    o_ref[...] = w_ref[...].astype(BF16)


def _cast_bf16(w):
    depth, rows, cols = w.shape
    blk = rows
    while blk * cols * 4 > CAST_BLOCK_BYTES and blk % 32 == 0:
        blk //= 2
    return pl.pallas_call(
        _cast_kernel,
        grid=(depth, rows // blk),
        in_specs=[pl.BlockSpec((1, blk, cols), lambda l, i: (l, i, 0))],
        out_specs=pl.BlockSpec((1, blk, cols), lambda l, i: (l, i, 0)),
        out_shape=jax.ShapeDtypeStruct(w.shape, BF16),
        name="cast_bf16",
    )(w)


def _rope_group(v, cos, sin, off):
    lane = lax.broadcasted_iota(jnp.int32, v.shape, 1)
    low = (lane % (2 * off)) < off
    partner = jnp.where(low, pltpu.roll(v, LANES - off, 1), pltpu.roll(v, off, 1))
    return v * cos + partner * sin


def _staged(n_units, stages):
    state = [None] * n_units
    for step in range(n_units + len(stages) - 1):
        for k, stage in enumerate(stages):
            u = step - k
            if 0 <= u < n_units:
                state[u] = stage(u, state[u])


def _lane_groups(lo, hi):
    return tuple(range(lo // LANES, hi // LANES))


_GROUP_PLAN = {}
for _g in _lane_groups(AQ, AK):
    _GROUP_PLAN[_g] = (0, HEAD_DIM // 4, QK_SCALE2_A)
for _g in _lane_groups(AK, AV):
    _GROUP_PLAN[_g] = (1, HEAD_DIM // 4, None)
for _g in _lane_groups(BQ, BK):
    _GROUP_PLAN[_g] = (2, B_QK_DIM // 4, QK_SCALE2_B)
for _g in _lane_groups(BK, BV):
    _GROUP_PLAN[_g] = (3, B_QK_DIM // 4, None)
for _g in _lane_groups(CQ, CK):
    _GROUP_PLAN[_g] = (None, None, QK_SCALE2_A)


def _inproj_kernel(x_ref, sh_ref, sc_ref, w_ref, *rest, rope, sub):
    if rope:
        tab_ref, o_ref = rest
    else:
        (o_ref,) = rest

    def rows(u):
        return slice(u * sub, (u + 1) * sub)

    def project(u, _):
        h = x_ref[0, rows(u)] * (1.0 + sc_ref[0]) + sh_ref[0]
        return _dot(h.astype(BF16), w_ref[0])

    def rotate_store(u, p):
        for g in range(PROJ_WIDTH // LANES):
            v = p[:, g * LANES:(g + 1) * LANES]
            pair, quarter, qscale = _GROUP_PLAN.get(g, (None, None, None))
            if rope and pair is not None:
                v = _rope_group(v, tab_ref[2 * pair, rows(u)], tab_ref[2 * pair + 1, rows(u)], quarter)
            elif qscale is not None:
                v = v * qscale
            o_ref[0, rows(u), g * LANES:(g + 1) * LANES] = v.astype(BF16)

    _staged(x_ref.shape[1] // sub, [project, rotate_store])


def _inproj(xs, mod, w_in_b, layer, tables, *, ctx):
    bsz, seq, _ = xs.shape
    t = min(IN_TILE, seq)
    row = (lambda b: CTX_ROW) if ctx else (lambda b: b)
    in_specs = [
        pl.BlockSpec((1, t, D_MODEL), lambda b, i: (b, i, 0)),
        pl.BlockSpec((1, 1, D_MODEL), lambda b, i: (row(b), 0, 0)),
        pl.BlockSpec((1, 1, D_MODEL), lambda b, i: (row(b), 0, 1)),
        pl.BlockSpec((1, D_MODEL, PROJ_WIDTH), lambda b, i: (layer, 0, 0), pipeline_mode=pl.Buffered(1)),
    ]
    args = [xs, mod, mod, w_in_b]
    if tables is not None:
        in_specs.append(pl.BlockSpec((tables.shape[0], t, LANES), lambda b, i: (0, i, 0)))
        args.append(tables)
    return pl.pallas_call(
        functools.partial(_inproj_kernel, rope=tables is not None, sub=min(SUB_TILE, t)),
        grid=(bsz, seq // t),
        in_specs=in_specs,
        out_specs=pl.BlockSpec((1, t, PROJ_WIDTH), lambda b, i: (b, i, 0)),
        out_shape=jax.ShapeDtypeStruct((bsz, seq, PROJ_WIDTH), BF16),
        compiler_params=pltpu.CompilerParams(vmem_limit_bytes=VMEM_LIMIT),
        name="inproj_ctx" if ctx else "inproj",
    )(*args)


def _rope_tables(seq):
    f32 = np.float32
    tpos = np.arange(seq, dtype=np.int32)
    rows = (tpos // GRID_W).astype(f32)[:, None]
    cols = (tpos % GRID_W).astype(f32)[:, None]
    lane = np.arange(LANES, dtype=np.int32)

    def table(head_dim):
        quarter = head_dim // 4
        inv = f32(ROPE_BASE) ** (-np.arange(quarter, dtype=f32) / f32(quarter))
        freq = inv[lane % quarter][None, :].astype(f32)
        use_cols = ((lane % head_dim) >= head_dim // 2)[None, :]
        ang = np.where(use_cols, cols * freq, rows * freq).astype(f32)
        sign = np.where((lane % (2 * quarter)) < quarter, f32(-1.0), f32(1.0))[None, :]
        return np.cos(ang).astype(f32), (np.sin(ang) * sign).astype(f32)

    cos_a, sin_a = table(HEAD_DIM)
    cos_b, sin_b = table(B_QK_DIM)
    return np.stack([cos_a * f32(QK_SCALE2_A), sin_a * f32(QK_SCALE2_A), cos_a, sin_a,
                     cos_b * f32(QK_SCALE2_B), sin_b * f32(QK_SCALE2_B), cos_b, sin_b]).astype(f32)


def _lam_value(lam_ref, lam_init):
    v = lam_ref[...]
    s1 = jnp.sum(v[0:1, :] * v[1:2, :], axis=-1, keepdims=True)
    s2 = jnp.sum(v[2:3, :] * v[3:4, :], axis=-1, keepdims=True)
    return jnp.exp(s1) - jnp.exp(s2) + lam_init


def _row_stack(parts):
    return jnp.concatenate(parts, axis=0)


def _dot_blocks(e, v):
    if not isinstance(v, (list, tuple)):
        return _dot(e, v)
    t = e.shape[0] // len(v)
    return _row_stack([_dot(e[k * t:(k + 1) * t], vk) for k, vk in enumerate(v)])


def _softmax_av(pieces, extra_logit=None):
    weights, den = _softmax_weights([(s, bias2) for s, bias2, _ in pieces], extra_logit)
    return _weighted_values(weights, [v for _, _, v in pieces], den)


def _softmax_weights(pieces, extra_logit=None):
    terms, m2 = [], None
    for s, bias2 in pieces:
        t = s if bias2 is None else s + bias2
        ms = jnp.max(t, axis=-1, keepdims=True)
        terms.append(t)
        m2 = ms if m2 is None else jnp.maximum(m2, ms)
    den = None
    if extra_logit is not None:
        m2 = jnp.maximum(m2, extra_logit * LOG2E)
        den = jnp.exp2(extra_logit * LOG2E - m2)
    weights = []
    for t in terms:
        e = jnp.exp2(t - m2)
        ls = jnp.sum(e, axis=-1, keepdims=True)
        den = ls if den is None else den + ls
        weights.append(e.astype(BF16))
    return weights, den


def _weighted_values(weights, values, den):
    out = None
    for e, v in zip(weights, values):
        o = _dot_blocks(e, v)
        out = o if out is None else out + o
    return out / den


def _lane_group(col):
    g0 = col // LANES * LANES
    return slice(g0, g0 + LANES)


def _placed(q, col):
    t, w = q.shape
    off = col % LANES
    parts = [jnp.zeros((t, off), q.dtype)] if off else []
    parts.append(q)
    if LANES - off - w:
        parts.append(jnp.zeros((t, LANES - off - w), q.dtype))
    return jnp.concatenate(parts, axis=-1)


def _pipelined(units, offsets):
    state = [None] * len(units)
    for step in range(len(units) + max(offsets)):
        for k, off in enumerate(offsets):
            u = step - off
            if 0 <= u < len(units):
                state[u] = units[u][k](state[u])


def _head_cols(base, h):
    return slice(base + h * HEAD_DIM, base + (h + 1) * HEAD_DIM)


def _diff_head_weights(scores, lam):
    t = scores[0].shape[0] // 2
    mx = None
    for s in scores:
        ms = jnp.max(s, axis=-1, keepdims=True)
        mx = ms if mx is None else jnp.maximum(mx, ms)
    es = [jnp.exp2(s - mx) for s in scores]
    den = None
    for e in es:
        ls = jnp.sum(e, axis=-1, keepdims=True)
        den = ls if den is None else den + ls
    w = lam * den[0:t] / den[t:2 * t]
    return [(e[0:t] - w * e[t:2 * t]).astype(BF16) for e in es], den[0:t]


def _diff_head_out(weights, den, v_pieces, g, lam_init, v_off):
    o = _weighted_values(weights, v_pieces, den)[:, v_off:v_off + HEAD_DIM]
    ms = jnp.mean(o * o, axis=-1, keepdims=True)
    return o * lax.rsqrt(ms + LN_EPS) * g * (1.0 - lam_init)


def _a_queries(ref, rows):
    return _row_stack([_placed(ref[0, rows, _head_cols(AQ, h)], AK + (h // A_REP) * HEAD_DIM) for h in range(A_HEADS)])


def _a_sink(sink_ref, t):
    return _row_stack([jnp.full((t, 1), sink_ref[h], F32) for h in range(A_HEADS)])


def _a_store(o, o_ref, rows, left):
    t = o.shape[0] // A_HEADS
    for j in range(A_HEADS // 2):
        halves = []
        for h in (2 * j, 2 * j + 1):
            oh = o[h * t:(h + 1) * t]
            if h // A_REP != h % 2:
                oh = pltpu.roll(oh, HEAD_DIM, 1)
            halves.append(oh)
        o_ref[0, rows, YA + j * LANES:YA + (j + 1) * LANES] = jnp.where(left, halves[0], halves[1]).astype(BF16)


def _b_queries(ref, rows, h, lane):
    qcol = BQ + h * 2 * B_QK_DIM
    qg = ref[0, rows, _lane_group(qcol)]
    zero = jnp.zeros_like(qg)
    offs = [qcol % LANES + m * B_QK_DIM for m in range(2)]
    return _row_stack([jnp.where((lane >= off) & (lane < off + B_QK_DIM), qg, zero) for off in offs])


def _c_queries(ref, rows, j, left):
    qg = ref[0, rows, _lane_group(CQ + j * LANES)]
    zero = jnp.zeros_like(qg)
    return _row_stack([jnp.where(left, qg, zero), jnp.where(left, zero, qg)])


def _attn_kernel(sink_ref, p_ref, pc_ref, lam_ref, g_ref, tb_ref, o_ref, *, lam_init, seq):
    n = pl.program_id(1)
    units_a, units_b = [], []
    lane = lax.broadcasted_iota(jnp.int32, (1, LANES), 1)
    left = lane < HEAD_DIM

    nb = seq // WIN_BLK
    for sub in range(Q_BLK // WIN_BLK):
        blk = n * (Q_BLK // WIN_BLK) + sub
        q0 = pl.multiple_of(blk * WIN_BLK, WIN_BLK)
        ws = pl.multiple_of(jnp.clip(blk - 1, 0, nb - 3) * WIN_BLK, WIN_BLK)
        qpos = q0 + lax.broadcasted_iota(jnp.int32, (A_HEADS * WIN_BLK, 3 * WIN_BLK), 0) % WIN_BLK
        kpos = ws + lax.broadcasted_iota(jnp.int32, (A_HEADS * WIN_BLK, 3 * WIN_BLK), 1)
        valid = jnp.abs(qpos - kpos) <= WINDOW
        rows = slice(sub * WIN_BLK, (sub + 1) * WIN_BLK)

        def start(_, q0=q0, ws=ws, valid=valid):
            q = _a_queries(p_ref, pl.ds(q0, WIN_BLK))
            kw = p_ref[0, pl.ds(ws, 3 * WIN_BLK), _lane_group(AK)]
            kc = pc_ref[0, :, _lane_group(AK)]
            return jnp.where(valid, _dot_nt(q, kw), NEG_INF), _dot_nt(q, kc)

        def soft(scores):
            return _softmax_weights([(scores[0], None), (scores[1], None)], extra_logit=_a_sink(sink_ref, WIN_BLK))

        def finish(weights_den, ws=ws, rows=rows):
            vw = p_ref[0, pl.ds(ws, 3 * WIN_BLK), _lane_group(AV)]
            vc = pc_ref[0, :, _lane_group(AV)]
            _a_store(_weighted_values(weights_den[0], [vw, vc], weights_den[1]), o_ref, rows, left)

        units_a.append((start, soft, finish))

    lam = _lam_value(lam_ref, lam_init)
    qb = pl.multiple_of(n * Q_BLK, Q_BLK)
    for h in range(B_HEADS):
        def start(_, h=h):
            q = _b_queries(p_ref, pl.ds(qb, Q_BLK), h, lane)
            kcol = BK + h * 2 * B_QK_DIM
            k_lat = p_ref[0, :, _lane_group(kcol)]
            k_ctx = pc_ref[0, :, _lane_group(kcol)]
            return _dot_nt(q, k_lat), _dot_nt(q, k_ctx)

        def soft(scores):
            return _diff_head_weights(scores, lam)

        def finish(weights_den, h=h):
            vcol = BV + h * HEAD_DIM
            v_lat = p_ref[0, :, _lane_group(vcol)]
            v_ctx = pc_ref[0, :, _lane_group(vcol)]
            o = _diff_head_out(*weights_den, [v_lat, v_ctx], g_ref[...], lam_init, vcol % LANES)
            o_ref[0, :, _head_cols(YB, h)] = o.astype(BF16)

        units_b.append((start, soft, finish))

    n_rows = seq // GRID_W
    blocks = []
    for i in range(Q_BLK // GRID_W):
        r = n * (Q_BLK // GRID_W) + i
        rs = jnp.clip(r - NA_KH // 2, 0, n_rows - NA_KH)
        dr0 = (NA_KH - 1) - (r - rs)
        qr = pl.multiple_of(r * GRID_W, GRID_W)
        kr = pl.multiple_of(rs * GRID_W, GRID_W)
        blocks += [(i, j, qr, kr, dr0) for j in range(C_HEADS // 2)]

    def start_c(_):
        s_w, s_c = [], []
        for _, j, qr, kr, _ in blocks:
            q = _c_queries(p_ref, pl.ds(qr, GRID_W), j, left)
            s_w.append(_dot_nt(q, p_ref[0, pl.ds(kr, NA_KH * GRID_W), _lane_group(CK + j * LANES)]))
            s_c.append(_dot_nt(q, pc_ref[0, :, _lane_group(CK + j * LANES)]))
        return _row_stack(s_w), _row_stack(s_c)

    def soft_c(scores):
        bias2 = _row_stack([
            jnp.concatenate([jnp.where(left, tb_ref[h, dr0 + kh], tb_ref[h, dr0 + kh + 1])
                             for kh in range(0, NA_KH, 2)], axis=-1)
            for _, j, _, _, dr0 in blocks for h in (2 * j, 2 * j + 1)])
        return _softmax_weights([(scores[0], bias2), (scores[1], None)])

    def finish_c(weights_den):
        vw = [p_ref[0, pl.ds(kr, NA_KH * GRID_W), _lane_group(CV + j * LANES)] for _, j, _, kr, _ in blocks]
        vc = [pc_ref[0, :, _lane_group(CV + j * LANES)] for _, j, _, _, _ in blocks]
        o = _weighted_values(weights_den[0], [vw, vc], weights_den[1])
        for k, (i, j, _, _, _) in enumerate(blocks):
            ok = o[2 * k * GRID_W:2 * (k + 1) * GRID_W]
            o_ref[0, i * GRID_W:(i + 1) * GRID_W, YC + j * LANES:YC + (j + 1) * LANES] = (
                jnp.where(left, ok[0:GRID_W], ok[GRID_W:2 * GRID_W]).astype(BF16))

    _pipelined([(start_c, soft_c, finish_c)] + units_a + units_b, (0, ATTN_LOOKAHEAD, ATTN_LOOKAHEAD))


def _attn(p, pc, sink_l, lamv, g, tb, lam_init):
    bsz, seq, _ = p.shape
    n_ctx = pc.shape[1]
    return pl.pallas_call(
        functools.partial(_attn_kernel, lam_init=lam_init, seq=seq),
        grid=(bsz, seq // Q_BLK),
        in_specs=[
            pl.BlockSpec(memory_space=pltpu.SMEM),
            pl.BlockSpec((1, seq, PROJ_WIDTH), lambda b, n: (b, 0, 0)),
            pl.BlockSpec((1, n_ctx, PROJ_WIDTH), lambda b, n: (b, 0, 0)),
            pl.BlockSpec((4, B_QK_DIM), lambda b, n: (0, 0)),
            pl.BlockSpec((1, HEAD_DIM), lambda b, n: (0, 0)),
            pl.BlockSpec(tb.shape, lambda b, n: (0, 0, 0, 0)),
        ],
        out_specs=pl.BlockSpec((1, Q_BLK, MIX_WIDTH), lambda b, n: (b, n, 0)),
        out_shape=jax.ShapeDtypeStruct((bsz, seq, MIX_WIDTH), BF16),
        compiler_params=pltpu.CompilerParams(vmem_limit_bytes=VMEM_LIMIT),
        name="attn",
    )(sink_l, p, pc, lamv, g, tb)


def _ctx_attn_kernel(sink_ref, pc_ref, lam_ref, g_ref, o_ref, *, lam_init):
    t = pc_ref.shape[1]
    every = slice(0, t)
    lane = lax.broadcasted_iota(jnp.int32, (1, LANES), 1)
    left = lane < HEAD_DIM
    lam = _lam_value(lam_ref, lam_init)

    def units_of(pc, out):
        def start_a(_):
            return _dot_nt(_a_queries(pc, every), pc[0, :, _lane_group(AK)])

        def finish_a(s):
            o = _softmax_av([(s, None, pc[0, :, _lane_group(AV)])], extra_logit=_a_sink(sink_ref, t))
            _a_store(o, out, every, left)

        units = [(start_a, finish_a)]
        for h in range(B_HEADS):
            def start(_, h=h):
                return (_dot_nt(_b_queries(pc, every, h, lane), pc[0, :, _lane_group(BK + h * HEAD_DIM)]),)

            def finish(scores, h=h):
                vcol = BV + h * HEAD_DIM
                o = _diff_head_out(*_diff_head_weights(scores, lam), [pc[0, :, _lane_group(vcol)]], g_ref[...],
                                   lam_init, vcol % LANES)
                out[0, :, _head_cols(YB, h)] = o.astype(BF16)

            units.append((start, finish))

        def start_c(_):
            return _row_stack([_dot_nt(_c_queries(pc, every, j, left), pc[0, :, _lane_group(CK + j * LANES)])
                               for j in range(C_HEADS // 2)])

        def finish_c(s):
            o = _softmax_av([(s, None, [pc[0, :, _lane_group(CV + j * LANES)] for j in range(C_HEADS // 2)])])
            for j in range(C_HEADS // 2):
                oj = o[2 * j * t:2 * (j + 1) * t]
                out[0, :, YC + j * LANES:YC + (j + 1) * LANES] = jnp.where(left, oj[0:t], oj[t:2 * t]).astype(BF16)

        return units + [(start_c, finish_c)]

    units = []
    for bi in range(pc_ref.shape[0]):
        units += units_of(pc_ref.at[pl.ds(bi, 1)], o_ref.at[pl.ds(bi, 1)])
    _pipelined(units, (0, ATTN_LOOKAHEAD))


def _ctx_attn(pc, sink_l, lamv, g, lam_init):
    bsz, n_ctx, _ = pc.shape
    return pl.pallas_call(
        functools.partial(_ctx_attn_kernel, lam_init=lam_init),
        grid=(bsz // CTX_BATCH,),
        in_specs=[
            pl.BlockSpec(memory_space=pltpu.SMEM),
            pl.BlockSpec((CTX_BATCH, n_ctx, PROJ_WIDTH), lambda b: (b, 0, 0)),
            pl.BlockSpec((4, B_QK_DIM), lambda b: (0, 0)),
            pl.BlockSpec((1, HEAD_DIM), lambda b: (0, 0)),
        ],
        out_specs=pl.BlockSpec((CTX_BATCH, n_ctx, MIX_WIDTH), lambda b: (b, 0, 0)),
        out_shape=jax.ShapeDtypeStruct((bsz, n_ctx, MIX_WIDTH), BF16),
        compiler_params=pltpu.CompilerParams(vmem_limit_bytes=VMEM_LIMIT),
        name="ctx_attn",
    )(sink_l, pc, lamv, g)


N_DR = 2 * NA_KH - 1
N_DC = 2 * NA_KW - 1


def _na_table_kernel(nb_ref, o_ref):
    base = (pl.program_id(0) * C_HEADS + pl.program_id(1)) * (N_DR * N_DC)
    wq = lax.broadcasted_iota(jnp.int32, (GRID_W, LANES), 0)
    wk = lax.broadcasted_iota(jnp.int32, (GRID_W, LANES), 1) % GRID_W
    dc = jnp.clip(wk - wq, -(NA_KW - 1), NA_KW - 1) + (NA_KW - 1)
    cs = jnp.clip(wq - NA_KW // 2, 0, GRID_W - NA_KW)
    valid = (wk >= cs) & (wk < cs + NA_KW)
    for dr in range(N_DR):
        acc = jnp.zeros((GRID_W, LANES), F32)
        for c in range(N_DC):
            acc = jnp.where(dc == c, nb_ref[base + dr * N_DC + c], acc)
        o_ref[0, 0, dr] = jnp.where(valid, acc * LOG2E, NEG_INF)


def _na_table(na_bias):
    return pl.pallas_call(
        _na_table_kernel,
        grid=(DEPTH, C_HEADS),
        in_specs=[pl.BlockSpec(memory_space=pltpu.SMEM)],
        out_specs=pl.BlockSpec((1, 1, N_DR, GRID_W, LANES), lambda l, h: (l, h, 0, 0, 0)),
        out_shape=jax.ShapeDtypeStruct((DEPTH, C_HEADS, N_DR, GRID_W, LANES), F32),
        name="na_table",
    )(na_bias.reshape(-1))


def _outffn_kernel(x_ref, y_ref, g1_ref, sh2_ref, sc2_ref, g2_ref, wo_ref, ln1g_ref, ln1b_ref, ln2g_ref, ln2b_ref,
                   wg_ref, wu_ref, wd_ref, o_ref, *, sub):
    def rows(u):
        return slice(u * sub, (u + 1) * sub)

    def out_proj(u, _):
        return _dot(y_ref[0, rows(u)], wo_ref[0])

    def norm_gate_up(u, y):
        xn = _layer_norm(ALPHA * x_ref[0, rows(u)] + g1_ref[0] * y, ln1g_ref[...], ln1b_ref[...])
        h = (xn * (1.0 + sc2_ref[0]) + sh2_ref[0]).astype(BF16)
        return xn, _dot(h, wg_ref[0]), _dot(h, wu_ref[0])

    def act_down(u, st):
        xn, gate, up = st
        return xn, _dot((_silu(gate) * up).astype(BF16), wd_ref[0])

    def norm_store(u, st):
        xn, ff = st
        o_ref[0, rows(u)] = _layer_norm(ALPHA * xn + g2_ref[0] * ff, ln2g_ref[...], ln2b_ref[...])

    _staged(x_ref.shape[1] // sub, [out_proj, norm_gate_up, act_down, norm_store])


def _outffn(xs, y, mod, layer, wo_b, ln1g, ln1b, ln2g, ln2b, wfi_b, wfo_b, *, ctx):
    bsz, seq, _ = xs.shape
    t = min(FFN_TILE, seq)
    row = (lambda b: CTX_ROW) if ctx else (lambda b: b)
    once = pl.Buffered(1)

    def mod_spec(k):
        return pl.BlockSpec((1, 1, D_MODEL), lambda b, i: (row(b), 0, k))

    vec = pl.BlockSpec((1, D_MODEL), lambda b, i: (0, 0))
    return pl.pallas_call(
        functools.partial(_outffn_kernel, sub=min(SUB_TILE, t // 2)),
        grid=(bsz, seq // t),
        in_specs=[
            pl.BlockSpec((1, t, D_MODEL), lambda b, i: (b, i, 0)),
            pl.BlockSpec((1, t, MIX_WIDTH), lambda b, i: (b, i, 0)),
            mod_spec(2), mod_spec(3), mod_spec(4), mod_spec(5),
            pl.BlockSpec((1, MIX_WIDTH, D_MODEL), lambda b, i: (layer, 0, 0), pipeline_mode=once),
            vec, vec, vec, vec,
            pl.BlockSpec((1, D_MODEL, D_FF), lambda b, i: (layer, 0, 0), pipeline_mode=once),
            pl.BlockSpec((1, D_MODEL, D_FF), lambda b, i: (layer, 0, 1), pipeline_mode=once),
            pl.BlockSpec((1, D_FF, D_MODEL), lambda b, i: (layer, 0, 0), pipeline_mode=once),
        ],
        out_specs=pl.BlockSpec((1, t, D_MODEL), lambda b, i: (b, i, 0)),
        out_shape=jax.ShapeDtypeStruct((bsz, seq, D_MODEL), F32),
        compiler_params=pltpu.CompilerParams(vmem_limit_bytes=VMEM_LIMIT),
        name="outffn_ctx" if ctx else "outffn",
    )(xs, y, mod, mod, mod, mod, wo_b, ln1g, ln1b, ln2g, ln2b, wfi_b, wfi_b, wfo_b)


def kernel(x, c, ctx, c_ctx, w_ada, b_ada, w_in, w_o, sink, lam_q1, lam_k1, lam_q2, lam_k2, subln_g, na_bias,
           ln1_g, ln1_b, w_ffn_in, w_ffn_out, ln2_g, ln2_b):
    bsz, seq, _ = x.shape
    assert x.shape == (bsz, seq, D_MODEL) and seq % Q_BLK == 0 and seq // Q_BLK >= 3
    assert bsz < CTX_ROW + 1 <= MOD_ROWS and bsz % CTX_BATCH == 0
    cc = jnp.zeros((MOD_ROWS, D_MODEL), F32).at[:bsz].set(c).at[CTX_ROW].set(c_ctx)
    mod_all = _ada(cc, w_ada, b_ada).reshape(DEPTH, MOD_ROWS, 1, N_MOD * D_MODEL)
    tables = _rope_tables(seq)
    na_tab = _na_table(na_bias)
    w_in_b, wo_b, wfi_b, wfo_b = (_cast_bf16(w) for w in (w_in, w_o, w_ffn_in, w_ffn_out))

    xs, cs = x, ctx
    for l in range(DEPTH):
        last = l == DEPTH - 1
        lam_init = 0.8 - 0.6 * math.exp(-0.3 * l)
        mod = mod_all[l]
        lamv = jnp.stack([lam_q1[l], lam_k1[l], lam_q2[l], lam_k2[l]])
        g = subln_g[l].reshape(1, HEAD_DIM)
        lnp = [v[l].reshape(1, D_MODEL) for v in (ln1_g, ln1_b, ln2_g, ln2_b)]

        p = _inproj(xs, mod, w_in_b, l, tables, ctx=False)
        pc = _inproj(cs, mod, w_in_b, l, None, ctx=True)
        y = _attn(p, pc, sink[l], lamv, g, na_tab[l], lam_init)
        xn = _outffn(xs, y, mod, l, wo_b, *lnp, wfi_b, wfo_b, ctx=False)
        if not last:
            yc = _ctx_attn(pc, sink[l], lamv, g, lam_init)
            cs = _outffn(cs, yc, mod, l, wo_b, *lnp, wfi_b, wfo_b, ctx=True)
        xs = xn
    return xs
```

```python
import functools
import math

import jax
import jax.numpy as jnp
import numpy as np
from jax import lax
from jax.experimental import pallas as pl
from jax.experimental.pallas import tpu as pltpu

F32 = jnp.float32
BF16 = jnp.bfloat16

D_MODEL = 1024
DEPTH = 2
GRID_W = 64
HEAD_DIM = 64
A_HEADS = 6
A_KV_HEADS = 2
A_REP = A_HEADS // A_KV_HEADS
WINDOW = 128
WIN_BLK = 128
B_HEADS = 4
B_QK_DIM = 32
C_HEADS = 6
NA_KH = 8
NA_KW = 16
N_MOD = 6
D_FF = 2816
MIX_WIDTH = (A_HEADS + B_HEADS + C_HEADS) * HEAD_DIM
ROPE_BASE = 10000.0
LN_EPS = 1e-5
NEG_INF = -1e30
LOG2E = 1.4426950408889634
QK_SCALE2_A = HEAD_DIM ** -0.5 * LOG2E
QK_SCALE2_B = B_QK_DIM ** -0.5 * LOG2E
ALPHA = (2.0 * DEPTH) ** 0.25

AQ = 0
AK = AQ + A_HEADS * HEAD_DIM
AV = AK + A_KV_HEADS * HEAD_DIM
BQ = AV + A_KV_HEADS * HEAD_DIM
BK = BQ + B_HEADS * 2 * B_QK_DIM
BV = BK + B_HEADS * 2 * B_QK_DIM
CQ = BV + B_HEADS * HEAD_DIM
CK = CQ + C_HEADS * HEAD_DIM
CV = CK + C_HEADS * HEAD_DIM
PROJ_WIDTH = CV + C_HEADS * HEAD_DIM

YA = 0
YB = YA + A_HEADS * HEAD_DIM
YC = YB + B_HEADS * HEAD_DIM

LANES = 128
MOD_ROWS = 16
CTX_ROW = 8
VMEM_LIMIT = 56 * 1024 * 1024

Q_BLK = 256
CTX_BATCH = 2
ATTN_LOOKAHEAD = 2
IN_TILE = 1024
FFN_TILE = 1024
SUB_TILE = 256
CAST_BLOCK_BYTES = 4 * 1024 * 1024


def _dot(a, b):
    return jnp.dot(a, b, preferred_element_type=F32)


def _dot_nt(a, b):
    return lax.dot_general(a, b, (((1,), (1,)), ((), ())), preferred_element_type=F32)


def _silu(v):
    return v / (1.0 + jnp.exp(-v))


def _layer_norm(v, g, b):
    mu = jnp.mean(v, axis=-1, keepdims=True)
    d = v - mu
    var = jnp.mean(d * d, axis=-1, keepdims=True)
    return d * lax.rsqrt(var + LN_EPS) * g + b


def _ada_kernel(c_ref, w_ref, b_ref, o_ref):
    s = _silu(c_ref[...])
    o_ref[0] = _dot(s.astype(BF16), w_ref[0].astype(BF16)) + b_ref[0]


def _ada(cc, w_ada, b_ada):
    tn = D_MODEL
    return pl.pallas_call(
        _ada_kernel,
        grid=(DEPTH, N_MOD * D_MODEL // tn),
        in_specs=[
            pl.BlockSpec((MOD_ROWS, D_MODEL), lambda l, j: (0, 0)),
            pl.BlockSpec((1, D_MODEL, tn), lambda l, j: (l, 0, j)),
            pl.BlockSpec((1, 1, tn), lambda l, j: (l, 0, j)),
        ],
        out_specs=pl.BlockSpec((1, MOD_ROWS, tn), lambda l, j: (l, 0, j)),
        out_shape=jax.ShapeDtypeStruct((DEPTH, MOD_ROWS, N_MOD * D_MODEL), F32),
        compiler_params=pltpu.CompilerParams(vmem_limit_bytes=VMEM_LIMIT),
        name="ada",
    )(cc, w_ada, b_ada.reshape(DEPTH, 1, N_MOD * D_MODEL))


def _cast_kernel(w_ref, o_ref):
    o_ref[...] = w_ref[...].astype(BF16)


def _cast_bf16(w):
    depth, rows, cols = w.shape
    blk = rows
    while blk * cols * 4 > CAST_BLOCK_BYTES and blk % 32 == 0:
        blk //= 2
    return pl.pallas_call(
        _cast_kernel,
        grid=(depth, rows // blk),
        in_specs=[pl.BlockSpec((1, blk, cols), lambda l, i: (l, i, 0))],
        out_specs=pl.BlockSpec((1, blk, cols), lambda l, i: (l, i, 0)),
        out_shape=jax.ShapeDtypeStruct(w.shape, BF16),
        name="cast_bf16",
    )(w)


def _rope_group(v, cos, sin, off):
    lane = lax.broadcasted_iota(jnp.int32, v.shape, 1)
    low = (lane % (2 * off)) < off
    partner = jnp.where(low, pltpu.roll(v, LANES - off, 1), pltpu.roll(v, off, 1))
    return v * cos + partner * sin


def _staged(n_units, stages):
    state = [None] * n_units
    for step in range(n_units + len(stages) - 1):
        for k, stage in enumerate(stages):
            u = step - k
            if 0 <= u < n_units:
                state[u] = stage(u, state[u])


def _lane_groups(lo, hi):
    return tuple(range(lo // LANES, hi // LANES))


_GROUP_PLAN = {}
for _g in _lane_groups(AQ, AK):
    _GROUP_PLAN[_g] = (0, HEAD_DIM // 4, QK_SCALE2_A)
for _g in _lane_groups(AK, AV):
    _GROUP_PLAN[_g] = (1, HEAD_DIM // 4, None)
for _g in _lane_groups(BQ, BK):
    _GROUP_PLAN[_g] = (2, B_QK_DIM // 4, QK_SCALE2_B)
for _g in _lane_groups(BK, BV):
    _GROUP_PLAN[_g] = (3, B_QK_DIM // 4, None)
for _g in _lane_groups(CQ, CK):
    _GROUP_PLAN[_g] = (None, None, QK_SCALE2_A)


def _inproj_kernel(x_ref, sh_ref, sc_ref, w_ref, *rest, rope, sub):
    if rope:
        tab_ref, o_ref = rest
    else:
        (o_ref,) = rest

    def rows(u):
        return slice(u * sub, (u + 1) * sub)

    def project(u, _):
        h = x_ref[0, rows(u)] * (1.0 + sc_ref[0]) + sh_ref[0]
        return _dot(h.astype(BF16), w_ref[0])

    def rotate_store(u, p):
        for g in range(PROJ_WIDTH // LANES):
            v = p[:, g * LANES:(g + 1) * LANES]
            pair, quarter, qscale = _GROUP_PLAN.get(g, (None, None, None))
            if rope and pair is not None:
                v = _rope_group(v, tab_ref[2 * pair, rows(u)], tab_ref[2 * pair + 1, rows(u)], quarter)
            elif qscale is not None:
                v = v * qscale
            o_ref[0, rows(u), g * LANES:(g + 1) * LANES] = v.astype(BF16)

    _staged(x_ref.shape[1] // sub, [project, rotate_store])


def _inproj(xs, mod, w_in_b, layer, tables, *, ctx):
    bsz, seq, _ = xs.shape
    t = min(IN_TILE, seq)
    row = (lambda b: CTX_ROW) if ctx else (lambda b: b)
    in_specs = [
        pl.BlockSpec((1, t, D_MODEL), lambda b, i: (b, i, 0)),
        pl.BlockSpec((1, 1, D_MODEL), lambda b, i: (row(b), 0, 0)),
        pl.BlockSpec((1, 1, D_MODEL), lambda b, i: (row(b), 0, 1)),
        pl.BlockSpec((1, D_MODEL, PROJ_WIDTH), lambda b, i: (layer, 0, 0), pipeline_mode=pl.Buffered(1)),
    ]
    args = [xs, mod, mod, w_in_b]
    if tables is not None:
        in_specs.append(pl.BlockSpec((tables.shape[0], t, LANES), lambda b, i: (0, i, 0)))
        args.append(tables)
    return pl.pallas_call(
        functools.partial(_inproj_kernel, rope=tables is not None, sub=min(SUB_TILE, t)),
        grid=(bsz, seq // t),
        in_specs=in_specs,
        out_specs=pl.BlockSpec((1, t, PROJ_WIDTH), lambda b, i: (b, i, 0)),
        out_shape=jax.ShapeDtypeStruct((bsz, seq, PROJ_WIDTH), BF16),
        compiler_params=pltpu.CompilerParams(vmem_limit_bytes=VMEM_LIMIT),
        name="inproj_ctx" if ctx else "inproj",
    )(*args)


def _rope_tables(seq):
    f32 = np.float32
    tpos = np.arange(seq, dtype=np.int32)
    rows = (tpos // GRID_W).astype(f32)[:, None]
    cols = (tpos % GRID_W).astype(f32)[:, None]
    lane = np.arange(LANES, dtype=np.int32)

    def table(head_dim):
        quarter = head_dim // 4
        inv = f32(ROPE_BASE) ** (-np.arange(quarter, dtype=f32) / f32(quarter))
        freq = inv[lane % quarter][None, :].astype(f32)
        use_cols = ((lane % head_dim) >= head_dim // 2)[None, :]
        ang = np.where(use_cols, cols * freq, rows * freq).astype(f32)
        sign = np.where((lane % (2 * quarter)) < quarter, f32(-1.0), f32(1.0))[None, :]
        return np.cos(ang).astype(f32), (np.sin(ang) * sign).astype(f32)

    cos_a, sin_a = table(HEAD_DIM)
    cos_b, sin_b = table(B_QK_DIM)
    return np.stack([cos_a * f32(QK_SCALE2_A), sin_a * f32(QK_SCALE2_A), cos_a, sin_a,
                     cos_b * f32(QK_SCALE2_B), sin_b * f32(QK_SCALE2_B), cos_b, sin_b]).astype(f32)


def _lam_value(lam_ref, lam_init):
    v = lam_ref[...]
    s1 = jnp.sum(v[0:1, :] * v[1:2, :], axis=-1, keepdims=True)
    s2 = jnp.sum(v[2:3, :] * v[3:4, :], axis=-1, keepdims=True)
    return jnp.exp(s1) - jnp.exp(s2) + lam_init


def _row_stack(parts):
    return jnp.concatenate(parts, axis=0)


def _dot_blocks(e, v):
    if not isinstance(v, (list, tuple)):
        return _dot(e, v)
    t = e.shape[0] // len(v)
    return _row_stack([_dot(e[k * t:(k + 1) * t], vk) for k, vk in enumerate(v)])


def _softmax_av(pieces, extra_logit=None):
    weights, den = _softmax_weights([(s, bias2) for s, bias2, _ in pieces], extra_logit)
    return _weighted_values(weights, [v for _, _, v in pieces], den)


def _softmax_weights(pieces, extra_logit=None):
    terms, m2 = [], None
    for s, bias2 in pieces:
        t = s if bias2 is None else s + bias2
        ms = jnp.max(t, axis=-1, keepdims=True)
        terms.append(t)
        m2 = ms if m2 is None else jnp.maximum(m2, ms)
    den = None
    if extra_logit is not None:
        m2 = jnp.maximum(m2, extra_logit * LOG2E)
        den = jnp.exp2(extra_logit * LOG2E - m2)
    weights = []
    for t in terms:
        e = jnp.exp2(t - m2)
        ls = jnp.sum(e, axis=-1, keepdims=True)
        den = ls if den is None else den + ls
        weights.append(e.astype(BF16))
    return weights, den


def _weighted_values(weights, values, den):
    out = None
    for e, v in zip(weights, values):
        o = _dot_blocks(e, v)
        out = o if out is None else out + o
    return out / den


def _lane_group(col):
    g0 = col // LANES * LANES
    return slice(g0, g0 + LANES)


def _placed(q, col):
    t, w = q.shape
    off = col % LANES
    parts = [jnp.zeros((t, off), q.dtype)] if off else []
    parts.append(q)
    if LANES - off - w:
        parts.append(jnp.zeros((t, LANES - off - w), q.dtype))
    return jnp.concatenate(parts, axis=-1)


def _pipelined(units, offsets):
    state = [None] * len(units)
    for step in range(len(units) + max(offsets)):
        for k, off in enumerate(offsets):
            u = step - off
            if 0 <= u < len(units):
                state[u] = units[u][k](state[u])


def _head_cols(base, h):
    return slice(base + h * HEAD_DIM, base + (h + 1) * HEAD_DIM)


def _diff_head_weights(scores, lam):
    t = scores[0].shape[0] // 2
    mx = None
    for s in scores:
        ms = jnp.max(s, axis=-1, keepdims=True)
        mx = ms if mx is None else jnp.maximum(mx, ms)
    es = [jnp.exp2(s - mx) for s in scores]
    den = None
    for e in es:
        ls = jnp.sum(e, axis=-1, keepdims=True)
        den = ls if den is None else den + ls
    w = lam * den[0:t] / den[t:2 * t]
    return [(e[0:t] - w * e[t:2 * t]).astype(BF16) for e in es], den[0:t]


def _diff_head_out(weights, den, v_pieces, g, lam_init, v_off):
    o = _weighted_values(weights, v_pieces, den)[:, v_off:v_off + HEAD_DIM]
    ms = jnp.mean(o * o, axis=-1, keepdims=True)
    return o * lax.rsqrt(ms + LN_EPS) * g * (1.0 - lam_init)


def _a_queries(ref, rows):
    return _row_stack([_placed(ref[0, rows, _head_cols(AQ, h)], AK + (h // A_REP) * HEAD_DIM) for h in range(A_HEADS)])


def _a_sink(sink_ref, t):
    return _row_stack([jnp.full((t, 1), sink_ref[h], F32) for h in range(A_HEADS)])


def _a_store(o, o_ref, rows, left):
    t = o.shape[0] // A_HEADS
    for j in range(A_HEADS // 2):
        halves = []
        for h in (2 * j, 2 * j + 1):
            oh = o[h * t:(h + 1) * t]
            if h // A_REP != h % 2:
                oh = pltpu.roll(oh, HEAD_DIM, 1)
            halves.append(oh)
        o_ref[0, rows, YA + j * LANES:YA + (j + 1) * LANES] = jnp.where(left, halves[0], halves[1]).astype(BF16)


def _b_queries(ref, rows, h, lane):
    qcol = BQ + h * 2 * B_QK_DIM
    qg = ref[0, rows, _lane_group(qcol)]
    zero = jnp.zeros_like(qg)
    offs = [qcol % LANES + m * B_QK_DIM for m in range(2)]
    return _row_stack([jnp.where((lane >= off) & (lane < off + B_QK_DIM), qg, zero) for off in offs])


def _c_queries(ref, rows, j, left):
    qg = ref[0, rows, _lane_group(CQ + j * LANES)]
    zero = jnp.zeros_like(qg)
    return _row_stack([jnp.where(left, qg, zero), jnp.where(left, zero, qg)])


def _attn_kernel(sink_ref, p_ref, pc_ref, lam_ref, g_ref, tb_ref, o_ref, *, lam_init, seq):
    n = pl.program_id(1)
    units_a, units_b = [], []
    lane = lax.broadcasted_iota(jnp.int32, (1, LANES), 1)
    left = lane < HEAD_DIM

    nb = seq // WIN_BLK
    for sub in range(Q_BLK // WIN_BLK):
        blk = n * (Q_BLK // WIN_BLK) + sub
        q0 = pl.multiple_of(blk * WIN_BLK, WIN_BLK)
        ws = pl.multiple_of(jnp.clip(blk - 1, 0, nb - 3) * WIN_BLK, WIN_BLK)
        qpos = q0 + lax.broadcasted_iota(jnp.int32, (A_HEADS * WIN_BLK, 3 * WIN_BLK), 0) % WIN_BLK
        kpos = ws + lax.broadcasted_iota(jnp.int32, (A_HEADS * WIN_BLK, 3 * WIN_BLK), 1)
        valid = jnp.abs(qpos - kpos) <= WINDOW
        rows = slice(sub * WIN_BLK, (sub + 1) * WIN_BLK)

        def start(_, q0=q0, ws=ws, valid=valid):
            q = _a_queries(p_ref, pl.ds(q0, WIN_BLK))
            kw = p_ref[0, pl.ds(ws, 3 * WIN_BLK), _lane_group(AK)]
            kc = pc_ref[0, :, _lane_group(AK)]
            return jnp.where(valid, _dot_nt(q, kw), NEG_INF), _dot_nt(q, kc)

        def soft(scores):
            return _softmax_weights([(scores[0], None), (scores[1], None)], extra_logit=_a_sink(sink_ref, WIN_BLK))

        def finish(weights_den, ws=ws, rows=rows):
            vw = p_ref[0, pl.ds(ws, 3 * WIN_BLK), _lane_group(AV)]
            vc = pc_ref[0, :, _lane_group(AV)]
            _a_store(_weighted_values(weights_den[0], [vw, vc], weights_den[1]), o_ref, rows, left)

        units_a.append((start, soft, finish))

    lam = _lam_value(lam_ref, lam_init)
    qb = pl.multiple_of(n * Q_BLK, Q_BLK)
    for h in range(B_HEADS):
        def start(_, h=h):
            q = _b_queries(p_ref, pl.ds(qb, Q_BLK), h, lane)
            kcol = BK + h * 2 * B_QK_DIM
            k_lat = p_ref[0, :, _lane_group(kcol)]
            k_ctx = pc_ref[0, :, _lane_group(kcol)]
            return _dot_nt(q, k_lat), _dot_nt(q, k_ctx)

        def soft(scores):
            return _diff_head_weights(scores, lam)

        def finish(weights_den, h=h):
            vcol = BV + h * HEAD_DIM
            v_lat = p_ref[0, :, _lane_group(vcol)]
            v_ctx = pc_ref[0, :, _lane_group(vcol)]
            o = _diff_head_out(*weights_den, [v_lat, v_ctx], g_ref[...], lam_init, vcol % LANES)
            o_ref[0, :, _head_cols(YB, h)] = o.astype(BF16)

        units_b.append((start, soft, finish))

    n_rows = seq // GRID_W
    blocks = []
    for i in range(Q_BLK // GRID_W):
        r = n * (Q_BLK // GRID_W) + i
        rs = jnp.clip(r - NA_KH // 2, 0, n_rows - NA_KH)
        dr0 = (NA_KH - 1) - (r - rs)
        qr = pl.multiple_of(r * GRID_W, GRID_W)
        kr = pl.multiple_of(rs * GRID_W, GRID_W)
        blocks += [(i, j, qr, kr, dr0) for j in range(C_HEADS // 2)]

    def start_c(_):
        s_w, s_c = [], []
        for _, j, qr, kr, _ in blocks:
            q = _c_queries(p_ref, pl.ds(qr, GRID_W), j, left)
            s_w.append(_dot_nt(q, p_ref[0, pl.ds(kr, NA_KH * GRID_W), _lane_group(CK + j * LANES)]))
            s_c.append(_dot_nt(q, pc_ref[0, :, _lane_group(CK + j * LANES)]))
        return _row_stack(s_w), _row_stack(s_c)

    def soft_c(scores):
        bias2 = _row_stack([
            jnp.concatenate([jnp.where(left, tb_ref[h, dr0 + kh], tb_ref[h, dr0 + kh + 1])
                             for kh in range(0, NA_KH, 2)], axis=-1)
            for _, j, _, _, dr0 in blocks for h in (2 * j, 2 * j + 1)])
        return _softmax_weights([(scores[0], bias2), (scores[1], None)])

    def finish_c(weights_den):
        vw = [p_ref[0, pl.ds(kr, NA_KH * GRID_W), _lane_group(CV + j * LANES)] for _, j, _, kr, _ in blocks]
        vc = [pc_ref[0, :, _lane_group(CV + j * LANES)] for _, j, _, _, _ in blocks]
        o = _weighted_values(weights_den[0], [vw, vc], weights_den[1])
        for k, (i, j, _, _, _) in enumerate(blocks):
            ok = o[2 * k * GRID_W:2 * (k + 1) * GRID_W]
            o_ref[0, i * GRID_W:(i + 1) * GRID_W, YC + j * LANES:YC + (j + 1) * LANES] = (
                jnp.where(left, ok[0:GRID_W], ok[GRID_W:2 * GRID_W]).astype(BF16))

    _pipelined([(start_c, soft_c, finish_c)] + units_a + units_b, (0, ATTN_LOOKAHEAD, ATTN_LOOKAHEAD))


def _attn(p, pc, sink_l, lamv, g, tb, lam_init):
    bsz, seq, _ = p.shape
    n_ctx = pc.shape[1]
    return pl.pallas_call(
        functools.partial(_attn_kernel, lam_init=lam_init, seq=seq),
        grid=(bsz, seq // Q_BLK),
        in_specs=[
            pl.BlockSpec(memory_space=pltpu.SMEM),
            pl.BlockSpec((1, seq, PROJ_WIDTH), lambda b, n: (b, 0, 0)),
            pl.BlockSpec((1, n_ctx, PROJ_WIDTH), lambda b, n: (b, 0, 0)),
            pl.BlockSpec((4, B_QK_DIM), lambda b, n: (0, 0)),
            pl.BlockSpec((1, HEAD_DIM), lambda b, n: (0, 0)),
            pl.BlockSpec(tb.shape, lambda b, n: (0, 0, 0, 0)),
        ],
        out_specs=pl.BlockSpec((1, Q_BLK, MIX_WIDTH), lambda b, n: (b, n, 0)),
        out_shape=jax.ShapeDtypeStruct((bsz, seq, MIX_WIDTH), BF16),
        compiler_params=pltpu.CompilerParams(vmem_limit_bytes=VMEM_LIMIT),
        name="attn",
    )(sink_l, p, pc, lamv, g, tb)


def _ctx_attn_kernel(sink_ref, pc_ref, lam_ref, g_ref, o_ref, *, lam_init):
    t = pc_ref.shape[1]
    every = slice(0, t)
    lane = lax.broadcasted_iota(jnp.int32, (1, LANES), 1)
    left = lane < HEAD_DIM
    lam = _lam_value(lam_ref, lam_init)

    def units_of(pc, out):
        def start_a(_):
            return _dot_nt(_a_queries(pc, every), pc[0, :, _lane_group(AK)])

        def finish_a(s):
            o = _softmax_av([(s, None, pc[0, :, _lane_group(AV)])], extra_logit=_a_sink(sink_ref, t))
            _a_store(o, out, every, left)

        units = [(start_a, finish_a)]
        for h in range(B_HEADS):
            def start(_, h=h):
                return (_dot_nt(_b_queries(pc, every, h, lane), pc[0, :, _lane_group(BK + h * HEAD_DIM)]),)

            def finish(scores, h=h):
                vcol = BV + h * HEAD_DIM
                o = _diff_head_out(*_diff_head_weights(scores, lam), [pc[0, :, _lane_group(vcol)]], g_ref[...],
                                   lam_init, vcol % LANES)
                out[0, :, _head_cols(YB, h)] = o.astype(BF16)

            units.append((start, finish))

        def start_c(_):
            return _row_stack([_dot_nt(_c_queries(pc, every, j, left), pc[0, :, _lane_group(CK + j * LANES)])
                               for j in range(C_HEADS // 2)])

        def finish_c(s):
            o = _softmax_av([(s, None, [pc[0, :, _lane_group(CV + j * LANES)] for j in range(C_HEADS // 2)])])
            for j in range(C_HEADS // 2):
                oj = o[2 * j * t:2 * (j + 1) * t]
                out[0, :, YC + j * LANES:YC + (j + 1) * LANES] = jnp.where(left, oj[0:t], oj[t:2 * t]).astype(BF16)

        return units + [(start_c, finish_c)]

    units = []
    for bi in range(pc_ref.shape[0]):
        units += units_of(pc_ref.at[pl.ds(bi, 1)], o_ref.at[pl.ds(bi, 1)])
    _pipelined(units, (0, ATTN_LOOKAHEAD))


def _ctx_attn(pc, sink_l, lamv, g, lam_init):
    bsz, n_ctx, _ = pc.shape
    return pl.pallas_call(
        functools.partial(_ctx_attn_kernel, lam_init=lam_init),
        grid=(bsz // CTX_BATCH,),
        in_specs=[
            pl.BlockSpec(memory_space=pltpu.SMEM),
            pl.BlockSpec((CTX_BATCH, n_ctx, PROJ_WIDTH), lambda b: (b, 0, 0)),
            pl.BlockSpec((4, B_QK_DIM), lambda b: (0, 0)),
            pl.BlockSpec((1, HEAD_DIM), lambda b: (0, 0)),
        ],
        out_specs=pl.BlockSpec((CTX_BATCH, n_ctx, MIX_WIDTH), lambda b: (b, 0, 0)),
        out_shape=jax.ShapeDtypeStruct((bsz, n_ctx, MIX_WIDTH), BF16),
        compiler_params=pltpu.CompilerParams(vmem_limit_bytes=VMEM_LIMIT),
        name="ctx_attn",
    )(sink_l, pc, lamv, g)


N_DR = 2 * NA_KH - 1
N_DC = 2 * NA_KW - 1


def _na_table_kernel(nb_ref, o_ref):
    wq = lax.broadcasted_iota(jnp.int32, (GRID_W, LANES), 0)
    wk = lax.broadcasted_iota(jnp.int32, (GRID_W, LANES), 1) % GRID_W
    dc = jnp.clip(wk - wq, -(NA_KW - 1), NA_KW - 1) + (NA_KW - 1)
    cs = jnp.clip(wq - NA_KW // 2, 0, GRID_W - NA_KW)
    valid = (wk >= cs) & (wk < cs + NA_KW)
    for h in range(C_HEADS):
        base = (pl.program_id(0) * C_HEADS + h) * (N_DR * N_DC)
        for dr in range(N_DR):
            acc = jnp.zeros((GRID_W, LANES), F32)
            for c in range(N_DC):
                acc = jnp.where(dc == c, nb_ref[base + dr * N_DC + c], acc)
            o_ref[0, h, dr] = jnp.where(valid, acc * LOG2E, NEG_INF)


def _na_table(na_bias):
    return pl.pallas_call(
        _na_table_kernel,
        grid=(DEPTH,),
        in_specs=[pl.BlockSpec(memory_space=pltpu.SMEM)],
        out_specs=pl.BlockSpec((1, C_HEADS, N_DR, GRID_W, LANES), lambda l: (l, 0, 0, 0, 0)),
        out_shape=jax.ShapeDtypeStruct((DEPTH, C_HEADS, N_DR, GRID_W, LANES), F32),
        name="na_table",
    )(na_bias.reshape(-1))


def _outffn_kernel(x_ref, y_ref, g1_ref, sh2_ref, sc2_ref, g2_ref, wo_ref, ln1g_ref, ln1b_ref, ln2g_ref, ln2b_ref,
                   wg_ref, wu_ref, wd_ref, o_ref, *, sub):
    def rows(u):
        return slice(u * sub, (u + 1) * sub)

    def out_proj(u, _):
        return _dot(y_ref[0, rows(u)], wo_ref[0])

    def norm_gate_up(u, y):
        xn = _layer_norm(ALPHA * x_ref[0, rows(u)] + g1_ref[0] * y, ln1g_ref[...], ln1b_ref[...])
        h = (xn * (1.0 + sc2_ref[0]) + sh2_ref[0]).astype(BF16)
        return xn, _dot(h, wg_ref[0]), _dot(h, wu_ref[0])

    def act_down(u, st):
        xn, gate, up = st
        return xn, _dot((_silu(gate) * up).astype(BF16), wd_ref[0])

    def norm_store(u, st):
        xn, ff = st
        o_ref[0, rows(u)] = _layer_norm(ALPHA * xn + g2_ref[0] * ff, ln2g_ref[...], ln2b_ref[...])

    _staged(x_ref.shape[1] // sub, [out_proj, norm_gate_up, act_down, norm_store])


def _outffn(xs, y, mod, layer, wo_b, ln1g, ln1b, ln2g, ln2b, wfi_b, wfo_b, *, ctx):
    bsz, seq, _ = xs.shape
    t = min(FFN_TILE, seq)
    row = (lambda b: CTX_ROW) if ctx else (lambda b: b)
    once = pl.Buffered(1)

    def mod_spec(k):
        return pl.BlockSpec((1, 1, D_MODEL), lambda b, i: (row(b), 0, k))

    vec = pl.BlockSpec((1, D_MODEL), lambda b, i: (0, 0))
    return pl.pallas_call(
        functools.partial(_outffn_kernel, sub=min(SUB_TILE, t // 2)),
        grid=(bsz, seq // t),
        in_specs=[
            pl.BlockSpec((1, t, D_MODEL), lambda b, i: (b, i, 0)),
            pl.BlockSpec((1, t, MIX_WIDTH), lambda b, i: (b, i, 0)),
            mod_spec(2), mod_spec(3), mod_spec(4), mod_spec(5),
            pl.BlockSpec((1, MIX_WIDTH, D_MODEL), lambda b, i: (layer, 0, 0), pipeline_mode=once),
            vec, vec, vec, vec,
            pl.BlockSpec((1, D_MODEL, D_FF), lambda b, i: (layer, 0, 0), pipeline_mode=once),
            pl.BlockSpec((1, D_MODEL, D_FF), lambda b, i: (layer, 0, 1), pipeline_mode=once),
            pl.BlockSpec((1, D_FF, D_MODEL), lambda b, i: (layer, 0, 0), pipeline_mode=once),
        ],
        out_specs=pl.BlockSpec((1, t, D_MODEL), lambda b, i: (b, i, 0)),
        out_shape=jax.ShapeDtypeStruct((bsz, seq, D_MODEL), F32),
        compiler_params=pltpu.CompilerParams(vmem_limit_bytes=VMEM_LIMIT),
        name="outffn_ctx" if ctx else "outffn",
    )(xs, y, mod, mod, mod, mod, wo_b, ln1g, ln1b, ln2g, ln2b, wfi_b, wfi_b, wfo_b)


def kernel(x, c, ctx, c_ctx, w_ada, b_ada, w_in, w_o, sink, lam_q1, lam_k1, lam_q2, lam_k2, subln_g, na_bias,
           ln1_g, ln1_b, w_ffn_in, w_ffn_out, ln2_g, ln2_b):
    bsz, seq, _ = x.shape
    assert x.shape == (bsz, seq, D_MODEL) and seq % Q_BLK == 0 and seq // Q_BLK >= 3
    assert bsz < CTX_ROW + 1 <= MOD_ROWS and bsz % CTX_BATCH == 0
    cc = jnp.zeros((MOD_ROWS, D_MODEL), F32).at[:bsz].set(c).at[CTX_ROW].set(c_ctx)
    mod_all = _ada(cc, w_ada, b_ada).reshape(DEPTH, MOD_ROWS, 1, N_MOD * D_MODEL)
    tables = _rope_tables(seq)
    na_tab = _na_table(na_bias)
    w_in_b, wo_b, wfi_b, wfo_b = (_cast_bf16(w) for w in (w_in, w_o, w_ffn_in, w_ffn_out))

    xs, cs = x, ctx
    for l in range(DEPTH):
        last = l == DEPTH - 1
        lam_init = 0.8 - 0.6 * math.exp(-0.3 * l)
        mod = mod_all[l]
        lamv = jnp.stack([lam_q1[l], lam_k1[l], lam_q2[l], lam_k2[l]])
        g = subln_g[l].reshape(1, HEAD_DIM)
        lnp = [v[l].reshape(1, D_MODEL) for v in (ln1_g, ln1_b, ln2_g, ln2_b)]

        p = _inproj(xs, mod, w_in_b, l, tables, ctx=False)
        n_ctx = cs.shape[1]
        flat = (1, bsz * n_ctx)
        pc = _inproj(cs.reshape(*flat, D_MODEL), mod, w_in_b, l, None, ctx=True).reshape(bsz, n_ctx, PROJ_WIDTH)
        y = _attn(p, pc, sink[l], lamv, g, na_tab[l], lam_init)
        xn = _outffn(xs, y, mod, l, wo_b, *lnp, wfi_b, wfo_b, ctx=False)
        if not last:
            yc = _ctx_attn(pc, sink[l], lamv, g, lam_init)
            cs = _outffn(cs.reshape(*flat, D_MODEL), yc.reshape(*flat, MIX_WIDTH), mod, l, wo_b, *lnp, wfi_b, wfo_b,
                         ctx=True).reshape(bsz, n_ctx, D_MODEL)
        xs = xn
    return xs
```

```python
import functools
import math

import jax
import jax.numpy as jnp
import numpy as np
from jax import lax
from jax.experimental import pallas as pl
from jax.experimental.pallas import tpu as pltpu

F32 = jnp.float32
BF16 = jnp.bfloat16

D_MODEL = 1024
DEPTH = 2
GRID_W = 64
HEAD_DIM = 64
A_HEADS = 6
A_KV_HEADS = 2
A_REP = A_HEADS // A_KV_HEADS
WINDOW = 128
WIN_BLK = 128
B_HEADS = 4
B_QK_DIM = 32
C_HEADS = 6
NA_KH = 8
NA_KW = 16
N_MOD = 6
D_FF = 2816
MIX_WIDTH = (A_HEADS + B_HEADS + C_HEADS) * HEAD_DIM
ROPE_BASE = 10000.0
LN_EPS = 1e-5
NEG_INF = -1e30
LOG2E = 1.4426950408889634
QK_SCALE2_A = HEAD_DIM ** -0.5 * LOG2E
QK_SCALE2_B = B_QK_DIM ** -0.5 * LOG2E
ALPHA = (2.0 * DEPTH) ** 0.25

AQ = 0
AK = AQ + A_HEADS * HEAD_DIM
AV = AK + A_KV_HEADS * HEAD_DIM
BQ = AV + A_KV_HEADS * HEAD_DIM
BK = BQ + B_HEADS * 2 * B_QK_DIM
BV = BK + B_HEADS * 2 * B_QK_DIM
CQ = BV + B_HEADS * HEAD_DIM
CK = CQ + C_HEADS * HEAD_DIM
CV = CK + C_HEADS * HEAD_DIM
PROJ_WIDTH = CV + C_HEADS * HEAD_DIM

YA = 0
YB = YA + A_HEADS * HEAD_DIM
YC = YB + B_HEADS * HEAD_DIM

LANES = 128
MOD_ROWS = 16
CTX_ROW = 8
VMEM_LIMIT = 56 * 1024 * 1024

Q_BLK = 256
CTX_BATCH = 2
ATTN_LOOKAHEAD = 1
IN_TILE = 1024
FFN_TILE = 1024
SUB_TILE = 256
CAST_BLOCK_BYTES = 4 * 1024 * 1024


def _dot(a, b):
    return jnp.dot(a, b, preferred_element_type=F32)


def _dot_nt(a, b):
    return lax.dot_general(a, b, (((1,), (1,)), ((), ())), preferred_element_type=F32)


def _silu(v):
    return v / (1.0 + jnp.exp(-v))


def _layer_norm(v, g, b):
    mu = jnp.mean(v, axis=-1, keepdims=True)
    d = v - mu
    var = jnp.mean(d * d, axis=-1, keepdims=True)
    return d * lax.rsqrt(var + LN_EPS) * g + b


def _ada_kernel(c_ref, w_ref, b_ref, o_ref):
    s = _silu(c_ref[...])
    o_ref[0] = _dot(s.astype(BF16), w_ref[0].astype(BF16)) + b_ref[0]


def _ada(cc, w_ada, b_ada):
    tn = D_MODEL
    return pl.pallas_call(
        _ada_kernel,
        grid=(DEPTH, N_MOD * D_MODEL // tn),
        in_specs=[
            pl.BlockSpec((MOD_ROWS, D_MODEL), lambda l, j: (0, 0)),
            pl.BlockSpec((1, D_MODEL, tn), lambda l, j: (l, 0, j)),
            pl.BlockSpec((1, 1, tn), lambda l, j: (l, 0, j)),
        ],
        out_specs=pl.BlockSpec((1, MOD_ROWS, tn), lambda l, j: (l, 0, j)),
        out_shape=jax.ShapeDtypeStruct((DEPTH, MOD_ROWS, N_MOD * D_MODEL), F32),
        compiler_params=pltpu.CompilerParams(vmem_limit_bytes=VMEM_LIMIT),
        name="ada",
    )(cc, w_ada, b_ada.reshape(DEPTH, 1, N_MOD * D_MODEL))


def _cast_kernel(w_ref, o_ref):
    o_ref[...] = w_ref[...].astype(BF16)


def _cast_bf16(w):
    depth, rows, cols = w.shape
    blk = rows
    while blk * cols * 4 > CAST_BLOCK_BYTES and blk % 32 == 0:
        blk //= 2
    return pl.pallas_call(
        _cast_kernel,
        grid=(depth, rows // blk),
        in_specs=[pl.BlockSpec((1, blk, cols), lambda l, i: (l, i, 0))],
        out_specs=pl.BlockSpec((1, blk, cols), lambda l, i: (l, i, 0)),
        out_shape=jax.ShapeDtypeStruct(w.shape, BF16),
        name="cast_bf16",
    )(w)


def _rope_group(v, cos, sin, off):
    lane = lax.broadcasted_iota(jnp.int32, v.shape, 1)
    low = (lane % (2 * off)) < off
    partner = jnp.where(low, pltpu.roll(v, LANES - off, 1), pltpu.roll(v, off, 1))
    return v * cos + partner * sin


def _staged(n_units, stages):
    state = [None] * n_units
    for step in range(n_units + len(stages) - 1):
        for k, stage in enumerate(stages):
            u = step - k
            if 0 <= u < n_units:
                state[u] = stage(u, state[u])


def _lane_groups(lo, hi):
    return tuple(range(lo // LANES, hi // LANES))


_GROUP_PLAN = {}
for _g in _lane_groups(AQ, AK):
    _GROUP_PLAN[_g] = (0, HEAD_DIM // 4, QK_SCALE2_A)
for _g in _lane_groups(AK, AV):
    _GROUP_PLAN[_g] = (1, HEAD_DIM // 4, None)
for _g in _lane_groups(BQ, BK):
    _GROUP_PLAN[_g] = (2, B_QK_DIM // 4, QK_SCALE2_B)
for _g in _lane_groups(BK, BV):
    _GROUP_PLAN[_g] = (3, B_QK_DIM // 4, None)
for _g in _lane_groups(CQ, CK):
    _GROUP_PLAN[_g] = (None, None, QK_SCALE2_A)


def _inproj_kernel(x_ref, sh_ref, sc_ref, w_ref, *rest, rope, sub):
    if rope:
        tab_ref, o_ref = rest
    else:
        (o_ref,) = rest

    def rows(u):
        return slice(u * sub, (u + 1) * sub)

    def project(u, _):
        h = x_ref[0, rows(u)] * (1.0 + sc_ref[0]) + sh_ref[0]
        return _dot(h.astype(BF16), w_ref[0])

    def rotate_store(u, p):
        for g in range(PROJ_WIDTH // LANES):
            v = p[:, g * LANES:(g + 1) * LANES]
            pair, quarter, qscale = _GROUP_PLAN.get(g, (None, None, None))
            if rope and pair is not None:
                v = _rope_group(v, tab_ref[2 * pair, rows(u)], tab_ref[2 * pair + 1, rows(u)], quarter)
            elif qscale is not None:
                v = v * qscale
            o_ref[0, rows(u), g * LANES:(g + 1) * LANES] = v.astype(BF16)

    _staged(x_ref.shape[1] // sub, [project, rotate_store])


def _inproj(xs, mod, w_in_b, layer, tables, *, ctx):
    bsz, seq, _ = xs.shape
    t = min(IN_TILE, seq)
    row = (lambda b: CTX_ROW) if ctx else (lambda b: b)
    in_specs = [
        pl.BlockSpec((1, t, D_MODEL), lambda b, i: (b, i, 0)),
        pl.BlockSpec((1, 1, D_MODEL), lambda b, i: (row(b), 0, 0)),
        pl.BlockSpec((1, 1, D_MODEL), lambda b, i: (row(b), 0, 1)),
        pl.BlockSpec((1, D_MODEL, PROJ_WIDTH), lambda b, i: (layer, 0, 0), pipeline_mode=pl.Buffered(1)),
    ]
    args = [xs, mod, mod, w_in_b]
    if tables is not None:
        in_specs.append(pl.BlockSpec((tables.shape[0], t, LANES), lambda b, i: (0, i, 0)))
        args.append(tables)
    return pl.pallas_call(
        functools.partial(_inproj_kernel, rope=tables is not None, sub=min(SUB_TILE, t)),
        grid=(bsz, seq // t),
        in_specs=in_specs,
        out_specs=pl.BlockSpec((1, t, PROJ_WIDTH), lambda b, i: (b, i, 0)),
        out_shape=jax.ShapeDtypeStruct((bsz, seq, PROJ_WIDTH), BF16),
        compiler_params=pltpu.CompilerParams(vmem_limit_bytes=VMEM_LIMIT),
        name="inproj_ctx" if ctx else "inproj",
    )(*args)


def _rope_tables(seq):
    f32 = np.float32
    tpos = np.arange(seq, dtype=np.int32)
    rows = (tpos // GRID_W).astype(f32)[:, None]
    cols = (tpos % GRID_W).astype(f32)[:, None]
    lane = np.arange(LANES, dtype=np.int32)

    def table(head_dim):
        quarter = head_dim // 4
        inv = f32(ROPE_BASE) ** (-np.arange(quarter, dtype=f32) / f32(quarter))
        freq = inv[lane % quarter][None, :].astype(f32)
        use_cols = ((lane % head_dim) >= head_dim // 2)[None, :]
        ang = np.where(use_cols, cols * freq, rows * freq).astype(f32)
        sign = np.where((lane % (2 * quarter)) < quarter, f32(-1.0), f32(1.0))[None, :]
        return np.cos(ang).astype(f32), (np.sin(ang) * sign).astype(f32)

    cos_a, sin_a = table(HEAD_DIM)
    cos_b, sin_b = table(B_QK_DIM)
    return np.stack([cos_a * f32(QK_SCALE2_A), sin_a * f32(QK_SCALE2_A), cos_a, sin_a,
                     cos_b * f32(QK_SCALE2_B), sin_b * f32(QK_SCALE2_B), cos_b, sin_b]).astype(f32)


def _lam_value(lam_ref, lam_init):
    v = lam_ref[...]
    s1 = jnp.sum(v[0:1, :] * v[1:2, :], axis=-1, keepdims=True)
    s2 = jnp.sum(v[2:3, :] * v[3:4, :], axis=-1, keepdims=True)
    return jnp.exp(s1) - jnp.exp(s2) + lam_init


def _row_stack(parts):
    return jnp.concatenate(parts, axis=0)


def _dot_blocks(e, v):
    if not isinstance(v, (list, tuple)):
        return _dot(e, v)
    t = e.shape[0] // len(v)
    return _row_stack([_dot(e[k * t:(k + 1) * t], vk) for k, vk in enumerate(v)])


def _softmax_av(pieces, extra_logit=None):
    weights, den = _softmax_weights([(s, bias2) for s, bias2, _ in pieces], extra_logit)
    return _weighted_values(weights, [v for _, _, v in pieces], den)


def _softmax_weights(pieces, extra_logit=None):
    terms, m2 = [], None
    for s, bias2 in pieces:
        t = s if bias2 is None else s + bias2
        ms = jnp.max(t, axis=-1, keepdims=True)
        terms.append(t)
        m2 = ms if m2 is None else jnp.maximum(m2, ms)
    den = None
    if extra_logit is not None:
        m2 = jnp.maximum(m2, extra_logit * LOG2E)
        den = jnp.exp2(extra_logit * LOG2E - m2)
    weights = []
    for t in terms:
        e = jnp.exp2(t - m2)
        ls = jnp.sum(e, axis=-1, keepdims=True)
        den = ls if den is None else den + ls
        weights.append(e.astype(BF16))
    return weights, den


def _weighted_values(weights, values, den):
    out = None
    for e, v in zip(weights, values):
        o = _dot_blocks(e, v)
        out = o if out is None else out + o
    return out / den


def _lane_group(col):
    g0 = col // LANES * LANES
    return slice(g0, g0 + LANES)


def _placed(q, col):
    t, w = q.shape
    off = col % LANES
    parts = [jnp.zeros((t, off), q.dtype)] if off else []
    parts.append(q)
    if LANES - off - w:
        parts.append(jnp.zeros((t, LANES - off - w), q.dtype))
    return jnp.concatenate(parts, axis=-1)


def _pipelined(units, offsets):
    state = [None] * len(units)
    for step in range(len(units) + max(offsets)):
        for k, off in enumerate(offsets):
            u = step - off
            if 0 <= u < len(units):
                state[u] = units[u][k](state[u])


def _head_cols(base, h):
    return slice(base + h * HEAD_DIM, base + (h + 1) * HEAD_DIM)


def _diff_head_weights(scores, lam):
    t = scores[0].shape[0] // 2
    mx = None
    for s in scores:
        ms = jnp.max(s, axis=-1, keepdims=True)
        mx = ms if mx is None else jnp.maximum(mx, ms)
    es = [jnp.exp2(s - mx) for s in scores]
    den = None
    for e in es:
        ls = jnp.sum(e, axis=-1, keepdims=True)
        den = ls if den is None else den + ls
    w = lam * den[0:t] / den[t:2 * t]
    return [(e[0:t] - w * e[t:2 * t]).astype(BF16) for e in es], den[0:t]


def _diff_head_out(weights, den, v_pieces, g, lam_init, v_off):
    o = _weighted_values(weights, v_pieces, den)[:, v_off:v_off + HEAD_DIM]
    ms = jnp.mean(o * o, axis=-1, keepdims=True)
    return o * lax.rsqrt(ms + LN_EPS) * g * (1.0 - lam_init)


def _a_queries(ref, rows):
    return _row_stack([_placed(ref[0, rows, _head_cols(AQ, h)], AK + (h // A_REP) * HEAD_DIM) for h in range(A_HEADS)])


def _a_sink(sink_ref, t):
    return _row_stack([jnp.full((t, 1), sink_ref[h], F32) for h in range(A_HEADS)])


def _a_store(o, o_ref, rows, left):
    t = o.shape[0] // A_HEADS
    for j in range(A_HEADS // 2):
        halves = []
        for h in (2 * j, 2 * j + 1):
            oh = o[h * t:(h + 1) * t]
            if h // A_REP != h % 2:
                oh = pltpu.roll(oh, HEAD_DIM, 1)
            halves.append(oh)
        o_ref[0, rows, YA + j * LANES:YA + (j + 1) * LANES] = jnp.where(left, halves[0], halves[1]).astype(BF16)


def _b_queries(ref, rows, h, lane):
    qcol = BQ + h * 2 * B_QK_DIM
    qg = ref[0, rows, _lane_group(qcol)]
    zero = jnp.zeros_like(qg)
    offs = [qcol % LANES + m * B_QK_DIM for m in range(2)]
    return _row_stack([jnp.where((lane >= off) & (lane < off + B_QK_DIM), qg, zero) for off in offs])


def _c_queries(ref, rows, j, left):
    qg = ref[0, rows, _lane_group(CQ + j * LANES)]
    zero = jnp.zeros_like(qg)
    return _row_stack([jnp.where(left, qg, zero), jnp.where(left, zero, qg)])


def _attn_kernel(sink_ref, p_ref, pc_ref, lam_ref, g_ref, tb_ref, o_ref, *, lam_init, seq):
    n = pl.program_id(1)
    units_a, units_b = [], []
    lane = lax.broadcasted_iota(jnp.int32, (1, LANES), 1)
    left = lane < HEAD_DIM

    nb = seq // WIN_BLK
    for sub in range(Q_BLK // WIN_BLK):
        blk = n * (Q_BLK // WIN_BLK) + sub
        q0 = pl.multiple_of(blk * WIN_BLK, WIN_BLK)
        ws = pl.multiple_of(jnp.clip(blk - 1, 0, nb - 3) * WIN_BLK, WIN_BLK)
        qpos = q0 + lax.broadcasted_iota(jnp.int32, (A_HEADS * WIN_BLK, 3 * WIN_BLK), 0) % WIN_BLK
        kpos = ws + lax.broadcasted_iota(jnp.int32, (A_HEADS * WIN_BLK, 3 * WIN_BLK), 1)
        valid = jnp.abs(qpos - kpos) <= WINDOW
        rows = slice(sub * WIN_BLK, (sub + 1) * WIN_BLK)

        def start(_, q0=q0, ws=ws, valid=valid):
            q = _a_queries(p_ref, pl.ds(q0, WIN_BLK))
            kw = p_ref[0, pl.ds(ws, 3 * WIN_BLK), _lane_group(AK)]
            kc = pc_ref[0, :, _lane_group(AK)]
            return jnp.where(valid, _dot_nt(q, kw), NEG_INF), _dot_nt(q, kc)

        def soft(scores):
            return _softmax_weights([(scores[0], None), (scores[1], None)], extra_logit=_a_sink(sink_ref, WIN_BLK))

        def finish(weights_den, ws=ws, rows=rows):
            vw = p_ref[0, pl.ds(ws, 3 * WIN_BLK), _lane_group(AV)]
            vc = pc_ref[0, :, _lane_group(AV)]
            _a_store(_weighted_values(weights_den[0], [vw, vc], weights_den[1]), o_ref, rows, left)

        units_a.append((start, soft, finish))

    lam = _lam_value(lam_ref, lam_init)
    qb = pl.multiple_of(n * Q_BLK, Q_BLK)
    for h in range(B_HEADS):
        def start(_, h=h):
            q = _b_queries(p_ref, pl.ds(qb, Q_BLK), h, lane)
            kcol = BK + h * 2 * B_QK_DIM
            k_lat = p_ref[0, :, _lane_group(kcol)]
            k_ctx = pc_ref[0, :, _lane_group(kcol)]
            return _dot_nt(q, k_lat), _dot_nt(q, k_ctx)

        def soft(scores):
            return _diff_head_weights(scores, lam)

        def finish(weights_den, h=h):
            vcol = BV + h * HEAD_DIM
            v_lat = p_ref[0, :, _lane_group(vcol)]
            v_ctx = pc_ref[0, :, _lane_group(vcol)]
            o = _diff_head_out(*weights_den, [v_lat, v_ctx], g_ref[...], lam_init, vcol % LANES)
            o_ref[0, :, _head_cols(YB, h)] = o.astype(BF16)

        units_b.append((start, soft, finish))

    n_rows = seq // GRID_W
    blocks = []
    for i in range(Q_BLK // GRID_W):
        r = n * (Q_BLK // GRID_W) + i
        rs = jnp.clip(r - NA_KH // 2, 0, n_rows - NA_KH)
        dr0 = (NA_KH - 1) - (r - rs)
        qr = pl.multiple_of(r * GRID_W, GRID_W)
        kr = pl.multiple_of(rs * GRID_W, GRID_W)
        blocks += [(i, j, qr, kr, dr0) for j in range(C_HEADS // 2)]

    def start_c(_):
        s_w, s_c = [], []
        for _, j, qr, kr, _ in blocks:
            q = _c_queries(p_ref, pl.ds(qr, GRID_W), j, left)
            s_w.append(_dot_nt(q, p_ref[0, pl.ds(kr, NA_KH * GRID_W), _lane_group(CK + j * LANES)]))
            s_c.append(_dot_nt(q, pc_ref[0, :, _lane_group(CK + j * LANES)]))
        return _row_stack(s_w), _row_stack(s_c)

    def soft_c(scores):
        bias2 = _row_stack([
            jnp.concatenate([jnp.where(left, tb_ref[h, dr0 + kh], tb_ref[h, dr0 + kh + 1])
                             for kh in range(0, NA_KH, 2)], axis=-1)
            for _, j, _, _, dr0 in blocks for h in (2 * j, 2 * j + 1)])
        return _softmax_weights([(scores[0], bias2), (scores[1], None)])

    def finish_c(weights_den):
        vw = [p_ref[0, pl.ds(kr, NA_KH * GRID_W), _lane_group(CV + j * LANES)] for _, j, _, kr, _ in blocks]
        vc = [pc_ref[0, :, _lane_group(CV + j * LANES)] for _, j, _, _, _ in blocks]
        o = _weighted_values(weights_den[0], [vw, vc], weights_den[1])
        for k, (i, j, _, _, _) in enumerate(blocks):
            ok = o[2 * k * GRID_W:2 * (k + 1) * GRID_W]
            o_ref[0, i * GRID_W:(i + 1) * GRID_W, YC + j * LANES:YC + (j + 1) * LANES] = (
                jnp.where(left, ok[0:GRID_W], ok[GRID_W:2 * GRID_W]).astype(BF16))

    _pipelined([(start_c, soft_c, finish_c)] + units_a + units_b, (0, ATTN_LOOKAHEAD, ATTN_LOOKAHEAD))


def _attn(p, pc, sink_l, lamv, g, tb, lam_init):
    bsz, seq, _ = p.shape
    n_ctx = pc.shape[1]
    return pl.pallas_call(
        functools.partial(_attn_kernel, lam_init=lam_init, seq=seq),
        grid=(bsz, seq // Q_BLK),
        in_specs=[
            pl.BlockSpec(memory_space=pltpu.SMEM),
            pl.BlockSpec((1, seq, PROJ_WIDTH), lambda b, n: (b, 0, 0)),
            pl.BlockSpec((1, n_ctx, PROJ_WIDTH), lambda b, n: (b, 0, 0)),
            pl.BlockSpec((4, B_QK_DIM), lambda b, n: (0, 0)),
            pl.BlockSpec((1, HEAD_DIM), lambda b, n: (0, 0)),
            pl.BlockSpec(tb.shape, lambda b, n: (0, 0, 0, 0)),
        ],
        out_specs=pl.BlockSpec((1, Q_BLK, MIX_WIDTH), lambda b, n: (b, n, 0)),
        out_shape=jax.ShapeDtypeStruct((bsz, seq, MIX_WIDTH), BF16),
        compiler_params=pltpu.CompilerParams(vmem_limit_bytes=VMEM_LIMIT),
        name="attn",
    )(sink_l, p, pc, lamv, g, tb)


def _ctx_attn_kernel(sink_ref, pc_ref, lam_ref, g_ref, o_ref, *, lam_init):
    t = pc_ref.shape[1]
    every = slice(0, t)
    lane = lax.broadcasted_iota(jnp.int32, (1, LANES), 1)
    left = lane < HEAD_DIM
    lam = _lam_value(lam_ref, lam_init)

    def units_of(pc, out):
        def start_a(_):
            return _dot_nt(_a_queries(pc, every), pc[0, :, _lane_group(AK)])

        def finish_a(s):
            o = _softmax_av([(s, None, pc[0, :, _lane_group(AV)])], extra_logit=_a_sink(sink_ref, t))
            _a_store(o, out, every, left)

        units = [(start_a, finish_a)]
        for h in range(B_HEADS):
            def start(_, h=h):
                return (_dot_nt(_b_queries(pc, every, h, lane), pc[0, :, _lane_group(BK + h * HEAD_DIM)]),)

            def finish(scores, h=h):
                vcol = BV + h * HEAD_DIM
                o = _diff_head_out(*_diff_head_weights(scores, lam), [pc[0, :, _lane_group(vcol)]], g_ref[...],
                                   lam_init, vcol % LANES)
                out[0, :, _head_cols(YB, h)] = o.astype(BF16)

            units.append((start, finish))

        def start_c(_):
            return _row_stack([_dot_nt(_c_queries(pc, every, j, left), pc[0, :, _lane_group(CK + j * LANES)])
                               for j in range(C_HEADS // 2)])

        def finish_c(s):
            o = _softmax_av([(s, None, [pc[0, :, _lane_group(CV + j * LANES)] for j in range(C_HEADS // 2)])])
            for j in range(C_HEADS // 2):
                oj = o[2 * j * t:2 * (j + 1) * t]
                out[0, :, YC + j * LANES:YC + (j + 1) * LANES] = jnp.where(left, oj[0:t], oj[t:2 * t]).astype(BF16)

        return units + [(start_c, finish_c)]

    units = []
    for bi in range(pc_ref.shape[0]):
        units += units_of(pc_ref.at[pl.ds(bi, 1)], o_ref.at[pl.ds(bi, 1)])
    _pipelined(units, (0, ATTN_LOOKAHEAD))


def _ctx_attn(pc, sink_l, lamv, g, lam_init):
    bsz, n_ctx, _ = pc.shape
    return pl.pallas_call(
        functools.partial(_ctx_attn_kernel, lam_init=lam_init),
        grid=(bsz // CTX_BATCH,),
        in_specs=[
            pl.BlockSpec(memory_space=pltpu.SMEM),
            pl.BlockSpec((CTX_BATCH, n_ctx, PROJ_WIDTH), lambda b: (b, 0, 0)),
            pl.BlockSpec((4, B_QK_DIM), lambda b: (0, 0)),
            pl.BlockSpec((1, HEAD_DIM), lambda b: (0, 0)),
        ],
        out_specs=pl.BlockSpec((CTX_BATCH, n_ctx, MIX_WIDTH), lambda b: (b, 0, 0)),
        out_shape=jax.ShapeDtypeStruct((bsz, n_ctx, MIX_WIDTH), BF16),
        compiler_params=pltpu.CompilerParams(vmem_limit_bytes=VMEM_LIMIT),
        name="ctx_attn",
    )(sink_l, pc, lamv, g)


N_DR = 2 * NA_KH - 1
N_DC = 2 * NA_KW - 1


def _na_table_kernel(nb_ref, o_ref):
    base = (pl.program_id(0) * C_HEADS + pl.program_id(1)) * (N_DR * N_DC)
    wq = lax.broadcasted_iota(jnp.int32, (GRID_W, LANES), 0)
    wk = lax.broadcasted_iota(jnp.int32, (GRID_W, LANES), 1) % GRID_W
    dc = jnp.clip(wk - wq, -(NA_KW - 1), NA_KW - 1) + (NA_KW - 1)
    cs = jnp.clip(wq - NA_KW // 2, 0, GRID_W - NA_KW)
    valid = (wk >= cs) & (wk < cs + NA_KW)
    for dr in range(N_DR):
        acc = jnp.zeros((GRID_W, LANES), F32)
        for c in range(N_DC):
            acc = jnp.where(dc == c, nb_ref[base + dr * N_DC + c], acc)
        o_ref[0, 0, dr] = jnp.where(valid, acc * LOG2E, NEG_INF)


def _na_table(na_bias):
    return pl.pallas_call(
        _na_table_kernel,
        grid=(DEPTH, C_HEADS),
        in_specs=[pl.BlockSpec(memory_space=pltpu.SMEM)],
        out_specs=pl.BlockSpec((1, 1, N_DR, GRID_W, LANES), lambda l, h: (l, h, 0, 0, 0)),
        out_shape=jax.ShapeDtypeStruct((DEPTH, C_HEADS, N_DR, GRID_W, LANES), F32),
        name="na_table",
    )(na_bias.reshape(-1))


def _outffn_kernel(x_ref, y_ref, g1_ref, sh2_ref, sc2_ref, g2_ref, wo_ref, ln1g_ref, ln1b_ref, ln2g_ref, ln2b_ref,
                   wg_ref, wu_ref, wd_ref, o_ref, *, sub):
    def rows(u):
        return slice(u * sub, (u + 1) * sub)

    def out_proj(u, _):
        return _dot(y_ref[0, rows(u)], wo_ref[0])

    def norm_gate_up(u, y):
        xn = _layer_norm(ALPHA * x_ref[0, rows(u)] + g1_ref[0] * y, ln1g_ref[...], ln1b_ref[...])
        h = (xn * (1.0 + sc2_ref[0]) + sh2_ref[0]).astype(BF16)
        return xn, _dot(h, wg_ref[0]), _dot(h, wu_ref[0])

    def act_down(u, st):
        xn, gate, up = st
        return xn, _dot((_silu(gate) * up).astype(BF16), wd_ref[0])

    def norm_store(u, st):
        xn, ff = st
        o_ref[0, rows(u)] = _layer_norm(ALPHA * xn + g2_ref[0] * ff, ln2g_ref[...], ln2b_ref[...])

    _staged(x_ref.shape[1] // sub, [out_proj, norm_gate_up, act_down, norm_store])


def _outffn(xs, y, mod, layer, wo_b, ln1g, ln1b, ln2g, ln2b, wfi_b, wfo_b, *, ctx):
    bsz, seq, _ = xs.shape
    t = min(FFN_TILE, seq)
    row = (lambda b: CTX_ROW) if ctx else (lambda b: b)
    once = pl.Buffered(1)

    def mod_spec(k):
        return pl.BlockSpec((1, 1, D_MODEL), lambda b, i: (row(b), 0, k))

    vec = pl.BlockSpec((1, D_MODEL), lambda b, i: (0, 0))
    return pl.pallas_call(
        functools.partial(_outffn_kernel, sub=min(SUB_TILE, t // 2)),
        grid=(bsz, seq // t),
        in_specs=[
            pl.BlockSpec((1, t, D_MODEL), lambda b, i: (b, i, 0)),
            pl.BlockSpec((1, t, MIX_WIDTH), lambda b, i: (b, i, 0)),
            mod_spec(2), mod_spec(3), mod_spec(4), mod_spec(5),
            pl.BlockSpec((1, MIX_WIDTH, D_MODEL), lambda b, i: (layer, 0, 0), pipeline_mode=once),
            vec, vec, vec, vec,
            pl.BlockSpec((1, D_MODEL, D_FF), lambda b, i: (layer, 0, 0), pipeline_mode=once),
            pl.BlockSpec((1, D_MODEL, D_FF), lambda b, i: (layer, 0, 1), pipeline_mode=once),
            pl.BlockSpec((1, D_FF, D_MODEL), lambda b, i: (layer, 0, 0), pipeline_mode=once),
        ],
        out_specs=pl.BlockSpec((1, t, D_MODEL), lambda b, i: (b, i, 0)),
        out_shape=jax.ShapeDtypeStruct((bsz, seq, D_MODEL), F32),
        compiler_params=pltpu.CompilerParams(vmem_limit_bytes=VMEM_LIMIT),
        name="outffn_ctx" if ctx else "outffn",
    )(xs, y, mod, mod, mod, mod, wo_b, ln1g, ln1b, ln2g, ln2b, wfi_b, wfi_b, wfo_b)


def kernel(x, c, ctx, c_ctx, w_ada, b_ada, w_in, w_o, sink, lam_q1, lam_k1, lam_q2, lam_k2, subln_g, na_bias,
           ln1_g, ln1_b, w_ffn_in, w_ffn_out, ln2_g, ln2_b):
    bsz, seq, _ = x.shape
    assert x.shape == (bsz, seq, D_MODEL) and seq % Q_BLK == 0 and seq // Q_BLK >= 3
    assert bsz < CTX_ROW + 1 <= MOD_ROWS and bsz % CTX_BATCH == 0
    cc = jnp.zeros((MOD_ROWS, D_MODEL), F32).at[:bsz].set(c).at[CTX_ROW].set(c_ctx)
    mod_all = _ada(cc, w_ada, b_ada).reshape(DEPTH, MOD_ROWS, 1, N_MOD * D_MODEL)
    tables = _rope_tables(seq)
    na_tab = _na_table(na_bias)
    w_in_b, wo_b, wfi_b, wfo_b = (_cast_bf16(w) for w in (w_in, w_o, w_ffn_in, w_ffn_out))

    xs, cs = x, ctx
    for l in range(DEPTH):
        last = l == DEPTH - 1
        lam_init = 0.8 - 0.6 * math.exp(-0.3 * l)
        mod = mod_all[l]
        lamv = jnp.stack([lam_q1[l], lam_k1[l], lam_q2[l], lam_k2[l]])
        g = subln_g[l].reshape(1, HEAD_DIM)
        lnp = [v[l].reshape(1, D_MODEL) for v in (ln1_g, ln1_b, ln2_g, ln2_b)]

        p = _inproj(xs, mod, w_in_b, l, tables, ctx=False)
        pc = _inproj(cs, mod, w_in_b, l, None, ctx=True)
        y = _attn(p, pc, sink[l], lamv, g, na_tab[l], lam_init)
        xn = _outffn(xs, y, mod, l, wo_b, *lnp, wfi_b, wfo_b, ctx=False)
        if not last:
            yc = _ctx_attn(pc, sink[l], lamv, g, lam_init)
            cs = _outffn(cs, yc, mod, l, wo_b, *lnp, wfi_b, wfo_b, ctx=True)
        xs = xn
    return xs
```

```python
import functools
import math

import jax
import jax.numpy as jnp
import numpy as np
from jax import lax
from jax.experimental import pallas as pl
from jax.experimental.pallas import tpu as pltpu

F32 = jnp.float32
BF16 = jnp.bfloat16

D_MODEL = 1024
DEPTH = 2
GRID_W = 64
HEAD_DIM = 64
A_HEADS = 6
A_KV_HEADS = 2
A_REP = A_HEADS // A_KV_HEADS
WINDOW = 128
WIN_BLK = 128
B_HEADS = 4
B_QK_DIM = 32
C_HEADS = 6
NA_KH = 8
NA_KW = 16
N_MOD = 6
D_FF = 2816
MIX_WIDTH = (A_HEADS + B_HEADS + C_HEADS) * HEAD_DIM
ROPE_BASE = 10000.0
LN_EPS = 1e-5
NEG_INF = -1e30
LOG2E = 1.4426950408889634
QK_SCALE2_A = HEAD_DIM ** -0.5 * LOG2E
QK_SCALE2_B = B_QK_DIM ** -0.5 * LOG2E
ALPHA = (2.0 * DEPTH) ** 0.25

AQ = 0
AK = AQ + A_HEADS * HEAD_DIM
AV = AK + A_KV_HEADS * HEAD_DIM
BQ = AV + A_KV_HEADS * HEAD_DIM
BK = BQ + B_HEADS * 2 * B_QK_DIM
BV = BK + B_HEADS * 2 * B_QK_DIM
CQ = BV + B_HEADS * HEAD_DIM
CK = CQ + C_HEADS * HEAD_DIM
CV = CK + C_HEADS * HEAD_DIM
PROJ_WIDTH = CV + C_HEADS * HEAD_DIM

YA = 0
YB = YA + A_HEADS * HEAD_DIM
YC = YB + B_HEADS * HEAD_DIM

LANES = 128
MOD_ROWS = 16
CTX_ROW = 8
VMEM_LIMIT = 56 * 1024 * 1024

Q_BLK = 256
CTX_BATCH = 2
ATTN_LOOKAHEAD = 1
IN_TILE = 1024
FFN_TILE = 1024
SUB_TILE = 256
CAST_BLOCK_BYTES = 4 * 1024 * 1024


def _dot(a, b):
    return jnp.dot(a, b, preferred_element_type=F32)


def _dot_nt(a, b):
    return lax.dot_general(a, b, (((1,), (1,)), ((), ())), preferred_element_type=F32)


def _silu(v):
    return v / (1.0 + jnp.exp(-v))


def _layer_norm(v, g, b):
    mu = jnp.mean(v, axis=-1, keepdims=True)
    d = v - mu
    var = jnp.mean(d * d, axis=-1, keepdims=True)
    return d * lax.rsqrt(var + LN_EPS) * g + b


def _ada_kernel(c_ref, w_ref, b_ref, o_ref):
    s = _silu(c_ref[...])
    o_ref[0] = _dot(s.astype(BF16), w_ref[0].astype(BF16)) + b_ref[0]


def _ada(cc, w_ada, b_ada):
    tn = D_MODEL
    return pl.pallas_call(
        _ada_kernel,
        grid=(DEPTH, N_MOD * D_MODEL // tn),
        in_specs=[
            pl.BlockSpec((MOD_ROWS, D_MODEL), lambda l, j: (0, 0)),
            pl.BlockSpec((1, D_MODEL, tn), lambda l, j: (l, 0, j)),
            pl.BlockSpec((1, 1, tn), lambda l, j: (l, 0, j)),
        ],
        out_specs=pl.BlockSpec((1, MOD_ROWS, tn), lambda l, j: (l, 0, j)),
        out_shape=jax.ShapeDtypeStruct((DEPTH, MOD_ROWS, N_MOD * D_MODEL), F32),
        compiler_params=pltpu.CompilerParams(vmem_limit_bytes=VMEM_LIMIT),
        name="ada",
    )(cc, w_ada, b_ada.reshape(DEPTH, 1, N_MOD * D_MODEL))


def _cast_kernel(w_ref, o_ref):
    o_ref[...] = w_ref[...].astype(BF16)


def _cast_bf16(w):
    depth, rows, cols = w.shape
    blk = rows
    while blk * cols * 4 > CAST_BLOCK_BYTES and blk % 32 == 0:
        blk //= 2
    return pl.pallas_call(
        _cast_kernel,
        grid=(depth, rows // blk),
        in_specs=[pl.BlockSpec((1, blk, cols), lambda l, i: (l, i, 0))],
        out_specs=pl.BlockSpec((1, blk, cols), lambda l, i: (l, i, 0)),
        out_shape=jax.ShapeDtypeStruct(w.shape, BF16),
        name="cast_bf16",
    )(w)


def _rope_group(v, cos, sin, off):
    lane = lax.broadcasted_iota(jnp.int32, v.shape, 1)
    low = (lane % (2 * off)) < off
    partner = jnp.where(low, pltpu.roll(v, LANES - off, 1), pltpu.roll(v, off, 1))
    return v * cos + partner * sin


def _staged(n_units, stages):
    state = [None] * n_units
    for step in range(n_units + len(stages) - 1):
        for k, stage in enumerate(stages):
            u = step - k
            if 0 <= u < n_units:
                state[u] = stage(u, state[u])


def _lane_groups(lo, hi):
    return tuple(range(lo // LANES, hi // LANES))


_GROUP_PLAN = {}
for _g in _lane_groups(AQ, AK):
    _GROUP_PLAN[_g] = (0, HEAD_DIM // 4, QK_SCALE2_A)
for _g in _lane_groups(AK, AV):
    _GROUP_PLAN[_g] = (1, HEAD_DIM // 4, None)
for _g in _lane_groups(BQ, BK):
    _GROUP_PLAN[_g] = (2, B_QK_DIM // 4, QK_SCALE2_B)
for _g in _lane_groups(BK, BV):
    _GROUP_PLAN[_g] = (3, B_QK_DIM // 4, None)
for _g in _lane_groups(CQ, CK):
    _GROUP_PLAN[_g] = (None, None, QK_SCALE2_A)


def _inproj_kernel(x_ref, sh_ref, sc_ref, w_ref, *rest, rope, sub):
    if rope:
        tab_ref, o_ref = rest
    else:
        (o_ref,) = rest

    def rows(u):
        return slice(u * sub, (u + 1) * sub)

    def project(u, _):
        h = x_ref[0, rows(u)] * (1.0 + sc_ref[0]) + sh_ref[0]
        return _dot(h.astype(BF16), w_ref[0])

    def rotate_store(u, p):
        for g in range(PROJ_WIDTH // LANES):
            v = p[:, g * LANES:(g + 1) * LANES]
            pair, quarter, qscale = _GROUP_PLAN.get(g, (None, None, None))
            if rope and pair is not None:
                v = _rope_group(v, tab_ref[2 * pair, rows(u)], tab_ref[2 * pair + 1, rows(u)], quarter)
            elif qscale is not None:
                v = v * qscale
            o_ref[0, rows(u), g * LANES:(g + 1) * LANES] = v.astype(BF16)

    _staged(x_ref.shape[1] // sub, [project, rotate_store])


def _inproj(xs, mod, w_in_b, layer, tables, *, ctx):
    bsz, seq, _ = xs.shape
    t = min(IN_TILE, seq)
    row = (lambda b: CTX_ROW) if ctx else (lambda b: b)
    in_specs = [
        pl.BlockSpec((1, t, D_MODEL), lambda b, i: (b, i, 0)),
        pl.BlockSpec((1, 1, D_MODEL), lambda b, i: (row(b), 0, 0)),
        pl.BlockSpec((1, 1, D_MODEL), lambda b, i: (row(b), 0, 1)),
        pl.BlockSpec((1, D_MODEL, PROJ_WIDTH), lambda b, i: (layer, 0, 0), pipeline_mode=pl.Buffered(1)),
    ]
    args = [xs, mod, mod, w_in_b]
    if tables is not None:
        in_specs.append(pl.BlockSpec((tables.shape[0], t, LANES), lambda b, i: (0, i, 0)))
        args.append(tables)
    return pl.pallas_call(
        functools.partial(_inproj_kernel, rope=tables is not None, sub=min(SUB_TILE, t)),
        grid=(bsz, seq // t),
        in_specs=in_specs,
        out_specs=pl.BlockSpec((1, t, PROJ_WIDTH), lambda b, i: (b, i, 0)),
        out_shape=jax.ShapeDtypeStruct((bsz, seq, PROJ_WIDTH), BF16),
        compiler_params=pltpu.CompilerParams(vmem_limit_bytes=VMEM_LIMIT),
        name="inproj_ctx" if ctx else "inproj",
    )(*args)


def _rope_tables(seq):
    f32 = np.float32
    tpos = np.arange(seq, dtype=np.int32)
    rows = (tpos // GRID_W).astype(f32)[:, None]
    cols = (tpos % GRID_W).astype(f32)[:, None]
    lane = np.arange(LANES, dtype=np.int32)

    def table(head_dim):
        quarter = head_dim // 4
        inv = f32(ROPE_BASE) ** (-np.arange(quarter, dtype=f32) / f32(quarter))
        freq = inv[lane % quarter][None, :].astype(f32)
        use_cols = ((lane % head_dim) >= head_dim // 2)[None, :]
        ang = np.where(use_cols, cols * freq, rows * freq).astype(f32)
        sign = np.where((lane % (2 * quarter)) < quarter, f32(-1.0), f32(1.0))[None, :]
        return np.cos(ang).astype(f32), (np.sin(ang) * sign).astype(f32)

    cos_a, sin_a = table(HEAD_DIM)
    cos_b, sin_b = table(B_QK_DIM)
    return np.stack([cos_a * f32(QK_SCALE2_A), sin_a * f32(QK_SCALE2_A), cos_a, sin_a,
                     cos_b * f32(QK_SCALE2_B), sin_b * f32(QK_SCALE2_B), cos_b, sin_b]).astype(f32)


def _lam_value(lam_ref, lam_init):
    v = lam_ref[...]
    s1 = jnp.sum(v[0:1, :] * v[1:2, :], axis=-1, keepdims=True)
    s2 = jnp.sum(v[2:3, :] * v[3:4, :], axis=-1, keepdims=True)
    return jnp.exp(s1) - jnp.exp(s2) + lam_init


def _row_stack(parts):
    return jnp.concatenate(parts, axis=0)


def _dot_blocks(e, v):
    if not isinstance(v, (list, tuple)):
        return _dot(e, v)
    t = e.shape[0] // len(v)
    return _row_stack([_dot(e[k * t:(k + 1) * t], vk) for k, vk in enumerate(v)])


def _softmax_av(pieces, extra_logit=None):
    weights, den = _softmax_weights([(s, bias2) for s, bias2, _ in pieces], extra_logit)
    return _weighted_values(weights, [v for _, _, v in pieces], den)


def _softmax_weights(pieces, extra_logit=None):
    terms, m2 = [], None
    for s, bias2 in pieces:
        t = s if bias2 is None else s + bias2
        ms = jnp.max(t, axis=-1, keepdims=True)
        terms.append(t)
        m2 = ms if m2 is None else jnp.maximum(m2, ms)
    den = None
    if extra_logit is not None:
        m2 = jnp.maximum(m2, extra_logit * LOG2E)
        den = jnp.exp2(extra_logit * LOG2E - m2)
    weights = []
    for t in terms:
        e = jnp.exp2(t - m2)
        ls = jnp.sum(e, axis=-1, keepdims=True)
        den = ls if den is None else den + ls
        weights.append(e.astype(BF16))
    return weights, den


def _wide(v):
    if isinstance(v, (list, tuple)):
        return [_wide(vk) for vk in v]
    return jnp.concatenate([v, jnp.ones_like(v)], axis=1)


def _weighted_values(weights, values, den):
    out = None
    for e, v in zip(weights, values):
        o = _dot_blocks(e, _wide(v))[:, :LANES]
        out = o if out is None else out + o
    return out / den


def _lane_group(col):
    g0 = col // LANES * LANES
    return slice(g0, g0 + LANES)


def _placed(q, col):
    t, w = q.shape
    off = col % LANES
    parts = [jnp.zeros((t, off), q.dtype)] if off else []
    parts.append(q)
    if LANES - off - w:
        parts.append(jnp.zeros((t, LANES - off - w), q.dtype))
    return jnp.concatenate(parts, axis=-1)


def _pipelined(units, offsets):
    state = [None] * len(units)
    for step in range(len(units) + max(offsets)):
        for k, off in enumerate(offsets):
            u = step - off
            if 0 <= u < len(units):
                state[u] = units[u][k](state[u])


def _head_cols(base, h):
    return slice(base + h * HEAD_DIM, base + (h + 1) * HEAD_DIM)


def _diff_head_weights(scores, lam):
    t = scores[0].shape[0] // 2
    mx = None
    for s in scores:
        ms = jnp.max(s, axis=-1, keepdims=True)
        mx = ms if mx is None else jnp.maximum(mx, ms)
    es = [jnp.exp2(s - mx) for s in scores]
    den = None
    for e in es:
        ls = jnp.sum(e, axis=-1, keepdims=True)
        den = ls if den is None else den + ls
    w = lam * den[0:t] / den[t:2 * t]
    return [(e[0:t] - w * e[t:2 * t]).astype(BF16) for e in es], den[0:t]


def _diff_head_out(weights, den, v_pieces, g, lam_init, v_off):
    o = _weighted_values(weights, v_pieces, den)[:, v_off:v_off + HEAD_DIM]
    ms = jnp.mean(o * o, axis=-1, keepdims=True)
    return o * lax.rsqrt(ms + LN_EPS) * g * (1.0 - lam_init)


def _a_queries(ref, rows):
    return _row_stack([_placed(ref[0, rows, _head_cols(AQ, h)], AK + (h // A_REP) * HEAD_DIM) for h in range(A_HEADS)])


def _a_sink(sink_ref, t):
    return _row_stack([jnp.full((t, 1), sink_ref[h], F32) for h in range(A_HEADS)])


def _a_store(o, o_ref, rows, left):
    t = o.shape[0] // A_HEADS
    for j in range(A_HEADS // 2):
        halves = []
        for h in (2 * j, 2 * j + 1):
            oh = o[h * t:(h + 1) * t]
            if h // A_REP != h % 2:
                oh = pltpu.roll(oh, HEAD_DIM, 1)
            halves.append(oh)
        o_ref[0, rows, YA + j * LANES:YA + (j + 1) * LANES] = jnp.where(left, halves[0], halves[1]).astype(BF16)


def _b_queries(ref, rows, h, lane):
    qcol = BQ + h * 2 * B_QK_DIM
    qg = ref[0, rows, _lane_group(qcol)]
    zero = jnp.zeros_like(qg)
    offs = [qcol % LANES + m * B_QK_DIM for m in range(2)]
    return _row_stack([jnp.where((lane >= off) & (lane < off + B_QK_DIM), qg, zero) for off in offs])


def _c_queries(ref, rows, j, left):
    qg = ref[0, rows, _lane_group(CQ + j * LANES)]
    zero = jnp.zeros_like(qg)
    return _row_stack([jnp.where(left, qg, zero), jnp.where(left, zero, qg)])


def _attn_kernel(sink_ref, p_ref, pc_ref, lam_ref, g_ref, tb_ref, o_ref, *, lam_init, seq):
    n = pl.program_id(1)
    units_a, units_b = [], []
    lane = lax.broadcasted_iota(jnp.int32, (1, LANES), 1)
    left = lane < HEAD_DIM

    nb = seq // WIN_BLK
    for sub in range(Q_BLK // WIN_BLK):
        blk = n * (Q_BLK // WIN_BLK) + sub
        q0 = pl.multiple_of(blk * WIN_BLK, WIN_BLK)
        ws = pl.multiple_of(jnp.clip(blk - 1, 0, nb - 3) * WIN_BLK, WIN_BLK)
        qpos = q0 + lax.broadcasted_iota(jnp.int32, (A_HEADS * WIN_BLK, 3 * WIN_BLK), 0) % WIN_BLK
        kpos = ws + lax.broadcasted_iota(jnp.int32, (A_HEADS * WIN_BLK, 3 * WIN_BLK), 1)
        valid = jnp.abs(qpos - kpos) <= WINDOW
        rows = slice(sub * WIN_BLK, (sub + 1) * WIN_BLK)

        def start(_, q0=q0, ws=ws, valid=valid):
            q = _a_queries(p_ref, pl.ds(q0, WIN_BLK))
            kw = p_ref[0, pl.ds(ws, 3 * WIN_BLK), _lane_group(AK)]
            kc = pc_ref[0, :, _lane_group(AK)]
            return jnp.where(valid, _dot_nt(q, kw), NEG_INF), _dot_nt(q, kc)

        def soft(scores):
            return _softmax_weights([(scores[0], None), (scores[1], None)], extra_logit=_a_sink(sink_ref, WIN_BLK))

        def finish(weights_den, ws=ws, rows=rows):
            vw = p_ref[0, pl.ds(ws, 3 * WIN_BLK), _lane_group(AV)]
            vc = pc_ref[0, :, _lane_group(AV)]
            _a_store(_weighted_values(weights_den[0], [vw, vc], weights_den[1]), o_ref, rows, left)

        units_a.append((start, soft, finish))

    lam = _lam_value(lam_ref, lam_init)
    qb = pl.multiple_of(n * Q_BLK, Q_BLK)
    for h in range(B_HEADS):
        def start(_, h=h):
            q = _b_queries(p_ref, pl.ds(qb, Q_BLK), h, lane)
            kcol = BK + h * 2 * B_QK_DIM
            k_lat = p_ref[0, :, _lane_group(kcol)]
            k_ctx = pc_ref[0, :, _lane_group(kcol)]
            return _dot_nt(q, k_lat), _dot_nt(q, k_ctx)

        def soft(scores):
            return _diff_head_weights(scores, lam)

        def finish(weights_den, h=h):
            vcol = BV + h * HEAD_DIM
            v_lat = p_ref[0, :, _lane_group(vcol)]
            v_ctx = pc_ref[0, :, _lane_group(vcol)]
            o = _diff_head_out(*weights_den, [v_lat, v_ctx], g_ref[...], lam_init, vcol % LANES)
            o_ref[0, :, _head_cols(YB, h)] = o.astype(BF16)

        units_b.append((start, soft, finish))

    n_rows = seq // GRID_W
    blocks = []
    for i in range(Q_BLK // GRID_W):
        r = n * (Q_BLK // GRID_W) + i
        rs = jnp.clip(r - NA_KH // 2, 0, n_rows - NA_KH)
        dr0 = (NA_KH - 1) - (r - rs)
        qr = pl.multiple_of(r * GRID_W, GRID_W)
        kr = pl.multiple_of(rs * GRID_W, GRID_W)
        blocks += [(i, j, qr, kr, dr0) for j in range(C_HEADS // 2)]

    def start_c(_):
        s_w, s_c = [], []
        for _, j, qr, kr, _ in blocks:
            q = _c_queries(p_ref, pl.ds(qr, GRID_W), j, left)
            s_w.append(_dot_nt(q, p_ref[0, pl.ds(kr, NA_KH * GRID_W), _lane_group(CK + j * LANES)]))
            s_c.append(_dot_nt(q, pc_ref[0, :, _lane_group(CK + j * LANES)]))
        return _row_stack(s_w), _row_stack(s_c)

    def soft_c(scores):
        bias2 = _row_stack([
            jnp.concatenate([jnp.where(left, tb_ref[h, dr0 + kh], tb_ref[h, dr0 + kh + 1])
                             for kh in range(0, NA_KH, 2)], axis=-1)
            for _, j, _, _, dr0 in blocks for h in (2 * j, 2 * j + 1)])
        return _softmax_weights([(scores[0], bias2), (scores[1], None)])

    def finish_c(weights_den):
        vw = [p_ref[0, pl.ds(kr, NA_KH * GRID_W), _lane_group(CV + j * LANES)] for _, j, _, kr, _ in blocks]
        vc = [pc_ref[0, :, _lane_group(CV + j * LANES)] for _, j, _, _, _ in blocks]
        o = _weighted_values(weights_den[0], [vw, vc], weights_den[1])
        for k, (i, j, _, _, _) in enumerate(blocks):
            ok = o[2 * k * GRID_W:2 * (k + 1) * GRID_W]
            o_ref[0, i * GRID_W:(i + 1) * GRID_W, YC + j * LANES:YC + (j + 1) * LANES] = (
                jnp.where(left, ok[0:GRID_W], ok[GRID_W:2 * GRID_W]).astype(BF16))

    _pipelined([(start_c, soft_c, finish_c)] + units_a + units_b, (0, ATTN_LOOKAHEAD, ATTN_LOOKAHEAD))


def _attn(p, pc, sink_l, lamv, g, tb, lam_init):
    bsz, seq, _ = p.shape
    n_ctx = pc.shape[1]
    return pl.pallas_call(
        functools.partial(_attn_kernel, lam_init=lam_init, seq=seq),
        grid=(bsz, seq // Q_BLK),
        in_specs=[
            pl.BlockSpec(memory_space=pltpu.SMEM),
            pl.BlockSpec((1, seq, PROJ_WIDTH), lambda b, n: (b, 0, 0)),
            pl.BlockSpec((1, n_ctx, PROJ_WIDTH), lambda b, n: (b, 0, 0)),
            pl.BlockSpec((4, B_QK_DIM), lambda b, n: (0, 0)),
            pl.BlockSpec((1, HEAD_DIM), lambda b, n: (0, 0)),
            pl.BlockSpec(tb.shape, lambda b, n: (0, 0, 0, 0)),
        ],
        out_specs=pl.BlockSpec((1, Q_BLK, MIX_WIDTH), lambda b, n: (b, n, 0)),
        out_shape=jax.ShapeDtypeStruct((bsz, seq, MIX_WIDTH), BF16),
        compiler_params=pltpu.CompilerParams(vmem_limit_bytes=VMEM_LIMIT),
        name="attn",
    )(sink_l, p, pc, lamv, g, tb)


def _ctx_attn_kernel(sink_ref, pc_ref, lam_ref, g_ref, o_ref, *, lam_init):
    t = pc_ref.shape[1]
    every = slice(0, t)
    lane = lax.broadcasted_iota(jnp.int32, (1, LANES), 1)
    left = lane < HEAD_DIM
    lam = _lam_value(lam_ref, lam_init)

    def units_of(pc, out):
        def start_a(_):
            return _dot_nt(_a_queries(pc, every), pc[0, :, _lane_group(AK)])

        def finish_a(s):
            o = _softmax_av([(s, None, pc[0, :, _lane_group(AV)])], extra_logit=_a_sink(sink_ref, t))
            _a_store(o, out, every, left)

        units = [(start_a, finish_a)]
        for h in range(B_HEADS):
            def start(_, h=h):
                return (_dot_nt(_b_queries(pc, every, h, lane), pc[0, :, _lane_group(BK + h * HEAD_DIM)]),)

            def finish(scores, h=h):
                vcol = BV + h * HEAD_DIM
                o = _diff_head_out(*_diff_head_weights(scores, lam), [pc[0, :, _lane_group(vcol)]], g_ref[...],
                                   lam_init, vcol % LANES)
                out[0, :, _head_cols(YB, h)] = o.astype(BF16)

            units.append((start, finish))

        def start_c(_):
            return _row_stack([_dot_nt(_c_queries(pc, every, j, left), pc[0, :, _lane_group(CK + j * LANES)])
                               for j in range(C_HEADS // 2)])

        def finish_c(s):
            o = _softmax_av([(s, None, [pc[0, :, _lane_group(CV + j * LANES)] for j in range(C_HEADS // 2)])])
            for j in range(C_HEADS // 2):
                oj = o[2 * j * t:2 * (j + 1) * t]
                out[0, :, YC + j * LANES:YC + (j + 1) * LANES] = jnp.where(left, oj[0:t], oj[t:2 * t]).astype(BF16)

        return units + [(start_c, finish_c)]

    units = []
    for bi in range(pc_ref.shape[0]):
        units += units_of(pc_ref.at[pl.ds(bi, 1)], o_ref.at[pl.ds(bi, 1)])
    _pipelined(units, (0, ATTN_LOOKAHEAD))


def _ctx_attn(pc, sink_l, lamv, g, lam_init):
    bsz, n_ctx, _ = pc.shape
    return pl.pallas_call(
        functools.partial(_ctx_attn_kernel, lam_init=lam_init),
        grid=(bsz // CTX_BATCH,),
        in_specs=[
            pl.BlockSpec(memory_space=pltpu.SMEM),
            pl.BlockSpec((CTX_BATCH, n_ctx, PROJ_WIDTH), lambda b: (b, 0, 0)),
            pl.BlockSpec((4, B_QK_DIM), lambda b: (0, 0)),
            pl.BlockSpec((1, HEAD_DIM), lambda b: (0, 0)),
        ],
        out_specs=pl.BlockSpec((CTX_BATCH, n_ctx, MIX_WIDTH), lambda b: (b, 0, 0)),
        out_shape=jax.ShapeDtypeStruct((bsz, n_ctx, MIX_WIDTH), BF16),
        compiler_params=pltpu.CompilerParams(vmem_limit_bytes=VMEM_LIMIT),
        name="ctx_attn",
    )(sink_l, pc, lamv, g)


N_DR = 2 * NA_KH - 1
N_DC = 2 * NA_KW - 1


def _na_table_kernel(nb_ref, o_ref):
    base = (pl.program_id(0) * C_HEADS + pl.program_id(1)) * (N_DR * N_DC)
    wq = lax.broadcasted_iota(jnp.int32, (GRID_W, LANES), 0)
    wk = lax.broadcasted_iota(jnp.int32, (GRID_W, LANES), 1) % GRID_W
    dc = jnp.clip(wk - wq, -(NA_KW - 1), NA_KW - 1) + (NA_KW - 1)
    cs = jnp.clip(wq - NA_KW // 2, 0, GRID_W - NA_KW)
    valid = (wk >= cs) & (wk < cs + NA_KW)
    for dr in range(N_DR):
        acc = jnp.zeros((GRID_W, LANES), F32)
        for c in range(N_DC):
            acc = jnp.where(dc == c, nb_ref[base + dr * N_DC + c], acc)
        o_ref[0, 0, dr] = jnp.where(valid, acc * LOG2E, NEG_INF)


def _na_table(na_bias):
    return pl.pallas_call(
        _na_table_kernel,
        grid=(DEPTH, C_HEADS),
        in_specs=[pl.BlockSpec(memory_space=pltpu.SMEM)],
        out_specs=pl.BlockSpec((1, 1, N_DR, GRID_W, LANES), lambda l, h: (l, h, 0, 0, 0)),
        out_shape=jax.ShapeDtypeStruct((DEPTH, C_HEADS, N_DR, GRID_W, LANES), F32),
        name="na_table",
    )(na_bias.reshape(-1))


def _outffn_kernel(x_ref, y_ref, g1_ref, sh2_ref, sc2_ref, g2_ref, wo_ref, ln1g_ref, ln1b_ref, ln2g_ref, ln2b_ref,
                   wg_ref, wu_ref, wd_ref, o_ref, *, sub):
    def rows(u):
        return slice(u * sub, (u + 1) * sub)

    def out_proj(u, _):
        return _dot(y_ref[0, rows(u)], wo_ref[0])

    def norm_gate_up(u, y):
        xn = _layer_norm(ALPHA * x_ref[0, rows(u)] + g1_ref[0] * y, ln1g_ref[...], ln1b_ref[...])
        h = (xn * (1.0 + sc2_ref[0]) + sh2_ref[0]).astype(BF16)
        return xn, _dot(h, wg_ref[0]), _dot(h, wu_ref[0])

    def act_down(u, st):
        xn, gate, up = st
        return xn, _dot((_silu(gate) * up).astype(BF16), wd_ref[0])

    def norm_store(u, st):
        xn, ff = st
        o_ref[0, rows(u)] = _layer_norm(ALPHA * xn + g2_ref[0] * ff, ln2g_ref[...], ln2b_ref[...])

    _staged(x_ref.shape[1] // sub, [out_proj, norm_gate_up, act_down, norm_store])


def _outffn(xs, y, mod, layer, wo_b, ln1g, ln1b, ln2g, ln2b, wfi_b, wfo_b, *, ctx):
    bsz, seq, _ = xs.shape
    t = min(FFN_TILE, seq)
    row = (lambda b: CTX_ROW) if ctx else (lambda b: b)
    once = pl.Buffered(1)

    def mod_spec(k):
        return pl.BlockSpec((1, 1, D_MODEL), lambda b, i: (row(b), 0, k))

    vec = pl.BlockSpec((1, D_MODEL), lambda b, i: (0, 0))
    return pl.pallas_call(
        functools.partial(_outffn_kernel, sub=min(SUB_TILE, t // 2)),
        grid=(bsz, seq // t),
        in_specs=[
            pl.BlockSpec((1, t, D_MODEL), lambda b, i: (b, i, 0)),
            pl.BlockSpec((1, t, MIX_WIDTH), lambda b, i: (b, i, 0)),
            mod_spec(2), mod_spec(3), mod_spec(4), mod_spec(5),
            pl.BlockSpec((1, MIX_WIDTH, D_MODEL), lambda b, i: (layer, 0, 0), pipeline_mode=once),
            vec, vec, vec, vec,
            pl.BlockSpec((1, D_MODEL, D_FF), lambda b, i: (layer, 0, 0), pipeline_mode=once),
            pl.BlockSpec((1, D_MODEL, D_FF), lambda b, i: (layer, 0, 1), pipeline_mode=once),
            pl.BlockSpec((1, D_FF, D_MODEL), lambda b, i: (layer, 0, 0), pipeline_mode=once),
        ],
        out_specs=pl.BlockSpec((1, t, D_MODEL), lambda b, i: (b, i, 0)),
        out_shape=jax.ShapeDtypeStruct((bsz, seq, D_MODEL), F32),
        compiler_params=pltpu.CompilerParams(vmem_limit_bytes=VMEM_LIMIT),
        name="outffn_ctx" if ctx else "outffn",
    )(xs, y, mod, mod, mod, mod, wo_b, ln1g, ln1b, ln2g, ln2b, wfi_b, wfi_b, wfo_b)


def kernel(x, c, ctx, c_ctx, w_ada, b_ada, w_in, w_o, sink, lam_q1, lam_k1, lam_q2, lam_k2, subln_g, na_bias,
           ln1_g, ln1_b, w_ffn_in, w_ffn_out, ln2_g, ln2_b):
    bsz, seq, _ = x.shape
    assert x.shape == (bsz, seq, D_MODEL) and seq % Q_BLK == 0 and seq // Q_BLK >= 3
    assert bsz < CTX_ROW + 1 <= MOD_ROWS and bsz % CTX_BATCH == 0
    cc = jnp.zeros((MOD_ROWS, D_MODEL), F32).at[:bsz].set(c).at[CTX_ROW].set(c_ctx)
    mod_all = _ada(cc, w_ada, b_ada).reshape(DEPTH, MOD_ROWS, 1, N_MOD * D_MODEL)
    tables = _rope_tables(seq)
    na_tab = _na_table(na_bias)
    w_in_b, wo_b, wfi_b, wfo_b = (_cast_bf16(w) for w in (w_in, w_o, w_ffn_in, w_ffn_out))

    xs, cs = x, ctx
    for l in range(DEPTH):
        last = l == DEPTH - 1
        lam_init = 0.8 - 0.6 * math.exp(-0.3 * l)
        mod = mod_all[l]
        lamv = jnp.stack([lam_q1[l], lam_k1[l], lam_q2[l], lam_k2[l]])
        g = subln_g[l].reshape(1, HEAD_DIM)
        lnp = [v[l].reshape(1, D_MODEL) for v in (ln1_g, ln1_b, ln2_g, ln2_b)]

        p = _inproj(xs, mod, w_in_b, l, tables, ctx=False)
        pc = _inproj(cs, mod, w_in_b, l, None, ctx=True)
        y = _attn(p, pc, sink[l], lamv, g, na_tab[l], lam_init)
        xn = _outffn(xs, y, mod, l, wo_b, *lnp, wfi_b, wfo_b, ctx=False)
        if not last:
            yc = _ctx_attn(pc, sink[l], lamv, g, lam_init)
            cs = _outffn(cs, yc, mod, l, wo_b, *lnp, wfi_b, wfo_b, ctx=True)
        xs = xn
    return xs
```

```python
import functools
import math

import jax
import jax.numpy as jnp
import numpy as np
from jax import lax
from jax.experimental import pallas as pl
from jax.experimental.pallas import tpu as pltpu

F32 = jnp.float32
BF16 = jnp.bfloat16

D_MODEL = 1024
DEPTH = 2
GRID_W = 64
HEAD_DIM = 64
A_HEADS = 6
A_KV_HEADS = 2
A_REP = A_HEADS // A_KV_HEADS
WINDOW = 128
WIN_BLK = 128
B_HEADS = 4
B_QK_DIM = 32
C_HEADS = 6
NA_KH = 8
NA_KW = 16
N_MOD = 6
D_FF = 2816
MIX_WIDTH = (A_HEADS + B_HEADS + C_HEADS) * HEAD_DIM
ROPE_BASE = 10000.0
LN_EPS = 1e-5
NEG_INF = -1e30
LOG2E = 1.4426950408889634
QK_SCALE2_A = HEAD_DIM ** -0.5 * LOG2E
QK_SCALE2_B = B_QK_DIM ** -0.5 * LOG2E
ALPHA = (2.0 * DEPTH) ** 0.25

AQ = 0
AK = AQ + A_HEADS * HEAD_DIM
AV = AK + A_KV_HEADS * HEAD_DIM
BQ = AV + A_KV_HEADS * HEAD_DIM
BK = BQ + B_HEADS * 2 * B_QK_DIM
BV = BK + B_HEADS * 2 * B_QK_DIM
CQ = BV + B_HEADS * HEAD_DIM
CK = CQ + C_HEADS * HEAD_DIM
CV = CK + C_HEADS * HEAD_DIM
PROJ_WIDTH = CV + C_HEADS * HEAD_DIM

YA = 0
YB = YA + A_HEADS * HEAD_DIM
YC = YB + B_HEADS * HEAD_DIM

LANES = 128
MOD_ROWS = 16
CTX_ROW = 8
VMEM_LIMIT = 56 * 1024 * 1024

Q_BLK = 256
CTX_BATCH = 2
ATTN_LOOKAHEAD = 1
IN_TILE = 1024
FFN_TILE = 1024
SUB_TILE = 256
CAST_BLOCK_BYTES = 4 * 1024 * 1024


def _dot(a, b):
    return jnp.dot(a, b, preferred_element_type=F32)


def _dot_nt(a, b):
    return lax.dot_general(a, b, (((1,), (1,)), ((), ())), preferred_element_type=F32)


def _silu(v):
    return v / (1.0 + jnp.exp(-v))


def _layer_norm(v, g, b):
    mu = jnp.mean(v, axis=-1, keepdims=True)
    d = v - mu
    var = jnp.mean(d * d, axis=-1, keepdims=True)
    return d * lax.rsqrt(var + LN_EPS) * g + b


def _ada_kernel(c_ref, w_ref, b_ref, o_ref):
    s = _silu(c_ref[...])
    o_ref[0] = _dot(s.astype(BF16), w_ref[0].astype(BF16)) + b_ref[0]


def _ada(cc, w_ada, b_ada):
    tn = D_MODEL
    return pl.pallas_call(
        _ada_kernel,
        grid=(DEPTH, N_MOD * D_MODEL // tn),
        in_specs=[
            pl.BlockSpec((MOD_ROWS, D_MODEL), lambda l, j: (0, 0)),
            pl.BlockSpec((1, D_MODEL, tn), lambda l, j: (l, 0, j)),
            pl.BlockSpec((1, 1, tn), lambda l, j: (l, 0, j)),
        ],
        out_specs=pl.BlockSpec((1, MOD_ROWS, tn), lambda l, j: (l, 0, j)),
        out_shape=jax.ShapeDtypeStruct((DEPTH, MOD_ROWS, N_MOD * D_MODEL), F32),
        compiler_params=pltpu.CompilerParams(vmem_limit_bytes=VMEM_LIMIT),
        name="ada",
    )(cc, w_ada, b_ada.reshape(DEPTH, 1, N_MOD * D_MODEL))


def _cast_kernel(w_ref, o_ref):
    o_ref[...] = w_ref[...].astype(BF16)


def _cast_bf16(w):
    depth, rows, cols = w.shape
    blk = rows
    while blk * cols * 4 > CAST_BLOCK_BYTES and blk % 32 == 0:
        blk //= 2
    return pl.pallas_call(
        _cast_kernel,
        grid=(depth, rows // blk),
        in_specs=[pl.BlockSpec((1, blk, cols), lambda l, i: (l, i, 0))],
        out_specs=pl.BlockSpec((1, blk, cols), lambda l, i: (l, i, 0)),
        out_shape=jax.ShapeDtypeStruct(w.shape, BF16),
        name="cast_bf16",
    )(w)


def _rope_group(v, cos, sin, off):
    lane = lax.broadcasted_iota(jnp.int32, v.shape, 1)
    low = (lane % (2 * off)) < off
    partner = jnp.where(low, pltpu.roll(v, LANES - off, 1), pltpu.roll(v, off, 1))
    return v * cos + partner * sin


def _staged(n_units, stages):
    state = [None] * n_units
    for step in range(n_units + len(stages) - 1):
        for k, stage in enumerate(stages):
            u = step - k
            if 0 <= u < n_units:
                state[u] = stage(u, state[u])


def _lane_groups(lo, hi):
    return tuple(range(lo // LANES, hi // LANES))


_GROUP_PLAN = {}
for _g in _lane_groups(AQ, AK):
    _GROUP_PLAN[_g] = (0, HEAD_DIM // 4, QK_SCALE2_A)
for _g in _lane_groups(AK, AV):
    _GROUP_PLAN[_g] = (1, HEAD_DIM // 4, None)
for _g in _lane_groups(BQ, BK):
    _GROUP_PLAN[_g] = (2, B_QK_DIM // 4, QK_SCALE2_B)
for _g in _lane_groups(BK, BV):
    _GROUP_PLAN[_g] = (3, B_QK_DIM // 4, None)
for _g in _lane_groups(CQ, CK):
    _GROUP_PLAN[_g] = (None, None, QK_SCALE2_A)


def _inproj_kernel(x_ref, sh_ref, sc_ref, w_ref, *rest, rope, sub):
    if rope:
        tab_ref, o_ref = rest
    else:
        (o_ref,) = rest

    def rows(u):
        return slice(u * sub, (u + 1) * sub)

    def project(u, _):
        h = x_ref[0, rows(u)] * (1.0 + sc_ref[0]) + sh_ref[0]
        return _dot(h.astype(BF16), w_ref[0])

    def rotate_store(u, p):
        for g in range(PROJ_WIDTH // LANES):
            v = p[:, g * LANES:(g + 1) * LANES]
            pair, quarter, qscale = _GROUP_PLAN.get(g, (None, None, None))
            if rope and pair is not None:
                v = _rope_group(v, tab_ref[2 * pair, rows(u)], tab_ref[2 * pair + 1, rows(u)], quarter)
            elif qscale is not None:
                v = v * qscale
            o_ref[0, rows(u), g * LANES:(g + 1) * LANES] = v.astype(BF16)

    _staged(x_ref.shape[1] // sub, [project, rotate_store])


def _inproj(xs, mod, w_in_b, layer, tables, *, ctx):
    bsz, seq, _ = xs.shape
    t = min(IN_TILE, seq)
    row = (lambda b: CTX_ROW) if ctx else (lambda b: b)
    in_specs = [
        pl.BlockSpec((1, t, D_MODEL), lambda b, i: (b, i, 0)),
        pl.BlockSpec((1, 1, D_MODEL), lambda b, i: (row(b), 0, 0)),
        pl.BlockSpec((1, 1, D_MODEL), lambda b, i: (row(b), 0, 1)),
        pl.BlockSpec((1, D_MODEL, PROJ_WIDTH), lambda b, i: (layer, 0, 0), pipeline_mode=pl.Buffered(1)),
    ]
    args = [xs, mod, mod, w_in_b]
    if tables is not None:
        in_specs.append(pl.BlockSpec((tables.shape[0], t, LANES), lambda b, i: (0, i, 0)))
        args.append(tables)
    return pl.pallas_call(
        functools.partial(_inproj_kernel, rope=tables is not None, sub=min(SUB_TILE, t)),
        grid=(bsz, seq // t),
        in_specs=in_specs,
        out_specs=pl.BlockSpec((1, t, PROJ_WIDTH), lambda b, i: (b, i, 0)),
        out_shape=jax.ShapeDtypeStruct((bsz, seq, PROJ_WIDTH), BF16),
        compiler_params=pltpu.CompilerParams(vmem_limit_bytes=VMEM_LIMIT),
        name="inproj_ctx" if ctx else "inproj",
    )(*args)


def _rope_tables(seq):
    f32 = np.float32
    tpos = np.arange(seq, dtype=np.int32)
    rows = (tpos // GRID_W).astype(f32)[:, None]
    cols = (tpos % GRID_W).astype(f32)[:, None]
    lane = np.arange(LANES, dtype=np.int32)

    def table(head_dim):
        quarter = head_dim // 4
        inv = f32(ROPE_BASE) ** (-np.arange(quarter, dtype=f32) / f32(quarter))
        freq = inv[lane % quarter][None, :].astype(f32)
        use_cols = ((lane % head_dim) >= head_dim // 2)[None, :]
        ang = np.where(use_cols, cols * freq, rows * freq).astype(f32)
        sign = np.where((lane % (2 * quarter)) < quarter, f32(-1.0), f32(1.0))[None, :]
        return np.cos(ang).astype(f32), (np.sin(ang) * sign).astype(f32)

    cos_a, sin_a = table(HEAD_DIM)
    cos_b, sin_b = table(B_QK_DIM)
    return np.stack([cos_a * f32(QK_SCALE2_A), sin_a * f32(QK_SCALE2_A), cos_a, sin_a,
                     cos_b * f32(QK_SCALE2_B), sin_b * f32(QK_SCALE2_B), cos_b, sin_b]).astype(f32)


def _lam_value(lam_ref, lam_init):
    v = lam_ref[...]
    s1 = jnp.sum(v[0:1, :] * v[1:2, :], axis=-1, keepdims=True)
    s2 = jnp.sum(v[2:3, :] * v[3:4, :], axis=-1, keepdims=True)
    return jnp.exp(s1) - jnp.exp(s2) + lam_init


def _row_stack(parts):
    return jnp.concatenate(parts, axis=0)


def _dot_blocks(e, v):
    if not isinstance(v, (list, tuple)):
        return _dot(e, v)
    t = e.shape[0] // len(v)
    return _row_stack([_dot(e[k * t:(k + 1) * t], vk) for k, vk in enumerate(v)])


def _softmax_av(pieces, extra_logit=None):
    weights, den = _softmax_weights([(s, bias2) for s, bias2, _ in pieces], extra_logit)
    return _weighted_values(weights, [v for _, _, v in pieces], den)


def _softmax_weights(pieces, extra_logit=None):
    terms, m2 = [], None
    for s, bias2 in pieces:
        t = s if bias2 is None else s + bias2
        ms = jnp.max(t, axis=-1, keepdims=True)
        terms.append(t)
        m2 = ms if m2 is None else jnp.maximum(m2, ms)
    den = None
    if extra_logit is not None:
        m2 = jnp.maximum(m2, extra_logit * LOG2E)
        den = jnp.exp2(extra_logit * LOG2E - m2)
    weights = []
    for t in terms:
        e = jnp.exp2(t - m2)
        ls = jnp.sum(e, axis=-1, keepdims=True)
        den = ls if den is None else den + ls
        weights.append(e.astype(BF16))
    return weights, den


def _weighted_values(weights, values, den):
    out = None
    for e, v in zip(weights, values):
        o = _dot_blocks(e, v)
        out = o if out is None else out + o
    return out / den


def _lane_group(col):
    g0 = col // LANES * LANES
    return slice(g0, g0 + LANES)


def _placed(q, col):
    t, w = q.shape
    off = col % LANES
    parts = [jnp.zeros((t, off), q.dtype)] if off else []
    parts.append(q)
    if LANES - off - w:
        parts.append(jnp.zeros((t, LANES - off - w), q.dtype))
    return jnp.concatenate(parts, axis=-1)


def _pipelined(units, offsets):
    state = [None] * len(units)
    for step in range(len(units) + max(offsets)):
        for k, off in enumerate(offsets):
            u = step - off
            if 0 <= u < len(units):
                state[u] = units[u][k](state[u])


def _head_cols(base, h):
    return slice(base + h * HEAD_DIM, base + (h + 1) * HEAD_DIM)


def _diff_head_weights(scores, lam):
    t = scores[0].shape[0] // 2
    mx = None
    for s in scores:
        ms = jnp.max(s, axis=-1, keepdims=True)
        mx = ms if mx is None else jnp.maximum(mx, ms)
    es = [jnp.exp2(s - mx) for s in scores]
    den = None
    for e in es:
        ls = jnp.sum(e, axis=-1, keepdims=True)
        den = ls if den is None else den + ls
    w = lam * den[0:t] / den[t:2 * t]
    return [(e[0:t] - w * e[t:2 * t]).astype(BF16) for e in es], den[0:t]


def _diff_head_out(weights, den, v_pieces, g, lam_init, v_off):
    o = _weighted_values(weights, v_pieces, den)[:, v_off:v_off + HEAD_DIM]
    ms = jnp.mean(o * o, axis=-1, keepdims=True)
    return o * lax.rsqrt(ms + LN_EPS) * g * (1.0 - lam_init)


def _a_queries(ref, rows):
    return _row_stack([_placed(ref[0, rows, _head_cols(AQ, h)], AK + (h // A_REP) * HEAD_DIM) for h in range(A_HEADS)])


def _a_sink(sink_ref, t):
    return _row_stack([jnp.full((t, 1), sink_ref[h], F32) for h in range(A_HEADS)])


def _a_store(o, o_ref, rows, left):
    t = o.shape[0] // A_HEADS
    for j in range(A_HEADS // 2):
        halves = []
        for h in (2 * j, 2 * j + 1):
            oh = o[h * t:(h + 1) * t]
            if h // A_REP != h % 2:
                oh = pltpu.roll(oh, HEAD_DIM, 1)
            halves.append(oh)
        o_ref[0, rows, YA + j * LANES:YA + (j + 1) * LANES] = jnp.where(left, halves[0], halves[1]).astype(BF16)


def _b_queries(ref, rows, h, lane):
    qcol = BQ + h * 2 * B_QK_DIM
    qg = ref[0, rows, _lane_group(qcol)]
    zero = jnp.zeros_like(qg)
    offs = [qcol % LANES + m * B_QK_DIM for m in range(2)]
    return _row_stack([jnp.where((lane >= off) & (lane < off + B_QK_DIM), qg, zero) for off in offs])


def _c_queries(ref, rows, j, left):
    qg = ref[0, rows, _lane_group(CQ + j * LANES)]
    zero = jnp.zeros_like(qg)
    return _row_stack([jnp.where(left, qg, zero), jnp.where(left, zero, qg)])


def _attn_kernel(sink_ref, p_ref, pc_ref, lam_ref, g_ref, tb_ref, o_ref, *, lam_init, seq, diff_only):
    n = pl.program_id(1)
    units_a, units_b = [], []
    lane = lax.broadcasted_iota(jnp.int32, (1, LANES), 1)
    left = lane < HEAD_DIM

    nb = seq // WIN_BLK
    for sub in range(Q_BLK // WIN_BLK):
        blk = n * (Q_BLK // WIN_BLK) + sub
        q0 = pl.multiple_of(blk * WIN_BLK, WIN_BLK)
        ws = pl.multiple_of(jnp.clip(blk - 1, 0, nb - 3) * WIN_BLK, WIN_BLK)
        qpos = q0 + lax.broadcasted_iota(jnp.int32, (A_HEADS * WIN_BLK, 3 * WIN_BLK), 0) % WIN_BLK
        kpos = ws + lax.broadcasted_iota(jnp.int32, (A_HEADS * WIN_BLK, 3 * WIN_BLK), 1)
        valid = jnp.abs(qpos - kpos) <= WINDOW
        rows = slice(sub * WIN_BLK, (sub + 1) * WIN_BLK)

        def start(_, q0=q0, ws=ws, valid=valid):
            q = _a_queries(p_ref, pl.ds(q0, WIN_BLK))
            kw = p_ref[0, pl.ds(ws, 3 * WIN_BLK), _lane_group(AK)]
            kc = pc_ref[0, :, _lane_group(AK)]
            return jnp.where(valid, _dot_nt(q, kw), NEG_INF), _dot_nt(q, kc)

        def soft(scores):
            return _softmax_weights([(scores[0], None), (scores[1], None)], extra_logit=_a_sink(sink_ref, WIN_BLK))

        def finish(weights_den, ws=ws, rows=rows):
            vw = p_ref[0, pl.ds(ws, 3 * WIN_BLK), _lane_group(AV)]
            vc = pc_ref[0, :, _lane_group(AV)]
            _a_store(_weighted_values(weights_den[0], [vw, vc], weights_den[1]), o_ref, rows, left)

        units_a.append((start, soft, finish))

    lam = _lam_value(lam_ref, lam_init)
    qb = pl.multiple_of(n * Q_BLK, Q_BLK)
    for h in range(B_HEADS):
        def start(_, h=h):
            q = _b_queries(p_ref, pl.ds(qb, Q_BLK), h, lane)
            kcol = BK + h * 2 * B_QK_DIM
            k_lat = p_ref[0, :, _lane_group(kcol)]
            k_ctx = pc_ref[0, :, _lane_group(kcol)]
            return _dot_nt(q, k_lat), _dot_nt(q, k_ctx)

        def soft(scores):
            return _diff_head_weights(scores, lam)

        def finish(weights_den, h=h):
            vcol = BV + h * HEAD_DIM
            v_lat = p_ref[0, :, _lane_group(vcol)]
            v_ctx = pc_ref[0, :, _lane_group(vcol)]
            o = _diff_head_out(*weights_den, [v_lat, v_ctx], g_ref[...], lam_init, vcol % LANES)
            o_ref[0, :, _head_cols(0 if diff_only else YB, h)] = o.astype(BF16)

        units_b.append((start, soft, finish))

    n_rows = seq // GRID_W
    blocks = []
    for i in range(Q_BLK // GRID_W):
        r = n * (Q_BLK // GRID_W) + i
        rs = jnp.clip(r - NA_KH // 2, 0, n_rows - NA_KH)
        dr0 = (NA_KH - 1) - (r - rs)
        qr = pl.multiple_of(r * GRID_W, GRID_W)
        kr = pl.multiple_of(rs * GRID_W, GRID_W)
        blocks += [(i, j, qr, kr, dr0) for j in range(C_HEADS // 2)]

    def start_c(_):
        s_w, s_c = [], []
        for _, j, qr, kr, _ in blocks:
            q = _c_queries(p_ref, pl.ds(qr, GRID_W), j, left)
            s_w.append(_dot_nt(q, p_ref[0, pl.ds(kr, NA_KH * GRID_W), _lane_group(CK + j * LANES)]))
            s_c.append(_dot_nt(q, pc_ref[0, :, _lane_group(CK + j * LANES)]))
        return _row_stack(s_w), _row_stack(s_c)

    def soft_c(scores):
        bias2 = _row_stack([
            jnp.concatenate([jnp.where(left, tb_ref[h, dr0 + kh], tb_ref[h, dr0 + kh + 1])
                             for kh in range(0, NA_KH, 2)], axis=-1)
            for _, j, _, _, dr0 in blocks for h in (2 * j, 2 * j + 1)])
        return _softmax_weights([(scores[0], bias2), (scores[1], None)])

    def finish_c(weights_den):
        vw = [p_ref[0, pl.ds(kr, NA_KH * GRID_W), _lane_group(CV + j * LANES)] for _, j, _, kr, _ in blocks]
        vc = [pc_ref[0, :, _lane_group(CV + j * LANES)] for _, j, _, _, _ in blocks]
        o = _weighted_values(weights_den[0], [vw, vc], weights_den[1])
        for k, (i, j, _, _, _) in enumerate(blocks):
            ok = o[2 * k * GRID_W:2 * (k + 1) * GRID_W]
            o_ref[0, i * GRID_W:(i + 1) * GRID_W, YC + j * LANES:YC + (j + 1) * LANES] = (
                jnp.where(left, ok[0:GRID_W], ok[GRID_W:2 * GRID_W]).astype(BF16))

    if diff_only:
        _pipelined(units_b, (0, ATTN_LOOKAHEAD, ATTN_LOOKAHEAD))
    else:
        o_ref[0, :, YB:YC] = jnp.zeros((Q_BLK, YC - YB), BF16)
        _pipelined([(start_c, soft_c, finish_c)] + units_a, (0, ATTN_LOOKAHEAD, ATTN_LOOKAHEAD))


def _attn(p, pc, sink_l, lamv, g, tb, lam_init):
    bsz, seq, _ = p.shape
    n_ctx = pc.shape[1]
    def call(diff_only):
        width = YC - YB if diff_only else MIX_WIDTH
        return pl.pallas_call(
            functools.partial(_attn_kernel, lam_init=lam_init, seq=seq, diff_only=diff_only),
            grid=(bsz, seq // Q_BLK),
            in_specs=[
                pl.BlockSpec(memory_space=pltpu.SMEM),
                pl.BlockSpec((1, seq, PROJ_WIDTH), lambda b, n: (b, 0, 0)),
                pl.BlockSpec((1, n_ctx, PROJ_WIDTH), lambda b, n: (b, 0, 0)),
                pl.BlockSpec((4, B_QK_DIM), lambda b, n: (0, 0)),
                pl.BlockSpec((1, HEAD_DIM), lambda b, n: (0, 0)),
                pl.BlockSpec(tb.shape, lambda b, n: (0, 0, 0, 0)),
            ],
            out_specs=pl.BlockSpec((1, Q_BLK, width), lambda b, n: (b, n, 0)),
            out_shape=jax.ShapeDtypeStruct((bsz, seq, width), BF16),
            compiler_params=pltpu.CompilerParams(vmem_limit_bytes=VMEM_LIMIT),
            name="attn_diff" if diff_only else "attn_win",
        )(sink_l, p, pc, lamv, g, tb)

    return call(False), call(True)


def _ctx_attn_kernel(sink_ref, pc_ref, lam_ref, g_ref, o_ref, *, lam_init):
    t = pc_ref.shape[1]
    every = slice(0, t)
    lane = lax.broadcasted_iota(jnp.int32, (1, LANES), 1)
    left = lane < HEAD_DIM
    lam = _lam_value(lam_ref, lam_init)

    def units_of(pc, out):
        def start_a(_):
            return _dot_nt(_a_queries(pc, every), pc[0, :, _lane_group(AK)])

        def finish_a(s):
            o = _softmax_av([(s, None, pc[0, :, _lane_group(AV)])], extra_logit=_a_sink(sink_ref, t))
            _a_store(o, out, every, left)

        units = [(start_a, finish_a)]
        for h in range(B_HEADS):
            def start(_, h=h):
                return (_dot_nt(_b_queries(pc, every, h, lane), pc[0, :, _lane_group(BK + h * HEAD_DIM)]),)

            def finish(scores, h=h):
                vcol = BV + h * HEAD_DIM
                o = _diff_head_out(*_diff_head_weights(scores, lam), [pc[0, :, _lane_group(vcol)]], g_ref[...],
                                   lam_init, vcol % LANES)
                out[0, :, _head_cols(YB, h)] = o.astype(BF16)

            units.append((start, finish))

        def start_c(_):
            return _row_stack([_dot_nt(_c_queries(pc, every, j, left), pc[0, :, _lane_group(CK + j * LANES)])
                               for j in range(C_HEADS // 2)])

        def finish_c(s):
            o = _softmax_av([(s, None, [pc[0, :, _lane_group(CV + j * LANES)] for j in range(C_HEADS // 2)])])
            for j in range(C_HEADS // 2):
                oj = o[2 * j * t:2 * (j + 1) * t]
                out[0, :, YC + j * LANES:YC + (j + 1) * LANES] = jnp.where(left, oj[0:t], oj[t:2 * t]).astype(BF16)

        return units + [(start_c, finish_c)]

    units = []
    for bi in range(pc_ref.shape[0]):
        units += units_of(pc_ref.at[pl.ds(bi, 1)], o_ref.at[pl.ds(bi, 1)])
    _pipelined(units, (0, ATTN_LOOKAHEAD))


def _ctx_attn(pc, sink_l, lamv, g, lam_init):
    bsz, n_ctx, _ = pc.shape
    return pl.pallas_call(
        functools.partial(_ctx_attn_kernel, lam_init=lam_init),
        grid=(bsz // CTX_BATCH,),
        in_specs=[
            pl.BlockSpec(memory_space=pltpu.SMEM),
            pl.BlockSpec((CTX_BATCH, n_ctx, PROJ_WIDTH), lambda b: (b, 0, 0)),
            pl.BlockSpec((4, B_QK_DIM), lambda b: (0, 0)),
            pl.BlockSpec((1, HEAD_DIM), lambda b: (0, 0)),
        ],
        out_specs=pl.BlockSpec((CTX_BATCH, n_ctx, MIX_WIDTH), lambda b: (b, 0, 0)),
        out_shape=jax.ShapeDtypeStruct((bsz, n_ctx, MIX_WIDTH), BF16),
        compiler_params=pltpu.CompilerParams(vmem_limit_bytes=VMEM_LIMIT),
        name="ctx_attn",
    )(sink_l, pc, lamv, g)


N_DR = 2 * NA_KH - 1
N_DC = 2 * NA_KW - 1


def _na_table_kernel(nb_ref, o_ref):
    base = (pl.program_id(0) * C_HEADS + pl.program_id(1)) * (N_DR * N_DC)
    wq = lax.broadcasted_iota(jnp.int32, (GRID_W, LANES), 0)
    wk = lax.broadcasted_iota(jnp.int32, (GRID_W, LANES), 1) % GRID_W
    dc = jnp.clip(wk - wq, -(NA_KW - 1), NA_KW - 1) + (NA_KW - 1)
    cs = jnp.clip(wq - NA_KW // 2, 0, GRID_W - NA_KW)
    valid = (wk >= cs) & (wk < cs + NA_KW)
    for dr in range(N_DR):
        acc = jnp.zeros((GRID_W, LANES), F32)
        for c in range(N_DC):
            acc = jnp.where(dc == c, nb_ref[base + dr * N_DC + c], acc)
        o_ref[0, 0, dr] = jnp.where(valid, acc * LOG2E, NEG_INF)


def _na_table(na_bias):
    return pl.pallas_call(
        _na_table_kernel,
        grid=(DEPTH, C_HEADS),
        in_specs=[pl.BlockSpec(memory_space=pltpu.SMEM)],
        out_specs=pl.BlockSpec((1, 1, N_DR, GRID_W, LANES), lambda l, h: (l, h, 0, 0, 0)),
        out_shape=jax.ShapeDtypeStruct((DEPTH, C_HEADS, N_DR, GRID_W, LANES), F32),
        name="na_table",
    )(na_bias.reshape(-1))


def _outffn_kernel(x_ref, y_ref, yb_ref, g1_ref, sh2_ref, sc2_ref, g2_ref, wo_ref, ln1g_ref, ln1b_ref, ln2g_ref, ln2b_ref,
                   wg_ref, wu_ref, wd_ref, o_ref, *, sub):
    def rows(u):
        return slice(u * sub, (u + 1) * sub)

    def out_proj(u, _):
        y = y_ref[0, rows(u)]
        y = jnp.concatenate([y[:, :YB], yb_ref[0, rows(u)], y[:, YC:]], axis=1)
        return _dot(y, wo_ref[0])

    def norm_gate_up(u, y):
        xn = _layer_norm(ALPHA * x_ref[0, rows(u)] + g1_ref[0] * y, ln1g_ref[...], ln1b_ref[...])
        h = (xn * (1.0 + sc2_ref[0]) + sh2_ref[0]).astype(BF16)
        return xn, _dot(h, wg_ref[0]), _dot(h, wu_ref[0])

    def act_down(u, st):
        xn, gate, up = st
        return xn, _dot((_silu(gate) * up).astype(BF16), wd_ref[0])

    def norm_store(u, st):
        xn, ff = st
        o_ref[0, rows(u)] = _layer_norm(ALPHA * xn + g2_ref[0] * ff, ln2g_ref[...], ln2b_ref[...])

    _staged(x_ref.shape[1] // sub, [out_proj, norm_gate_up, act_down, norm_store])


def _outffn(xs, y, yb, mod, layer, wo_b, ln1g, ln1b, ln2g, ln2b, wfi_b, wfo_b, *, ctx):
    bsz, seq, _ = xs.shape
    t = min(FFN_TILE, seq)
    row = (lambda b: CTX_ROW) if ctx else (lambda b: b)
    once = pl.Buffered(1)

    def mod_spec(k):
        return pl.BlockSpec((1, 1, D_MODEL), lambda b, i: (row(b), 0, k))

    vec = pl.BlockSpec((1, D_MODEL), lambda b, i: (0, 0))
    return pl.pallas_call(
        functools.partial(_outffn_kernel, sub=min(SUB_TILE, t // 2)),
        grid=(bsz, seq // t),
        in_specs=[
            pl.BlockSpec((1, t, D_MODEL), lambda b, i: (b, i, 0)),
            pl.BlockSpec((1, t, MIX_WIDTH), lambda b, i: (b, i, 0)),
            pl.BlockSpec((1, t, YC - YB), lambda b, i: (b, i, 0)),
            mod_spec(2), mod_spec(3), mod_spec(4), mod_spec(5),
            pl.BlockSpec((1, MIX_WIDTH, D_MODEL), lambda b, i: (layer, 0, 0), pipeline_mode=once),
            vec, vec, vec, vec,
            pl.BlockSpec((1, D_MODEL, D_FF), lambda b, i: (layer, 0, 0), pipeline_mode=once),
            pl.BlockSpec((1, D_MODEL, D_FF), lambda b, i: (layer, 0, 1), pipeline_mode=once),
            pl.BlockSpec((1, D_FF, D_MODEL), lambda b, i: (layer, 0, 0), pipeline_mode=once),
        ],
        out_specs=pl.BlockSpec((1, t, D_MODEL), lambda b, i: (b, i, 0)),
        out_shape=jax.ShapeDtypeStruct((bsz, seq, D_MODEL), F32),
        compiler_params=pltpu.CompilerParams(vmem_limit_bytes=VMEM_LIMIT),
        name="outffn_ctx" if ctx else "outffn",
    )(xs, y, yb, mod, mod, mod, mod, wo_b, ln1g, ln1b, ln2g, ln2b, wfi_b, wfi_b, wfo_b)


def kernel(x, c, ctx, c_ctx, w_ada, b_ada, w_in, w_o, sink, lam_q1, lam_k1, lam_q2, lam_k2, subln_g, na_bias,
           ln1_g, ln1_b, w_ffn_in, w_ffn_out, ln2_g, ln2_b):
    bsz, seq, _ = x.shape
    assert x.shape == (bsz, seq, D_MODEL) and seq % Q_BLK == 0 and seq // Q_BLK >= 3
    assert bsz < CTX_ROW + 1 <= MOD_ROWS and bsz % CTX_BATCH == 0
    cc = jnp.zeros((MOD_ROWS, D_MODEL), F32).at[:bsz].set(c).at[CTX_ROW].set(c_ctx)
    mod_all = _ada(cc, w_ada, b_ada).reshape(DEPTH, MOD_ROWS, 1, N_MOD * D_MODEL)
    tables = _rope_tables(seq)
    na_tab = _na_table(na_bias)
    w_in_b, wo_b, wfi_b, wfo_b = (_cast_bf16(w) for w in (w_in, w_o, w_ffn_in, w_ffn_out))

    xs, cs = x, ctx
    for l in range(DEPTH):
        last = l == DEPTH - 1
        lam_init = 0.8 - 0.6 * math.exp(-0.3 * l)
        mod = mod_all[l]
        lamv = jnp.stack([lam_q1[l], lam_k1[l], lam_q2[l], lam_k2[l]])
        g = subln_g[l].reshape(1, HEAD_DIM)
        lnp = [v[l].reshape(1, D_MODEL) for v in (ln1_g, ln1_b, ln2_g, ln2_b)]

        p = _inproj(xs, mod, w_in_b, l, tables, ctx=False)
        pc = _inproj(cs, mod, w_in_b, l, None, ctx=True)
        y, yb = _attn(p, pc, sink[l], lamv, g, na_tab[l], lam_init)
        xn = _outffn(xs, y, yb, mod, l, wo_b, *lnp, wfi_b, wfo_b, ctx=False)
        if not last:
            yc = _ctx_attn(pc, sink[l], lamv, g, lam_init)
            cs = _outffn(cs, yc, yc[:, :, YB:YC], mod, l, wo_b, *lnp, wfi_b, wfo_b, ctx=True)
        xs = xn
    return xs
```

```python
import functools
import math

import jax
import jax.numpy as jnp
import numpy as np
from jax import lax
from jax.experimental import pallas as pl
from jax.experimental.pallas import tpu as pltpu

F32 = jnp.float32
BF16 = jnp.bfloat16

D_MODEL = 1024
DEPTH = 2
GRID_W = 64
HEAD_DIM = 64
A_HEADS = 6
A_KV_HEADS = 2
A_REP = A_HEADS // A_KV_HEADS
WINDOW = 128
WIN_BLK = 128
B_HEADS = 4
B_QK_DIM = 32
C_HEADS = 6
NA_KH = 8
NA_KW = 16
N_MOD = 6
D_FF = 2816
MIX_WIDTH = (A_HEADS + B_HEADS + C_HEADS) * HEAD_DIM
ROPE_BASE = 10000.0
LN_EPS = 1e-5
NEG_INF = -1e30
LOG2E = 1.4426950408889634
QK_SCALE2_A = HEAD_DIM ** -0.5 * LOG2E
QK_SCALE2_B = B_QK_DIM ** -0.5 * LOG2E
ALPHA = (2.0 * DEPTH) ** 0.25

AQ = 0
AK = AQ + A_HEADS * HEAD_DIM
AV = AK + A_KV_HEADS * HEAD_DIM
BQ = AV + A_KV_HEADS * HEAD_DIM
BK = BQ + B_HEADS * 2 * B_QK_DIM
BV = BK + B_HEADS * 2 * B_QK_DIM
CQ = BV + B_HEADS * HEAD_DIM
CK = CQ + C_HEADS * HEAD_DIM
CV = CK + C_HEADS * HEAD_DIM
PROJ_WIDTH = CV + C_HEADS * HEAD_DIM

YA = 0
YB = YA + A_HEADS * HEAD_DIM
YC = YB + B_HEADS * HEAD_DIM

LANES = 128
MOD_ROWS = 16
CTX_ROW = 8
VMEM_LIMIT = 56 * 1024 * 1024

Q_BLK = 256
CTX_BATCH = 2
ATTN_LOOKAHEAD = 1
IN_TILE = 1024
FFN_TILE = 1024
SUB_TILE = 256
CAST_BLOCK_BYTES = 4 * 1024 * 1024


def _dot(a, b):
    return jnp.dot(a, b, preferred_element_type=F32)


def _dot_nt(a, b):
    return lax.dot_general(a, b, (((1,), (1,)), ((), ())), preferred_element_type=F32)


def _silu(v):
    return v / (1.0 + jnp.exp(-v))


def _layer_norm(v, g, b):
    mu = jnp.mean(v, axis=-1, keepdims=True)
    d = v - mu
    var = jnp.mean(d * d, axis=-1, keepdims=True)
    return d * lax.rsqrt(var + LN_EPS) * g + b


def _ada_kernel(c_ref, w_ref, b_ref, o_ref):
    s = _silu(c_ref[...])
    o_ref[0] = _dot(s.astype(BF16), w_ref[0].astype(BF16)) + b_ref[0]


def _ada(cc, w_ada, b_ada):
    tn = D_MODEL
    return pl.pallas_call(
        _ada_kernel,
        grid=(DEPTH, N_MOD * D_MODEL // tn),
        in_specs=[
            pl.BlockSpec((MOD_ROWS, D_MODEL), lambda l, j: (0, 0)),
            pl.BlockSpec((1, D_MODEL, tn), lambda l, j: (l, 0, j)),
            pl.BlockSpec((1, 1, tn), lambda l, j: (l, 0, j)),
        ],
        out_specs=pl.BlockSpec((1, MOD_ROWS, tn), lambda l, j: (l, 0, j)),
        out_shape=jax.ShapeDtypeStruct((DEPTH, MOD_ROWS, N_MOD * D_MODEL), F32),
        compiler_params=pltpu.CompilerParams(vmem_limit_bytes=VMEM_LIMIT),
        name="ada",
    )(cc, w_ada, b_ada.reshape(DEPTH, 1, N_MOD * D_MODEL))


def _cast_kernel(w_ref, o_ref):
    o_ref[...] = w_ref[...].astype(BF16)


def _cast_bf16(w):
    depth, rows, cols = w.shape
    blk = rows
    while blk * cols * 4 > CAST_BLOCK_BYTES and blk % 32 == 0:
        blk //= 2
    return pl.pallas_call(
        _cast_kernel,
        grid=(depth, rows // blk),
        in_specs=[pl.BlockSpec((1, blk, cols), lambda l, i: (l, i, 0))],
        out_specs=pl.BlockSpec((1, blk, cols), lambda l, i: (l, i, 0)),
        out_shape=jax.ShapeDtypeStruct(w.shape, BF16),
        name="cast_bf16",
    )(w)


def _rope_group(v, cos, sin, off):
    lane = lax.broadcasted_iota(jnp.int32, v.shape, 1)
    low = (lane % (2 * off)) < off
    partner = jnp.where(low, pltpu.roll(v, LANES - off, 1), pltpu.roll(v, off, 1))
    return v * cos + partner * sin


def _staged(n_units, stages):
    state = [None] * n_units
    for step in range(n_units + len(stages) - 1):
        for k, stage in enumerate(stages):
            u = step - k
            if 0 <= u < n_units:
                state[u] = stage(u, state[u])


def _lane_groups(lo, hi):
    return tuple(range(lo // LANES, hi // LANES))


_GROUP_PLAN = {}
for _g in _lane_groups(AQ, AK):
    _GROUP_PLAN[_g] = (0, HEAD_DIM // 4, QK_SCALE2_A)
for _g in _lane_groups(AK, AV):
    _GROUP_PLAN[_g] = (1, HEAD_DIM // 4, None)
for _g in _lane_groups(BQ, BK):
    _GROUP_PLAN[_g] = (2, B_QK_DIM // 4, QK_SCALE2_B)
for _g in _lane_groups(BK, BV):
    _GROUP_PLAN[_g] = (3, B_QK_DIM // 4, None)
for _g in _lane_groups(CQ, CK):
    _GROUP_PLAN[_g] = (None, None, QK_SCALE2_A)


def _inproj_kernel(x_ref, sh_ref, sc_ref, w_ref, *rest, rope, sub):
    if rope:
        tab_ref, o_ref = rest
    else:
        (o_ref,) = rest

    def rows(u):
        return slice(u * sub, (u + 1) * sub)

    def project(u, _):
        h = x_ref[0, rows(u)] * (1.0 + sc_ref[0]) + sh_ref[0]
        return _dot(h.astype(BF16), w_ref[0])

    def rotate_store(u, p):
        for g in range(PROJ_WIDTH // LANES):
            v = p[:, g * LANES:(g + 1) * LANES]
            pair, quarter, qscale = _GROUP_PLAN.get(g, (None, None, None))
            if rope and pair is not None:
                v = _rope_group(v, tab_ref[2 * pair, rows(u)], tab_ref[2 * pair + 1, rows(u)], quarter)
            elif qscale is not None:
                v = v * qscale
            o_ref[0, rows(u), g * LANES:(g + 1) * LANES] = v.astype(BF16)

    _staged(x_ref.shape[1] // sub, [project, rotate_store])


def _inproj(xs, mod, w_in_b, layer, tables, *, ctx):
    bsz, seq, _ = xs.shape
    t = min(IN_TILE, seq)
    row = (lambda b: CTX_ROW) if ctx else (lambda b: b)
    in_specs = [
        pl.BlockSpec((1, t, D_MODEL), lambda b, i: (b, i, 0)),
        pl.BlockSpec((1, 1, D_MODEL), lambda b, i: (row(b), 0, 0)),
        pl.BlockSpec((1, 1, D_MODEL), lambda b, i: (row(b), 0, 1)),
        pl.BlockSpec((1, D_MODEL, PROJ_WIDTH), lambda b, i: (layer, 0, 0), pipeline_mode=pl.Buffered(1)),
    ]
    args = [xs, mod, mod, w_in_b]
    if tables is not None:
        in_specs.append(pl.BlockSpec((tables.shape[0], t, LANES), lambda b, i: (0, i, 0)))
        args.append(tables)
    return pl.pallas_call(
        functools.partial(_inproj_kernel, rope=tables is not None, sub=min(SUB_TILE, t)),
        grid=(bsz, seq // t),
        in_specs=in_specs,
        out_specs=pl.BlockSpec((1, t, PROJ_WIDTH), lambda b, i: (b, i, 0)),
        out_shape=jax.ShapeDtypeStruct((bsz, seq, PROJ_WIDTH), BF16),
        compiler_params=pltpu.CompilerParams(vmem_limit_bytes=VMEM_LIMIT),
        name="inproj_ctx" if ctx else "inproj",
    )(*args)


def _rope_tables(seq):
    f32 = np.float32
    tpos = np.arange(seq, dtype=np.int32)
    rows = (tpos // GRID_W).astype(f32)[:, None]
    cols = (tpos % GRID_W).astype(f32)[:, None]
    lane = np.arange(LANES, dtype=np.int32)

    def table(head_dim):
        quarter = head_dim // 4
        inv = f32(ROPE_BASE) ** (-np.arange(quarter, dtype=f32) / f32(quarter))
        freq = inv[lane % quarter][None, :].astype(f32)
        use_cols = ((lane % head_dim) >= head_dim // 2)[None, :]
        ang = np.where(use_cols, cols * freq, rows * freq).astype(f32)
        sign = np.where((lane % (2 * quarter)) < quarter, f32(-1.0), f32(1.0))[None, :]
        return np.cos(ang).astype(f32), (np.sin(ang) * sign).astype(f32)

    cos_a, sin_a = table(HEAD_DIM)
    cos_b, sin_b = table(B_QK_DIM)
    return np.stack([cos_a * f32(QK_SCALE2_A), sin_a * f32(QK_SCALE2_A), cos_a, sin_a,
                     cos_b * f32(QK_SCALE2_B), sin_b * f32(QK_SCALE2_B), cos_b, sin_b]).astype(f32)


def _lam_value(lam_ref, lam_init):
    v = lam_ref[...]
    s1 = jnp.sum(v[0:1, :] * v[1:2, :], axis=-1, keepdims=True)
    s2 = jnp.sum(v[2:3, :] * v[3:4, :], axis=-1, keepdims=True)
    return jnp.exp(s1) - jnp.exp(s2) + lam_init


def _row_stack(parts):
    return jnp.concatenate(parts, axis=0)


def _dot_blocks(e, v):
    if not isinstance(v, (list, tuple)):
        return _dot(e, v)
    t = e.shape[0] // len(v)
    return _row_stack([_dot(e[k * t:(k + 1) * t], vk) for k, vk in enumerate(v)])


def _softmax_av(pieces, extra_logit=None):
    weights, den = _softmax_weights([(s, bias2) for s, bias2, _ in pieces], extra_logit)
    return _weighted_values(weights, [v for _, _, v in pieces], den)


def _softmax_weights(pieces, extra_logit=None):
    terms, m2 = [], None
    for s, bias2 in pieces:
        t = s if bias2 is None else s + bias2
        ms = jnp.max(t, axis=-1, keepdims=True)
        terms.append(t)
        m2 = ms if m2 is None else jnp.maximum(m2, ms)
    den = None
    if extra_logit is not None:
        m2 = jnp.maximum(m2, extra_logit * LOG2E)
        den = jnp.exp2(extra_logit * LOG2E - m2)
    weights = []
    for t in terms:
        e = jnp.exp2(t - m2)
        ls = jnp.sum(e, axis=-1, keepdims=True)
        den = ls if den is None else den + ls
        weights.append(e.astype(BF16))
    return weights, den


def _weighted_values(weights, values, den):
    out = None
    for e, v in zip(weights, values):
        o = _dot_blocks(e, v)
        out = o if out is None else out + o
    return out / den


def _lane_group(col):
    g0 = col // LANES * LANES
    return slice(g0, g0 + LANES)


def _placed(q, col):
    t, w = q.shape
    off = col % LANES
    parts = [jnp.zeros((t, off), q.dtype)] if off else []
    parts.append(q)
    if LANES - off - w:
        parts.append(jnp.zeros((t, LANES - off - w), q.dtype))
    return jnp.concatenate(parts, axis=-1)


def _pipelined(units, offsets):
    state = [None] * len(units)
    for step in range(len(units) + max(offsets)):
        for k, off in enumerate(offsets):
            u = step - off
            if 0 <= u < len(units):
                state[u] = units[u][k](state[u])


def _head_cols(base, h):
    return slice(base + h * HEAD_DIM, base + (h + 1) * HEAD_DIM)


def _diff_head_weights(scores, lam):
    t = scores[0].shape[0] // 2
    mx = None
    for s in scores:
        ms = jnp.max(s, axis=-1, keepdims=True)
        mx = ms if mx is None else jnp.maximum(mx, ms)
    es = [jnp.exp2(s - mx) for s in scores]
    den = None
    for e in es:
        ls = jnp.sum(e, axis=-1, keepdims=True)
        den = ls if den is None else den + ls
    w = lam * den[0:t] / den[t:2 * t]
    return [(e[0:t] - w * e[t:2 * t]).astype(BF16) for e in es], den[0:t]


def _diff_head_out(weights, den, v_pieces, g, lam_init, v_off):
    o = _weighted_values(weights, v_pieces, den)[:, v_off:v_off + HEAD_DIM]
    ms = jnp.mean(o * o, axis=-1, keepdims=True)
    return o * lax.rsqrt(ms + LN_EPS) * g * (1.0 - lam_init)


def _a_queries(ref, rows):
    return _row_stack([_placed(ref[0, rows, _head_cols(AQ, h)], AK + (h // A_REP) * HEAD_DIM) for h in range(A_HEADS)])


def _a_sink(sink_ref, t):
    return _row_stack([jnp.full((t, 1), sink_ref[h], F32) for h in range(A_HEADS)])


def _a_store(o, o_ref, rows, left):
    t = o.shape[0] // A_HEADS
    for j in range(A_HEADS // 2):
        halves = []
        for h in (2 * j, 2 * j + 1):
            oh = o[h * t:(h + 1) * t]
            if h // A_REP != h % 2:
                oh = pltpu.roll(oh, HEAD_DIM, 1)
            halves.append(oh)
        o_ref[0, rows, YA + j * LANES:YA + (j + 1) * LANES] = jnp.where(left, halves[0], halves[1]).astype(BF16)


def _b_queries(ref, rows, h, lane):
    qcol = BQ + h * 2 * B_QK_DIM
    qg = ref[0, rows, _lane_group(qcol)]
    zero = jnp.zeros_like(qg)
    offs = [qcol % LANES + m * B_QK_DIM for m in range(2)]
    return _row_stack([jnp.where((lane >= off) & (lane < off + B_QK_DIM), qg, zero) for off in offs])


def _c_queries(ref, rows, j, left):
    qg = ref[0, rows, _lane_group(CQ + j * LANES)]
    zero = jnp.zeros_like(qg)
    return _row_stack([jnp.where(left, qg, zero), jnp.where(left, zero, qg)])


def _attn_kernel(sink_ref, p_ref, pc_ref, lam_ref, g_ref, tb_ref, o_ref, *, lam_init, seq):
    n = pl.program_id(1)
    units_a, units_b = [], []
    lane = lax.broadcasted_iota(jnp.int32, (1, LANES), 1)
    left = lane < HEAD_DIM

    nb = seq // WIN_BLK
    for sub in range(Q_BLK // WIN_BLK):
        blk = n * (Q_BLK // WIN_BLK) + sub
        q0 = pl.multiple_of(blk * WIN_BLK, WIN_BLK)
        ws = pl.multiple_of(jnp.clip(blk - 1, 0, nb - 3) * WIN_BLK, WIN_BLK)
        qpos = q0 + lax.broadcasted_iota(jnp.int32, (A_HEADS * WIN_BLK, 3 * WIN_BLK), 0) % WIN_BLK
        kpos = ws + lax.broadcasted_iota(jnp.int32, (A_HEADS * WIN_BLK, 3 * WIN_BLK), 1)
        valid = jnp.abs(qpos - kpos) <= WINDOW
        rows = slice(sub * WIN_BLK, (sub + 1) * WIN_BLK)

        def start(_, q0=q0, ws=ws, valid=valid):
            q = _a_queries(p_ref, pl.ds(q0, WIN_BLK))
            kw = p_ref[0, pl.ds(ws, 3 * WIN_BLK), _lane_group(AK)]
            kc = pc_ref[0, :, _lane_group(AK)]
            return jnp.where(valid, _dot_nt(q, kw), NEG_INF), _dot_nt(q, kc)

        def soft(scores):
            return _softmax_weights([(scores[0], None), (scores[1], None)], extra_logit=_a_sink(sink_ref, WIN_BLK))

        def finish(weights_den, ws=ws, rows=rows):
            vw = p_ref[0, pl.ds(ws, 3 * WIN_BLK), _lane_group(AV)]
            vc = pc_ref[0, :, _lane_group(AV)]
            _a_store(_weighted_values(weights_den[0], [vw, vc], weights_den[1]), o_ref, rows, left)

        units_a.append((start, soft, finish))

    lam = _lam_value(lam_ref, lam_init)
    qb = pl.multiple_of(n * Q_BLK, Q_BLK)
    for h in range(B_HEADS):
        def start(_, h=h):
            q = _b_queries(p_ref, pl.ds(qb, Q_BLK), h, lane)
            kcol = BK + h * 2 * B_QK_DIM
            k_lat = p_ref[0, :, _lane_group(kcol)]
            k_ctx = pc_ref[0, :, _lane_group(kcol)]
            return _dot_nt(q, k_lat), _dot_nt(q, k_ctx)

        def soft(scores):
            return _diff_head_weights(scores, lam)

        def finish(weights_den, h=h):
            vcol = BV + h * HEAD_DIM
            v_lat = p_ref[0, :, _lane_group(vcol)]
            v_ctx = pc_ref[0, :, _lane_group(vcol)]
            o = _diff_head_out(*weights_den, [v_lat, v_ctx], g_ref[...], lam_init, vcol % LANES)
            o_ref[0, :, _head_cols(YB, h)] = o.astype(BF16)

        units_b.append((start, soft, finish))

    n_rows = seq // GRID_W
    blocks = []
    for i in range(Q_BLK // GRID_W):
        r = n * (Q_BLK // GRID_W) + i
        rs = jnp.clip(r - NA_KH // 2, 0, n_rows - NA_KH)
        dr0 = (NA_KH - 1) - (r - rs)
        qr = pl.multiple_of(r * GRID_W, GRID_W)
        kr = pl.multiple_of(rs * GRID_W, GRID_W)
        blocks += [(i, j, qr, kr, dr0) for j in range(C_HEADS // 2)]

    def start_c(_):
        s_w, s_c = [], []
        for _, j, qr, kr, _ in blocks:
            q = _c_queries(p_ref, pl.ds(qr, GRID_W), j, left)
            s_w.append(_dot_nt(q, p_ref[0, pl.ds(kr, NA_KH * GRID_W), _lane_group(CK + j * LANES)]))
            s_c.append(_dot_nt(q, pc_ref[0, :, _lane_group(CK + j * LANES)]))
        return _row_stack(s_w), _row_stack(s_c)

    def soft_c(scores):
        bias2 = _row_stack([
            jnp.concatenate([jnp.where(left, tb_ref[h, dr0 + kh], tb_ref[h, dr0 + kh + 1])
                             for kh in range(0, NA_KH, 2)], axis=-1)
            for _, j, _, _, dr0 in blocks for h in (2 * j, 2 * j + 1)])
        return _softmax_weights([(scores[0], bias2), (scores[1], None)])

    def finish_c(weights_den):
        vw = [p_ref[0, pl.ds(kr, NA_KH * GRID_W), _lane_group(CV + j * LANES)] for _, j, _, kr, _ in blocks]
        vc = [pc_ref[0, :, _lane_group(CV + j * LANES)] for _, j, _, _, _ in blocks]
        o = _weighted_values(weights_den[0], [vw, vc], weights_den[1])
        for k, (i, j, _, _, _) in enumerate(blocks):
            ok = o[2 * k * GRID_W:2 * (k + 1) * GRID_W]
            o_ref[0, i * GRID_W:(i + 1) * GRID_W, YC + j * LANES:YC + (j + 1) * LANES] = (
                jnp.where(left, ok[0:GRID_W], ok[GRID_W:2 * GRID_W]).astype(BF16))

    _pipelined([(start_c, soft_c, finish_c)] + units_a + units_b, (0, ATTN_LOOKAHEAD, ATTN_LOOKAHEAD))


def _attn(p, pc, sink_l, lamv, g, tb, lam_init):
    bsz, seq, _ = p.shape
    n_ctx = pc.shape[1]
    return pl.pallas_call(
        functools.partial(_attn_kernel, lam_init=lam_init, seq=seq),
        grid=(bsz, seq // Q_BLK),
        in_specs=[
            pl.BlockSpec(memory_space=pltpu.SMEM),
            pl.BlockSpec((1, seq, PROJ_WIDTH), lambda b, n: (b, 0, 0)),
            pl.BlockSpec((1, n_ctx, PROJ_WIDTH), lambda b, n: (b, 0, 0)),
            pl.BlockSpec((4, B_QK_DIM), lambda b, n: (0, 0)),
            pl.BlockSpec((1, HEAD_DIM), lambda b, n: (0, 0)),
            pl.BlockSpec(tb.shape, lambda b, n: (0, 0, 0, 0), pipeline_mode=pl.Buffered(1)),
        ],
        out_specs=pl.BlockSpec((1, Q_BLK, MIX_WIDTH), lambda b, n: (b, n, 0)),
        out_shape=jax.ShapeDtypeStruct((bsz, seq, MIX_WIDTH), BF16),
        compiler_params=pltpu.CompilerParams(vmem_limit_bytes=VMEM_LIMIT),
        name="attn",
    )(sink_l, p, pc, lamv, g, tb)


def _ctx_attn_kernel(sink_ref, pc_ref, lam_ref, g_ref, o_ref, *, lam_init):
    t = pc_ref.shape[1]
    every = slice(0, t)
    lane = lax.broadcasted_iota(jnp.int32, (1, LANES), 1)
    left = lane < HEAD_DIM
    lam = _lam_value(lam_ref, lam_init)

    def units_of(pc, out):
        def start_a(_):
            return _dot_nt(_a_queries(pc, every), pc[0, :, _lane_group(AK)])

        def finish_a(s):
            o = _softmax_av([(s, None, pc[0, :, _lane_group(AV)])], extra_logit=_a_sink(sink_ref, t))
            _a_store(o, out, every, left)

        units = [(start_a, finish_a)]
        for h in range(B_HEADS):
            def start(_, h=h):
                return (_dot_nt(_b_queries(pc, every, h, lane), pc[0, :, _lane_group(BK + h * HEAD_DIM)]),)

            def finish(scores, h=h):
                vcol = BV + h * HEAD_DIM
                o = _diff_head_out(*_diff_head_weights(scores, lam), [pc[0, :, _lane_group(vcol)]], g_ref[...],
                                   lam_init, vcol % LANES)
                out[0, :, _head_cols(YB, h)] = o.astype(BF16)

            units.append((start, finish))

        def start_c(_):
            return _row_stack([_dot_nt(_c_queries(pc, every, j, left), pc[0, :, _lane_group(CK + j * LANES)])
                               for j in range(C_HEADS // 2)])

        def finish_c(s):
            o = _softmax_av([(s, None, [pc[0, :, _lane_group(CV + j * LANES)] for j in range(C_HEADS // 2)])])
            for j in range(C_HEADS // 2):
                oj = o[2 * j * t:2 * (j + 1) * t]
                out[0, :, YC + j * LANES:YC + (j + 1) * LANES] = jnp.where(left, oj[0:t], oj[t:2 * t]).astype(BF16)

        return units + [(start_c, finish_c)]

    units = []
    for bi in range(pc_ref.shape[0]):
        units += units_of(pc_ref.at[pl.ds(bi, 1)], o_ref.at[pl.ds(bi, 1)])
    _pipelined(units, (0, ATTN_LOOKAHEAD))


def _ctx_attn(pc, sink_l, lamv, g, lam_init):
    bsz, n_ctx, _ = pc.shape
    return pl.pallas_call(
        functools.partial(_ctx_attn_kernel, lam_init=lam_init),
        grid=(bsz // CTX_BATCH,),
        in_specs=[
            pl.BlockSpec(memory_space=pltpu.SMEM),
            pl.BlockSpec((CTX_BATCH, n_ctx, PROJ_WIDTH), lambda b: (b, 0, 0)),
            pl.BlockSpec((4, B_QK_DIM), lambda b: (0, 0)),
            pl.BlockSpec((1, HEAD_DIM), lambda b: (0, 0)),
        ],
        out_specs=pl.BlockSpec((CTX_BATCH, n_ctx, MIX_WIDTH), lambda b: (b, 0, 0)),
        out_shape=jax.ShapeDtypeStruct((bsz, n_ctx, MIX_WIDTH), BF16),
        compiler_params=pltpu.CompilerParams(vmem_limit_bytes=VMEM_LIMIT),
        name="ctx_attn",
    )(sink_l, pc, lamv, g)


N_DR = 2 * NA_KH - 1
N_DC = 2 * NA_KW - 1


def _na_table_kernel(nb_ref, o_ref):
    base = (pl.program_id(0) * C_HEADS + pl.program_id(1)) * (N_DR * N_DC)
    wq = lax.broadcasted_iota(jnp.int32, (GRID_W, LANES), 0)
    wk = lax.broadcasted_iota(jnp.int32, (GRID_W, LANES), 1) % GRID_W
    dc = jnp.clip(wk - wq, -(NA_KW - 1), NA_KW - 1) + (NA_KW - 1)
    cs = jnp.clip(wq - NA_KW // 2, 0, GRID_W - NA_KW)
    valid = (wk >= cs) & (wk < cs + NA_KW)
    for dr in range(N_DR):
        acc = jnp.zeros((GRID_W, LANES), F32)
        for c in range(N_DC):
            acc = jnp.where(dc == c, nb_ref[base + dr * N_DC + c], acc)
        o_ref[0, 0, dr] = jnp.where(valid, acc * LOG2E, NEG_INF)


def _na_table(na_bias):
    return pl.pallas_call(
        _na_table_kernel,
        grid=(DEPTH, C_HEADS),
        in_specs=[pl.BlockSpec(memory_space=pltpu.SMEM)],
        out_specs=pl.BlockSpec((1, 1, N_DR, GRID_W, LANES), lambda l, h: (l, h, 0, 0, 0)),
        out_shape=jax.ShapeDtypeStruct((DEPTH, C_HEADS, N_DR, GRID_W, LANES), F32),
        name="na_table",
    )(na_bias.reshape(-1))


def _outffn_kernel(x_ref, y_ref, g1_ref, sh2_ref, sc2_ref, g2_ref, wo_ref, ln1g_ref, ln1b_ref, ln2g_ref, ln2b_ref,
                   wg_ref, wu_ref, wd_ref, o_ref, *, sub):
    def rows(u):
        return slice(u * sub, (u + 1) * sub)

    def out_proj(u, _):
        return _dot(y_ref[0, rows(u)], wo_ref[0])

    def norm_gate_up(u, y):
        xn = _layer_norm(ALPHA * x_ref[0, rows(u)] + g1_ref[0] * y, ln1g_ref[...], ln1b_ref[...])
        h = (xn * (1.0 + sc2_ref[0]) + sh2_ref[0]).astype(BF16)
        return xn, _dot(h, wg_ref[0]), _dot(h, wu_ref[0])

    def act_down(u, st):
        xn, gate, up = st
        return xn, _dot((_silu(gate) * up).astype(BF16), wd_ref[0])

    def norm_store(u, st):
        xn, ff = st
        o_ref[0, rows(u)] = _layer_norm(ALPHA * xn + g2_ref[0] * ff, ln2g_ref[...], ln2b_ref[...])

    _staged(x_ref.shape[1] // sub, [out_proj, norm_gate_up, act_down, norm_store])


def _outffn(xs, y, mod, layer, wo_b, ln1g, ln1b, ln2g, ln2b, wfi_b, wfo_b, *, ctx):
    bsz, seq, _ = xs.shape
    t = min(FFN_TILE, seq)
    row = (lambda b: CTX_ROW) if ctx else (lambda b: b)
    once = pl.Buffered(1)

    def mod_spec(k):
        return pl.BlockSpec((1, 1, D_MODEL), lambda b, i: (row(b), 0, k))

    vec = pl.BlockSpec((1, D_MODEL), lambda b, i: (0, 0))
    return pl.pallas_call(
        functools.partial(_outffn_kernel, sub=min(SUB_TILE, t // 2)),
        grid=(bsz, seq // t),
        in_specs=[
            pl.BlockSpec((1, t, D_MODEL), lambda b, i: (b, i, 0)),
            pl.BlockSpec((1, t, MIX_WIDTH), lambda b, i: (b, i, 0)),
            mod_spec(2), mod_spec(3), mod_spec(4), mod_spec(5),
            pl.BlockSpec((1, MIX_WIDTH, D_MODEL), lambda b, i: (layer, 0, 0), pipeline_mode=once),
            vec, vec, vec, vec,
            pl.BlockSpec((1, D_MODEL, D_FF), lambda b, i: (layer, 0, 0), pipeline_mode=once),
            pl.BlockSpec((1, D_MODEL, D_FF), lambda b, i: (layer, 0, 1), pipeline_mode=once),
            pl.BlockSpec((1, D_FF, D_MODEL), lambda b, i: (layer, 0, 0), pipeline_mode=once),
        ],
        out_specs=pl.BlockSpec((1, t, D_MODEL), lambda b, i: (b, i, 0)),
        out_shape=jax.ShapeDtypeStruct((bsz, seq, D_MODEL), F32),
        compiler_params=pltpu.CompilerParams(vmem_limit_bytes=VMEM_LIMIT),
        name="outffn_ctx" if ctx else "outffn",
    )(xs, y, mod, mod, mod, mod, wo_b, ln1g, ln1b, ln2g, ln2b, wfi_b, wfi_b, wfo_b)


def kernel(x, c, ctx, c_ctx, w_ada, b_ada, w_in, w_o, sink, lam_q1, lam_k1, lam_q2, lam_k2, subln_g, na_bias,
           ln1_g, ln1_b, w_ffn_in, w_ffn_out, ln2_g, ln2_b):
    bsz, seq, _ = x.shape
    assert x.shape == (bsz, seq, D_MODEL) and seq % Q_BLK == 0 and seq // Q_BLK >= 3
    assert bsz < CTX_ROW + 1 <= MOD_ROWS and bsz % CTX_BATCH == 0
    cc = jnp.zeros((MOD_ROWS, D_MODEL), F32).at[:bsz].set(c).at[CTX_ROW].set(c_ctx)
    mod_all = _ada(cc, w_ada, b_ada).reshape(DEPTH, MOD_ROWS, 1, N_MOD * D_MODEL)
    tables = _rope_tables(seq)
    na_tab = _na_table(na_bias)
    w_in_b, wo_b, wfi_b, wfo_b = (_cast_bf16(w) for w in (w_in, w_o, w_ffn_in, w_ffn_out))

    xs, cs = x, ctx
    for l in range(DEPTH):
        last = l == DEPTH - 1
        lam_init = 0.8 - 0.6 * math.exp(-0.3 * l)
        mod = mod_all[l]
        lamv = jnp.stack([lam_q1[l], lam_k1[l], lam_q2[l], lam_k2[l]])
        g = subln_g[l].reshape(1, HEAD_DIM)
        lnp = [v[l].reshape(1, D_MODEL) for v in (ln1_g, ln1_b, ln2_g, ln2_b)]

        p = _inproj(xs, mod, w_in_b, l, tables, ctx=False)
        pc = _inproj(cs, mod, w_in_b, l, None, ctx=True)
        y = _attn(p, pc, sink[l], lamv, g, na_tab[l], lam_init)
        xn = _outffn(xs, y, mod, l, wo_b, *lnp, wfi_b, wfo_b, ctx=False)
        if not last:
            yc = _ctx_attn(pc, sink[l], lamv, g, lam_init)
            cs = _outffn(cs, yc, mod, l, wo_b, *lnp, wfi_b, wfo_b, ctx=True)
        xs = xn
    return xs
```
